```python
import jax
import jax.numpy as jnp
from jax import lax
import numpy as np

D_MODEL = 2048
BATCH = 16
SEQ = 256
DEPTH = 4
DEC_BATCH = 2
DEC_SEQ = 2048
PAST_LEN = 256

GRID_W = 64
EPS = 1e-6
N_MOD = 6

A_WIDTH = D_MODEL // 2
A_DK = 128
A_DV = 128
A_HEADS = A_WIDTH // A_DV
A_QK = A_HEADS * A_DK
HGRN_CHUNK = 32

B_WIDTH = D_MODEL // 2
B_BLOCKS = 8
B_BLOCK = B_WIDTH // B_BLOCKS
LRU_C = 8.0

C_INNER = D_MODEL // 2
C_HEADDIM = 64
C_HEADS = C_INNER // C_HEADDIM
C_GROUPS = 4
C_STATE = 128
C_CONV = C_INNER + 2 * C_GROUPS * C_STATE
SSD_CHUNK = 64

CONV_W = 4
CONV_LEFT = CONV_W // 2

N_EXPERTS = 16
N_EXPERT_GROUPS = 4
EXPERTS_PER_GROUP = N_EXPERTS // N_EXPERT_GROUPS
TOP_K = 2
D_EXPERT = D_MODEL // 2

IN_SIZES = (A_QK, A_QK, A_QK, A_WIDTH, A_WIDTH, B_WIDTH, B_WIDTH, C_INNER, C_CONV, 2 * C_HEADS, 3 * D_MODEL)
D_IN = 3 * A_QK + 2 * A_WIDTH + 2 * B_WIDTH + C_INNER + C_CONV + 2 * C_HEADS + 3 * D_MODEL

kernel_name = 'hybrid_prefix_diffusion_step'


def rmsnorm(x, g):
    xf = x.astype(jnp.float32)
    xf = xf * lax.rsqrt(jnp.mean(xf * xf, axis=-1, keepdims=True) + EPS)
    return (xf * g.astype(jnp.float32)).astype(x.dtype)


def dwconv(x, w, b, on_grid):
    bsz, t, ch = x.shape
    seq = x.reshape(bsz * (t // GRID_W), GRID_W, ch) if on_grid else x
    n = seq.shape[1]
    xp = jnp.pad(seq, ((0, 0), (CONV_LEFT, CONV_W - 1 - CONV_LEFT), (0, 0)))
    y = b + xp[:, 0:n] * w[0]
    for k in range(1, CONV_W):
        y = y + xp[:, k:k + n] * w[k]
    return y.reshape(bsz, t, ch)


def hgrn2_scan(q, k, lf, v, s0):
    bsz, t, nh, dk = q.shape
    dv = v.shape[-1]
    n = t // HGRN_CHUNK

    def blocks(a):
        return a.reshape(bsz, n, HGRN_CHUNK, nh, a.shape[-1]).transpose(1, 0, 3, 2, 4)

    causal = jnp.tril(jnp.ones((HGRN_CHUNK, HGRN_CHUNK), dtype=bool))[:, :, None]

    def step(s, blk):
        qc, kc, lfc, vc = blk
        b = jnp.cumsum(lfc, axis=2)
        o_inter = jnp.einsum('bhtk,bhkv->bhtv', qc * jnp.exp(b), s)
        rel = jnp.exp(jnp.where(causal, b[:, :, :, None, :] - b[:, :, None, :, :], -jnp.inf))
        att = jnp.sum(qc[:, :, :, None, :] * rel * kc[:, :, None, :, :], axis=-1)
        o_intra = jnp.einsum('bhtj,bhjv->bhtv', att, vc)
        b_end = b[:, :, -1:, :]
        s = jnp.exp(b_end)[:, :, 0, :, None] * s + jnp.einsum('bhjk,bhjv->bhkv', kc * jnp.exp(b_end - b), vc)
        return s, o_inter + o_intra

    s_end, o = lax.scan(step, s0.astype(jnp.float32), (blocks(q), blocks(k), blocks(lf), blocks(v)))
    return o.transpose(1, 0, 3, 2, 4).reshape(bsz, t, nh, dv), s_end


def ssd_scan(x, dt, a, bm, cm, s0):
    bsz, t, nh, hp = x.shape
    n = t // SSD_CHUNK

    def blocks(z):
        return z.reshape((bsz, n, SSD_CHUNK) + z.shape[2:]).swapaxes(0, 1)

    causal = jnp.tril(jnp.ones((SSD_CHUNK, SSD_CHUNK), dtype=bool))[:, :, None]

    def step(s, blk):
        xc, dtc, bc, cc = blk
        cum = jnp.cumsum(dtc * a, axis=1)
        seg = jnp.exp(jnp.where(causal, cum[:, :, None, :] - cum[:, None, :, :], -jnp.inf))
        xdt = xc * dtc[..., None]
        scores = jnp.einsum('bthn,bjhn->btjh', cc, bc) * seg
        y = jnp.einsum('btjh,bjhp->bthp', scores, xdt)
        y = y + jnp.einsum('bthn,bhpn->bthp', cc, s) * jnp.exp(cum)[..., None]
        end = cum[:, -1, :]
        s = jnp.exp(end)[:, :, None, None] * s + jnp.einsum(
            'bjhn,bjhp->bhpn', bc * jnp.exp(end[:, None, :] - cum)[..., None], xdt)
        return s, y

    s_end, y = lax.scan(step, s0.astype(jnp.float32), (blocks(x), blocks(dt), blocks(bm), blocks(cm)))
    return y.swapaxes(0, 1).reshape(bsz, t, nh, hp), s_end


def _linear_recurrence_combine(left, right):
    a_l, u_l = left
    a_r, u_r = right
    return a_l * a_r, a_r * u_l + u_r


def rglru_scan(x, r, i, lam, h0):
    log_a = LRU_C * r * jax.nn.log_sigmoid(lam.astype(jnp.float32))
    a = jnp.exp(log_a)
    u = jnp.sqrt(-jnp.expm1(2.0 * log_a)) * (i * x)
    u = u.at[:, 0].add(a[:, 0] * h0.astype(jnp.float32))
    _, h = lax.associative_scan(_linear_recurrence_combine, (a, u), axis=1)
    return h, h[:, -1]


def hgrn_lower_bounds(lb_raw):
    p = jax.nn.softmax(lb_raw.astype(jnp.float32), axis=0)
    cum = jnp.cumsum(p, axis=0)
    return cum - cum[0]


def token_mix(h, lp, lb, s_hg, s_lru, s_ssd, on_grid):
    f32 = jnp.float32
    cdt = h.dtype
    bsz, t, _ = h.shape
    flip = lambda z: jnp.flip(z, axis=1)
    proj = h @ lp['w_in']
    idx, acc = [], 0
    for size in IN_SIZES[:-1]:
        acc += size
        idx.append(acc)
    q, f_fw, f_bw, v, g_a, x_b, g_b, z, xbc, dt_raw, merge = jnp.split(proj, idx, axis=-1)

    q = (jax.nn.silu(q.astype(f32)) * A_DK ** -0.5).reshape(bsz, t, A_HEADS, A_DK)
    v = v.astype(f32).reshape(bsz, t, A_HEADS, A_DV)

    def forget(fr):
        f = lb + (1.0 - lb) * jax.nn.sigmoid(fr.astype(f32))
        return (1.0 - f).reshape(bsz, t, A_HEADS, A_DK), jnp.log(f).reshape(bsz, t, A_HEADS, A_DK)

    k_f, lf_f = forget(f_fw)
    k_b, lf_b = forget(f_bw)
    o_f, hg_f = hgrn2_scan(q, k_f, lf_f, v, s_hg[:, 0])
    o_r, hg_b = hgrn2_scan(flip(q), flip(k_b), flip(lf_b), flip(v), s_hg[:, 1])
    o_a = rmsnorm(o_f + flip(o_r), lp['hgrn_norm_g']).reshape(bsz, t, A_WIDTH) * jax.nn.silu(g_a.astype(f32))

    x_b = dwconv(x_b, lp['conv_b_w'], lp['conv_b_b'], on_grid).astype(f32)

    def lru_gates(xs, d):
        blk = xs.reshape(bsz, t, B_BLOCKS, B_BLOCK)
        r = jnp.einsum('btnk,nkj->btnj', blk, lp['lru_wa'][d]).reshape(bsz, t, B_WIDTH) + lp['lru_ba'][d]
        i = jnp.einsum('btnk,nkj->btnj', blk, lp['lru_wx'][d]).reshape(bsz, t, B_WIDTH) + lp['lru_bx'][d]
        return jax.nn.sigmoid(r.astype(f32)), jax.nn.sigmoid(i.astype(f32))

    r_f, i_f = lru_gates(x_b, 0)
    h_f, lru_f = rglru_scan(x_b, r_f, i_f, lp['lru_lambda'][0], s_lru[:, 0])
    x_r = flip(x_b)
    r_b, i_b = lru_gates(x_r, 1)
    h_r, lru_b = rglru_scan(x_r, r_b, i_b, lp['lru_lambda'][1], s_lru[:, 1])
    o_b = (h_f + flip(h_r)) * jax.nn.gelu(g_b.astype(f32))

    xbc = jax.nn.silu(dwconv(xbc, lp['conv_c_w'], lp['conv_c_b'], on_grid).astype(f32))
    xs, bm, cm = jnp.split(xbc, [C_INNER, C_INNER + C_GROUPS * C_STATE], axis=-1)
    xs = xs.reshape(bsz, t, C_HEADS, C_HEADDIM)
    rep = C_HEADS // C_GROUPS
    bm = jnp.repeat(bm.reshape(bsz, t, C_GROUPS, C_STATE), rep, axis=2)
    cm = jnp.repeat(cm.reshape(bsz, t, C_GROUPS, C_STATE), rep, axis=2)
    dt = jax.nn.softplus(dt_raw.astype(f32).reshape(bsz, t, 2, C_HEADS) + lp['ssd_dt_bias'])
    a = -jnp.exp(lp['ssd_a_log'].astype(f32))
    y_f, ssd_f = ssd_scan(xs, dt[:, :, 0], a[0], bm, cm, s_ssd[:, 0])
    y_r, ssd_b = ssd_scan(flip(xs), flip(dt[:, :, 1]), a[1], flip(bm), flip(cm), s_ssd[:, 1])
    y = y_f + flip(y_r) + lp['ssd_d'][:, None] * xs
    y = y.reshape(bsz, t, C_INNER) * jax.nn.silu(z.astype(f32))
    o_c = rmsnorm(y.reshape(bsz, t, C_GROUPS, C_INNER // C_GROUPS),
                  lp['ssd_norm_g'].reshape(C_GROUPS, C_INNER // C_GROUPS)).reshape(bsz, t, C_INNER)

    g1, g2, g3 = jnp.split(jax.nn.sigmoid(merge), 3, axis=-1)
    merged = (g1 * (o_a.astype(cdt) @ lp['w_branch_a'])
              + g2 * (o_b.astype(cdt) @ lp['w_branch_b'])
              + g3 * (o_c.astype(cdt) @ lp['w_branch_c']))
    out = (merged @ lp['w_out']).astype(cdt)
    new_hg = jnp.stack([hg_f, hg_b], axis=1)
    new_lru = jnp.stack([lru_f, lru_b], axis=1)
    new_ssd = jnp.stack([ssd_f, ssd_b], axis=1)
    return out, new_hg, new_lru, new_ssd


def moe(h, w_router, router_bias, w_gate, w_up, w_down):
    shp = h.shape
    tok = h.reshape(-1, shp[-1])
    n = tok.shape[0]
    probs = jax.nn.softmax((tok @ w_router).astype(jnp.float32), axis=-1)
    sel = (probs + router_bias.astype(jnp.float32)).reshape(n, N_EXPERT_GROUPS, EXPERTS_PER_GROUP)
    group_score = jnp.sum(lax.top_k(sel, TOP_K)[0], axis=-1)
    best = jnp.argmax(group_score, axis=-1)
    in_group = (best[:, None] == jnp.arange(N_EXPERT_GROUPS))[:, :, None]
    masked = jnp.where(in_group, sel, -jnp.inf).reshape(n, N_EXPERTS)
    _, idx = lax.top_k(masked, TOP_K)
    w = jnp.take_along_axis(probs, idx, axis=-1)
    w = w / jnp.sum(w, axis=-1, keepdims=True)
    gate = jnp.einsum('nk,nke->ne', w, jax.nn.one_hot(idx, N_EXPERTS, dtype=jnp.float32))
    hg = jnp.einsum('nd,edf->nef', tok, w_gate)
    hu = jnp.einsum('nd,edf->nef', tok, w_up)
    act = jax.nn.silu(hg) * hu * gate[:, :, None].astype(hg.dtype)
    return jnp.einsum('nef,efd->nd', act, w_down).reshape(shp).astype(h.dtype)


def modulation(cond, w_mod_l, b_mod_l):
    m = jax.nn.silu(cond) @ w_mod_l + b_mod_l
    return jnp.split(m[:, None, :], N_MOD, axis=-1)


def trunk_layer(x, cond, lp, lb, w_router, router_bias, s_hg, s_lru, s_ssd, on_grid):
    sh1, sc1, ga1, sh2, sc2, ga2 = modulation(cond, lp['w_mod'], lp['b_mod'])
    h = rmsnorm(x, lp['norm1_g']) * (1.0 + sc1) + sh1
    y, s_hg, s_lru, s_ssd = token_mix(h, lp, lb, s_hg, s_lru, s_ssd, on_grid)
    x = x + ga1 * y
    h = rmsnorm(x, lp['norm2_g']) * (1.0 + sc2) + sh2
    x = x + ga2 * moe(h, w_router, router_bias, lp['w_e_gate'], lp['w_e_up'], lp['w_e_down'])
    return x, s_hg, s_lru, s_ssd


def setup_inputs(seed: int = 0) -> dict:
    key = jax.random.key(seed)
    keys = jax.random.split(key, 40)
    f32 = jnp.float32

    def nrm(i, shape, scale):
        return jax.random.normal(keys[i], shape, f32) * scale

    def gain(i, shape, scale=0.01):
        return 1.0 + nrm(i, shape, scale)

    a0 = jax.random.uniform(keys[30], (DEPTH, 2, B_WIDTH), f32, 0.9, 0.999)
    root = a0 ** (1.0 / LRU_C)
    lru_lambda = jnp.log(root) - jnp.log1p(-root)
    dt0 = jnp.exp(jax.random.uniform(keys[31], (DEPTH, 2, C_HEADS), f32, float(np.log(1e-3)), float(np.log(1e-1))))
    ssd_dt_bias = dt0 + jnp.log(-jnp.expm1(-dt0))
    ssd_a_log = jnp.log(jax.random.uniform(keys[32], (DEPTH, 2, C_HEADS), f32, 1.0, 16.0))
    return {
        'x_prompt': nrm(0, (BATCH, SEQ, D_MODEL), 1.0),
        'x_sample': nrm(1, (DEC_BATCH, DEC_SEQ, D_MODEL), 1.0),
        'state_hgrn': nrm(2, (DEC_BATCH, DEPTH, 2, A_HEADS, A_DK, A_DV), 0.3),
        'state_rglru': nrm(3, (DEC_BATCH, DEPTH, 2, B_WIDTH), 0.5),
        'state_ssd': nrm(4, (DEC_BATCH, DEPTH, 2, C_HEADS, C_HEADDIM, C_STATE), 0.1),
        'c': nrm(5, (DEC_BATCH, D_MODEL), 1.0),
        'c_ctx': nrm(6, (D_MODEL,), 1.0),
        'w_mod': nrm(7, (DEPTH, D_MODEL, N_MOD * D_MODEL), 0.5 * D_MODEL ** -0.5),
        'b_mod': nrm(8, (DEPTH, N_MOD * D_MODEL), 0.01),
        'norm1_g': gain(9, (DEPTH, D_MODEL)),
        'norm2_g': gain(10, (DEPTH, D_MODEL)),
        'w_in': nrm(11, (DEPTH, D_MODEL, D_IN), D_MODEL ** -0.5),
        'hgrn_lb': nrm(12, (DEPTH, A_QK), 0.1),
        'hgrn_norm_g': gain(13, (DEPTH, A_DV)),
        'conv_b_w': nrm(14, (DEPTH, CONV_W, B_WIDTH), CONV_W ** -0.5),
        'conv_b_b': nrm(15, (DEPTH, B_WIDTH), 0.01),
        'lru_wa': nrm(16, (DEPTH, 2, B_BLOCKS, B_BLOCK, B_BLOCK), B_BLOCK ** -0.5),
        'lru_ba': nrm(17, (DEPTH, 2, B_WIDTH), 0.01),
        'lru_wx': nrm(18, (DEPTH, 2, B_BLOCKS, B_BLOCK, B_BLOCK), B_BLOCK ** -0.5),
        'lru_bx': nrm(19, (DEPTH, 2, B_WIDTH), 0.01),
        'lru_lambda': lru_lambda,
        'conv_c_w': nrm(20, (DEPTH, CONV_W, C_CONV), CONV_W ** -0.5),
        'conv_c_b': nrm(21, (DEPTH, C_CONV), 0.01),
        'ssd_a_log': ssd_a_log,
        'ssd_dt_bias': ssd_dt_bias,
        'ssd_d': gain(22, (DEPTH, C_HEADS), 0.1),
        'ssd_norm_g': gain(23, (DEPTH, C_INNER)),
        'w_branch_a': nrm(24, (DEPTH, A_WIDTH, D_MODEL), A_WIDTH ** -0.5),
        'w_branch_b': nrm(25, (DEPTH, B_WIDTH, D_MODEL), B_WIDTH ** -0.5),
        'w_branch_c': nrm(26, (DEPTH, C_INNER, D_MODEL), C_INNER ** -0.5),
        'w_out': nrm(27, (DEPTH, D_MODEL, D_MODEL), D_MODEL ** -0.5),
        'w_router': nrm(28, (D_MODEL, N_EXPERTS), D_MODEL ** -0.5),
        'router_bias': nrm(29, (N_EXPERTS,), 0.01),
        'w_e_gate': nrm(33, (DEPTH, N_EXPERTS, D_MODEL, D_EXPERT), D_MODEL ** -0.5),
        'w_e_up': nrm(34, (DEPTH, N_EXPERTS, D_MODEL, D_EXPERT), D_MODEL ** -0.5),
        'w_e_down': nrm(35, (DEPTH, N_EXPERTS, D_EXPERT, D_MODEL), D_EXPERT ** -0.5),
        'final_g': gain(36, (D_MODEL,)),
    }


def reference(x_prompt, x_sample, state_hgrn, state_rglru, state_ssd, c, c_ctx, w_mod, b_mod, norm1_g, norm2_g,
              w_in, hgrn_lb, hgrn_norm_g, conv_b_w, conv_b_b, lru_wa, lru_ba, lru_wx, lru_bx, lru_lambda,
              conv_c_w, conv_c_b, ssd_a_log, ssd_dt_bias, ssd_d, ssd_norm_g, w_branch_a, w_branch_b, w_branch_c,
              w_out, w_router, router_bias, w_e_gate, w_e_up, w_e_down, final_g):
    per_layer = dict(w_mod=w_mod, b_mod=b_mod, norm1_g=norm1_g, norm2_g=norm2_g, w_in=w_in,
                     hgrn_norm_g=hgrn_norm_g, conv_b_w=conv_b_w, conv_b_b=conv_b_b, lru_wa=lru_wa,
                     lru_ba=lru_ba, lru_wx=lru_wx, lru_bx=lru_bx, lru_lambda=lru_lambda, conv_c_w=conv_c_w,
                     conv_c_b=conv_c_b, ssd_a_log=ssd_a_log, ssd_dt_bias=ssd_dt_bias, ssd_d=ssd_d,
                     ssd_norm_g=ssd_norm_g, w_branch_a=w_branch_a, w_branch_b=w_branch_b,
                     w_branch_c=w_branch_c, w_out=w_out, w_e_gate=w_e_gate, w_e_up=w_e_up, w_e_down=w_e_down)
    lbs = hgrn_lower_bounds(hgrn_lb)

    bsz = x_prompt.shape[0]
    zero_hg = jnp.zeros((bsz, 2, A_HEADS, A_DK, A_DV), jnp.float32)
    zero_lru = jnp.zeros((bsz, 2, B_WIDTH), jnp.float32)
    zero_ssd = jnp.zeros((bsz, 2, C_HEADS, C_HEADDIM, C_STATE), jnp.float32)
    cond_ctx = c_ctx[None, :]
    x = x_prompt
    hg_list, lru_list, ssd_list = [], [], []
    for l in range(DEPTH):
        lp = {name: arr[l] for name, arr in per_layer.items()}
        x, s_hg, s_lru, s_ssd = trunk_layer(x, cond_ctx, lp, lbs[l], w_router, router_bias,
                                            zero_hg, zero_lru, zero_ssd, False)
        hg_list.append(s_hg)
        lru_list.append(s_lru)
        ssd_list.append(s_ssd)
    y_prompt = rmsnorm(x, final_g)
    new_state_hgrn = jnp.stack(hg_list, axis=1)
    new_state_rglru = jnp.stack(lru_list, axis=1)
    new_state_ssd = jnp.stack(ssd_list, axis=1)

    x = x_sample
    for l in range(DEPTH):
        lp = {name: arr[l] for name, arr in per_layer.items()}
        x, _, _, _ = trunk_layer(x, c, lp, lbs[l], w_router, router_bias,
                                 state_hgrn[:, l], state_rglru[:, l], state_ssd[:, l], True)
    y_sample = rmsnorm(x, final_g)
    return (y_prompt, y_sample, new_state_hgrn, new_state_rglru, new_state_ssd)
```

```python
import functools
import math

import numpy as np
import jax
import jax.numpy as jnp
from jax import lax
from jax.experimental import pallas as pl
from jax.experimental.pallas import tpu as pltpu

F32 = jnp.float32
BF16 = jnp.bfloat16
HIGHEST = lax.Precision.HIGHEST

EPS = 1e-6
D_MODEL = 2048
DEPTH = 4
N_MOD = 6
GRID_W = 64
SEQ = 256
TOK_BLOCK = 2048
N_CTX_BLOCKS = 2

A_HEADS = 8
A_DK = 128
A_DV = 128
A_WIDTH = 1024
HGRN_C = 64

B_WIDTH = 1024
B_BLOCK = 128
LRU_C = 8.0
LRU_SEG = 256

C_INNER = 1024
C_HEADDIM = 64
C_HEADS = 16
C_GROUPS = 4
C_STATE = 128
SSD_L = 128
CONV_W = 4

N_EXPERTS = 16
N_EXPERT_GROUPS = 4
D_EXPERT = 1024
MOE_TM = 1024
MOE_TF = 256

COL_Q, COL_FF, COL_FB, COL_V, COL_GA = 0, 1024, 2048, 3072, 4096
COL_XB, COL_GB, COL_Z, COL_XBC, COL_DT, COL_MERGE = 5120, 6144, 7168, 8192, 10240, 10272
N_MAIN = 10240

VMEM_LIMIT_BYTES = 56 * 1024 * 1024
NEG_BIG = -1e30


def _cp(*sem):
    return pltpu.CompilerParams(dimension_semantics=sem, vmem_limit_bytes=VMEM_LIMIT_BYTES)


def _silu(x):
    return x * jax.nn.sigmoid(x)


def _dot(a, b):
    return jnp.dot(a, b, preferred_element_type=F32)


def _dot_nt(a, b):
    return lax.dot_general(a, b, (((1,), (1,)), ((), ())), preferred_element_type=F32)


def _dot_tn(a, b):
    return lax.dot_general(a, b, (((0,), (0,)), ((), ())), preferred_element_type=F32)


def _dot_hi(a, b):
    return jnp.dot(a, b, preferred_element_type=F32, precision=HIGHEST)


def _cond_row(i, tm):
    return jnp.maximum((i * tm) // TOK_BLOCK - (N_CTX_BLOCKS - 1), 0)


def _mod_kernel(c_ref, w_ref, b_ref, o_ref):
    s = _silu(c_ref[...]).astype(BF16)
    o_ref[...] = _dot(s, w_ref[...].astype(BF16)) + b_ref[...]


def _modulation(cond8, w_mod, b_mod):
    depth, d, n = w_mod.shape
    tn = 1024
    return pl.pallas_call(
        _mod_kernel,
        grid=(depth, n // tn),
        in_specs=[pl.BlockSpec((8, d), lambda l, j: (0, 0)),
                  pl.BlockSpec((None, d, tn), lambda l, j: (l, 0, j)),
                  pl.BlockSpec((None, 1, tn), lambda l, j: (l, 0, j))],
        out_specs=pl.BlockSpec((None, 8, tn), lambda l, j: (l, 0, j)),
        out_shape=jax.ShapeDtypeStruct((depth, 8, n), F32),
        compiler_params=_cp("arbitrary", "arbitrary"),
        name="modulation",
    )(cond8, w_mod, b_mod.reshape(depth, 1, n))


def _normmod_kernel(*refs, modulate, router):
    it = iter(refs)
    x_ref, g_ref = next(it), next(it)
    sc_ref = sh_ref = wr_ref = lg_ref = None
    if modulate:
        sc_ref, sh_ref = next(it), next(it)
    if router:
        wr_ref = next(it)
    h_ref = next(it)
    if router:
        lg_ref = next(it)
    x = x_ref[...]
    h = x * lax.rsqrt(jnp.mean(x * x, axis=-1, keepdims=True) + EPS) * g_ref[...]
    if modulate:
        h = h * (1.0 + sc_ref[...]) + sh_ref[...]
    h_ref[...] = h.astype(h_ref.dtype)
    if router:
        lg_ref[...] = _dot_hi(h, wr_ref[...])


def _normmod(x, g, mod_l=None, chunk=None, w_router=None, out_dtype=BF16, tm=512):
    n, d = x.shape
    modulate = mod_l is not None
    router = w_router is not None
    in_specs = [pl.BlockSpec((tm, d), lambda i: (i, 0)),
                pl.BlockSpec((1, d), lambda i: (0, 0))]
    args = [x, g.reshape(1, d)]
    if modulate:
        sh_c, sc_c = chunk
        in_specs += [pl.BlockSpec((None, 1, d), lambda i: (_cond_row(i, tm), 0, sc_c)),
                     pl.BlockSpec((None, 1, d), lambda i: (_cond_row(i, tm), 0, sh_c))]
        args += [mod_l, mod_l]
    out_specs = [pl.BlockSpec((tm, d), lambda i: (i, 0))]
    out_shape = [jax.ShapeDtypeStruct((n, d), out_dtype)]
    if router:
        ne = w_router.shape[1]
        in_specs.append(pl.BlockSpec((d, ne), lambda i: (0, 0)))
        args.append(w_router)
        out_specs.append(pl.BlockSpec((tm, ne), lambda i: (i, 0)))
        out_shape.append(jax.ShapeDtypeStruct((n, ne), F32))
    res = pl.pallas_call(
        functools.partial(_normmod_kernel, modulate=modulate, router=router),
        grid=(n // tm,),
        in_specs=in_specs, out_specs=out_specs, out_shape=out_shape,
        compiler_params=_cp("arbitrary"),
        name="normmod",
    )(*args)
    return res if router else res[0]


def _mm_kernel(a_ref, w_ref, o_ref, *, act):
    acc = _dot(a_ref[...], w_ref[...].astype(BF16))
    if act == "sigmoid":
        acc = jax.nn.sigmoid(acc)
    o_ref[...] = acc.astype(o_ref.dtype)


def _mm(a, w, layer, col0, n_out, act=None, out_dtype=F32, tm=2048, tn=512):
    m, k = a.shape
    tn = min(tn, n_out)
    off = col0 // tn
    assert col0 % tn == 0 and n_out % tn == 0 and m % tm == 0
    if layer is None:
        w_spec = pl.BlockSpec((k, tn), lambda i, j: (0, j + off))
    else:
        w_spec = pl.BlockSpec((None, k, tn), lambda i, j: (layer, 0, j + off))
    return pl.pallas_call(
        functools.partial(_mm_kernel, act=act),
        grid=(m // tm, n_out // tn),
        in_specs=[pl.BlockSpec((tm, k), lambda i, j: (i, 0)), w_spec],
        out_specs=pl.BlockSpec((tm, tn), lambda i, j: (i, j)),
        out_shape=jax.ShapeDtypeStruct((m, n_out), out_dtype),
        compiler_params=_cp("arbitrary", "arbitrary"),
        name="matmul",
    )(a, w)


def _mm_res_kernel(a_ref, w_ref, x_ref, ga_ref, o_ref):
    acc = _dot(a_ref[...], w_ref[...].astype(BF16))
    o_ref[...] = x_ref[...] + ga_ref[...] * acc


def _mm_residual(a, w, layer, x, mod_l, ga_chunk, tm=1024, tn=512):
    m, k = a.shape
    n = w.shape[-1]
    gs = n // tn
    return pl.pallas_call(
        _mm_res_kernel,
        grid=(m // tm, n // tn),
        in_specs=[pl.BlockSpec((tm, k), lambda i, j: (i, 0)),
                  pl.BlockSpec((None, k, tn), lambda i, j: (layer, 0, j)),
                  pl.BlockSpec((tm, tn), lambda i, j: (i, j)),
                  pl.BlockSpec((None, 1, tn), lambda i, j: (_cond_row(i, tm), 0, ga_chunk * gs + j))],
        out_specs=pl.BlockSpec((tm, tn), lambda i, j: (i, j)),
        out_shape=jax.ShapeDtypeStruct((m, n), F32),
        compiler_params=_cp("arbitrary", "arbitrary"),
        name="out_proj",
    )(a, w, x, mod_l)


def _merge_kernel(oa_ref, ob_ref, oc_ref, wa_ref, wb_ref, wc_ref, g1_ref, g2_ref, g3_ref, o_ref):
    ya = _dot(oa_ref[...], wa_ref[...].astype(BF16))
    yb = _dot(ob_ref[...], wb_ref[...].astype(BF16))
    yc = _dot(oc_ref[...], wc_ref[...].astype(BF16))
    m = (g1_ref[...].astype(F32) * ya + g2_ref[...].astype(F32) * yb + g3_ref[...].astype(F32) * yc)
    o_ref[...] = m.astype(o_ref.dtype)


def _merge(oa, ob, oc, wa, wb, wc, gates, layer, tm=1024, tn=512):
    m, k = oa.shape
    n = wa.shape[-1]
    gs = n // tn
    a_spec = pl.BlockSpec((tm, k), lambda i, j: (i, 0))
    w_spec = pl.BlockSpec((None, k, tn), lambda i, j: (layer, 0, j))
    return pl.pallas_call(
        _merge_kernel,
        grid=(m // tm, n // tn),
        in_specs=[a_spec, a_spec, a_spec, w_spec, w_spec, w_spec,
                  pl.BlockSpec((tm, tn), lambda i, j: (i, j)),
                  pl.BlockSpec((tm, tn), lambda i, j: (i, j + gs)),
                  pl.BlockSpec((tm, tn), lambda i, j: (i, j + 2 * gs))],
        out_specs=pl.BlockSpec((tm, tn), lambda i, j: (i, j)),
        out_shape=jax.ShapeDtypeStruct((m, n), BF16),
        compiler_params=_cp("arbitrary", "arbitrary"),
        name="merge",
    )(oa, ob, oc, wa, wb, wc, gates, gates, gates)


def _dwconv(x, w, b, row_len):
    t = x.shape[0]
    pos = lax.broadcasted_iota(jnp.int32, x.shape, 0) % row_len
    xm2 = jnp.where(pos >= 2, pltpu.roll(x, 2, axis=0), 0.0)
    xm1 = jnp.where(pos >= 1, pltpu.roll(x, 1, axis=0), 0.0)
    xp1 = jnp.where(pos <= row_len - 2, pltpu.roll(x, t - 1, axis=0), 0.0)
    y = b + xm2 * w[0:1]
    y = y + xm1 * w[1:2]
    y = y + x * w[2:3]
    y = y + xp1 * w[3:4]
    return y


def _hgrn_consts(c):
    nl = int(math.log2(c))
    seg = np.zeros((2, nl + 1, c, c), np.float32)
    msk = np.zeros((2, nl + 1, c, c), np.float32)
    up = np.zeros((2, nl, c, 128), np.float32)
    for d in range(2):
        tt = np.arange(c) if d == 0 else c - 1 - np.arange(c)
        tr, tc = tt[:, None], tt[None, :]
        seg[d, 0] = tc <= tr
        for l in range(nl):
            s = 1 << l
            blk, upper = tt // (2 * s), (tt % (2 * s)) >= s
            mid = (blk * 2 * s + s)[:, None]
            seg_u = (tc >= mid) & (tc <= tr)
            seg_l = (tc >= tr + 1) & (tc <= mid - 1)
            seg[d, l + 1] = np.where(upper[:, None], seg_u, seg_l)
            msk[d, l] = upper[:, None] & (~upper)[None, :] & (blk[:, None] == blk[None, :])
            up[d, l] = upper[:, None]
        msk[d, nl] = np.eye(c)
    return seg.reshape(2, (nl + 1) * c, c), msk, up


def _hgrn_kernel(*refs, seq_len, chained):
    it = iter(refs)
    q_ref, ff_ref, fb_ref, v_ref, ga_ref, lb_ref, ng_ref = (next(it) for _ in range(7))
    seg_ref, msk_ref, up_ref = next(it), next(it), next(it)
    s0_ref = next(it) if chained else None
    o_ref = next(it)
    sout_ref = None if chained else next(it)
    of_scr, ob_scr, st_scr = next(it), next(it), next(it)

    c = HGRN_C
    nl = int(math.log2(c))
    tb = q_ref.shape[0]
    n_seq = tb // seq_len
    n_ch = seq_len // c
    lb = lb_ref[...]
    f_refs = (ff_ref, fb_ref)
    o_scrs = (of_scr, ob_scr)

    def chunk(d, r0):
        q = _silu(q_ref[pl.ds(r0, c), :]) * (A_DK ** -0.5)
        v = v_ref[pl.ds(r0, c), :].astype(BF16)
        f = lb + (1.0 - lb) * jax.nn.sigmoid(f_refs[d][pl.ds(r0, c), :])
        k = 1.0 - f
        lf = jnp.log(f)
        e = _dot_hi(seg_ref[d], lf)
        b = e[0:c]
        att = jnp.zeros((c, c), F32)
        for l in range(nl):
            x = (jnp.where(up_ref[d, l] > 0.5, q, k) * jnp.exp(e[(l + 1) * c:(l + 2) * c])).astype(BF16)
            att = att + msk_ref[d, l] * _dot_nt(x, x)
        att = att + msk_ref[d, nl] * _dot_nt(q.astype(BF16), k.astype(BF16))
        o = _dot(att.astype(BF16), v)
        b_end = b[c - 1:c] if d == 0 else b[0:1]
        st = st_scr[d]
        o = o + _dot_nt((q * jnp.exp(b)).astype(BF16), st.astype(BF16))
        kp = (k * jnp.exp(b_end - b)).astype(BF16)
        st_scr[d] = st * jnp.exp(b_end) + _dot_tn(v, kp)
        o_scrs[d][pl.ds(r0, c), :] = o

    def seq_body(s, carry):
        for d in range(2):
            st_scr[d] = s0_ref[d].T if chained else jnp.zeros((A_DV, A_DK), F32)

        def ch_body(i, carry2):
            base = s * seq_len
            chunk(0, pl.multiple_of(base + i * c, c))
            chunk(1, pl.multiple_of(base + (n_ch - 1 - i) * c, c))
            return carry2

        lax.fori_loop(0, n_ch, ch_body, 0)
        if not chained:
            for d in range(2):
                sout_ref[s, d] = st_scr[d].T
        return carry

    lax.fori_loop(0, n_seq, seq_body, 0)
    o = of_scr[...] + ob_scr[...]
    o = o * lax.rsqrt(jnp.mean(o * o, axis=-1, keepdims=True) + EPS) * ng_ref[...]
    o_ref[...] = (o * _silu(ga_ref[...])).astype(o_ref.dtype)


def _hgrn(proj, lb_l, ng_l, blk0, n_blk, seq_len, state=None, layer=0):
    tb = TOK_BLOCK
    chained = state is not None
    seg, msk, up = _hgrn_consts(HGRN_C)
    cb = lambda col: col // A_DK

    def col_spec(col):
        return pl.BlockSpec((tb, A_DK), lambda b, h: (b + blk0, cb(col) + h))

    def full(a):
        nd = a.ndim
        return pl.BlockSpec(a.shape, lambda b, h: (0,) * nd)

    in_specs = [col_spec(COL_Q), col_spec(COL_FF), col_spec(COL_FB), col_spec(COL_V), col_spec(COL_GA),
                pl.BlockSpec((1, A_DK), lambda b, h: (0, h)),
                pl.BlockSpec((1, A_DV), lambda b, h: (0, 0)),
                full(seg), full(msk), full(up)]
    args = [proj, proj, proj, proj, proj, lb_l.reshape(1, -1), ng_l.reshape(1, -1),
            jnp.asarray(seg), jnp.asarray(msk), jnp.asarray(up)]
    out_specs = [pl.BlockSpec((tb, A_DV), lambda b, h: (b, h))]
    out_shape = [jax.ShapeDtypeStruct((n_blk * tb, A_WIDTH), BF16)]
    if chained:
        in_specs.append(pl.BlockSpec((None, None, 2, None, A_DK, A_DV), lambda b, h: (b, layer, 0, h, 0, 0)))
        args.append(state)
    else:
        n_seq = tb // seq_len
        out_specs.append(pl.BlockSpec((n_seq, 2, None, A_DK, A_DV), lambda b, h: (b, 0, h, 0, 0)))
        out_shape.append(jax.ShapeDtypeStruct((n_blk * n_seq, 2, A_HEADS, A_DK, A_DV), F32))
    res = pl.pallas_call(
        functools.partial(_hgrn_kernel, seq_len=seq_len, chained=chained),
        grid=(n_blk, A_HEADS),
        in_specs=in_specs, out_specs=out_specs, out_shape=out_shape,
        scratch_shapes=[pltpu.VMEM((tb, A_DV), F32), pltpu.VMEM((tb, A_DV), F32),
                        pltpu.VMEM((2, A_DV, A_DK), F32)],
        compiler_params=_cp("arbitrary", "arbitrary"),
        name="hgrn2",
    )(*args)
    return res[0], (None if chained else res[1])


def _lru_kernel(*refs, row_len, chained):
    it = iter(refs)
    x_ref, gb_ref, cw_ref, cb_ref, wa_ref, ba_ref, wx_ref, bx_ref, lam_ref = (next(it) for _ in range(9))
    h0_ref = next(it) if chained else None
    o_ref = next(it)
    hout_ref = None if chained else next(it)
    a_scr, u_scr, h_scr, p_scr = (next(it) for _ in range(4))

    tb = x_ref.shape[0]
    seg = LRU_SEG
    n_seg = tb // seg
    xc = _dwconv(x_ref[...], cw_ref[...], cb_ref[...], row_len)
    for d in range(2):
        r = jax.nn.sigmoid(_dot_hi(xc, wa_ref[d]) + ba_ref[d])
        g = jax.nn.sigmoid(_dot_hi(xc, wx_ref[d]) + bx_ref[d])
        log_a = LRU_C * r * jax.nn.log_sigmoid(lam_ref[d])
        a = jnp.exp(log_a)
        a_scr[d] = a
        u_scr[d] = jnp.sqrt(1.0 - a * a) * (g * xc)

    def step(i, carry):
        hf, pf, hb, pb = carry
        tf = i
        tr = seg - 1 - i
        af = a_scr[0, pl.ds(tf, n_seg, stride=seg), :]
        hf = af * hf + u_scr[0, pl.ds(tf, n_seg, stride=seg), :]
        h_scr[0, pl.ds(tf, n_seg, stride=seg), :] = hf
        ab = a_scr[1, pl.ds(tr, n_seg, stride=seg), :]
        hb = ab * hb + u_scr[1, pl.ds(tr, n_seg, stride=seg), :]
        h_scr[1, pl.ds(tr, n_seg, stride=seg), :] = hb
        if chained:
            pf = af * pf
            pb = ab * pb
            p_scr[0, pl.ds(tf, n_seg, stride=seg), :] = pf
            p_scr[1, pl.ds(tr, n_seg, stride=seg), :] = pb
        return hf, pf, hb, pb

    zeros = jnp.zeros((n_seg, B_BLOCK), F32)
    ones = jnp.ones((n_seg, B_BLOCK), F32)
    hf, pf, hb, pb = lax.fori_loop(0, seg, step, (zeros, ones, zeros, ones))

    if chained:
        hin = h0_ref[0:1, :]
        for s in range(n_seg):
            rows = pl.ds(s * seg, seg)
            h_scr[0, rows, :] = h_scr[0, rows, :] + p_scr[0, rows, :] * hin
            hin = hf[s:s + 1] + pf[s:s + 1] * hin
        hin = h0_ref[1:2, :]
        for s in range(n_seg - 1, -1, -1):
            rows = pl.ds(s * seg, seg)
            h_scr[1, rows, :] = h_scr[1, rows, :] + p_scr[1, rows, :] * hin
            hin = hb[s:s + 1] + pb[s:s + 1] * hin
    else:
        hout_ref[0] = hf
        hout_ref[1] = hb
    o_ref[...] = ((h_scr[0] + h_scr[1]) * jax.nn.gelu(gb_ref[...])).astype(o_ref.dtype)


def _lru(proj, p, layer, blk0, n_blk, row_len, state=None):
    tb = TOK_BLOCK
    chained = state is not None
    nb = B_WIDTH // B_BLOCK
    cbx, cbg = COL_XB // B_BLOCK, COL_GB // B_BLOCK
    in_specs = [pl.BlockSpec((tb, B_BLOCK), lambda b, n: (b + blk0, cbx + n)),
                pl.BlockSpec((tb, B_BLOCK), lambda b, n: (b + blk0, cbg + n)),
                pl.BlockSpec((None, CONV_W, B_BLOCK), lambda b, n: (layer, 0, n)),
                pl.BlockSpec((None, 1, B_BLOCK), lambda b, n: (layer, 0, n)),
                pl.BlockSpec((None, 2, None, B_BLOCK, B_BLOCK), lambda b, n: (layer, 0, n, 0, 0)),
                pl.BlockSpec((None, 2, 1, B_BLOCK), lambda b, n: (layer, 0, 0, n)),
                pl.BlockSpec((None, 2, None, B_BLOCK, B_BLOCK), lambda b, n: (layer, 0, n, 0, 0)),
                pl.BlockSpec((None, 2, 1, B_BLOCK), lambda b, n: (layer, 0, 0, n)),
                pl.BlockSpec((None, 2, 1, B_BLOCK), lambda b, n: (layer, 0, 0, n))]
    d4 = lambda a: a.reshape(DEPTH, 2, 1, B_WIDTH)
    args = [proj, proj, p["conv_b_w"], p["conv_b_b"].reshape(DEPTH, 1, B_WIDTH),
            p["lru_wa"], d4(p["lru_ba"]), p["lru_wx"], d4(p["lru_bx"]), d4(p["lru_lambda"])]
    out_specs = [pl.BlockSpec((tb, B_BLOCK), lambda b, n: (b, n))]
    out_shape = [jax.ShapeDtypeStruct((n_blk * tb, B_WIDTH), BF16)]
    if chained:
        in_specs.append(pl.BlockSpec((None, None, 2, B_BLOCK), lambda b, n: (b, layer, 0, n)))
        args.append(state)
    else:
        n_seq = tb // LRU_SEG
        out_specs.append(pl.BlockSpec((2, n_seq, B_BLOCK), lambda b, n: (0, b, n)))
        out_shape.append(jax.ShapeDtypeStruct((2, n_blk * n_seq, B_WIDTH), F32))
    res = pl.pallas_call(
        functools.partial(_lru_kernel, row_len=row_len, chained=chained),
        grid=(n_blk, nb),
        in_specs=in_specs, out_specs=out_specs, out_shape=out_shape,
        scratch_shapes=[pltpu.VMEM((2, tb, B_BLOCK), F32) for _ in range(4)],
        compiler_params=_cp("arbitrary", "arbitrary"),
        name="rglru",
    )(*args)
    return res[0], (None if chained else res[1])


def _ssd_kernel(*refs, seq_len, row_len, chained):
    it = iter(refs)
    (xs_ref, bm_ref, cm_ref, z_ref, dt_ref, cwx_ref, cwb_ref, cwc_ref, cbx_ref, cbb_ref, cbc_ref,
     dtb_ref, a_ref, dsk_ref, ng_ref, tri_ref) = (next(it) for _ in range(16))
    s0_ref = next(it) if chained else None
    o_ref = next(it)
    sout_ref = None if chained else next(it)
    xs_scr, bm_scr, cm_scr, dt_scr, yf_scr, yb_scr, st_scr = (next(it) for _ in range(7))

    lc = SSD_L
    hpg = C_HEADS // C_GROUPS
    tb = xs_ref.shape[0]
    n_seq = tb // seq_len
    n_ch = seq_len // lc
    xs_scr[...] = _silu(_dwconv(xs_ref[...], cwx_ref[...], cbx_ref[...], row_len))
    bm_scr[...] = _silu(_dwconv(bm_ref[...], cwb_ref[...], cbb_ref[...], row_len))
    cm_scr[...] = _silu(_dwconv(cm_ref[...], cwc_ref[...], cbc_ref[...], row_len))
    dt_scr[...] = jax.nn.softplus(dt_ref[...] + dtb_ref[...])
    a_row = -jnp.exp(a_ref[...])
    y_scrs = (yf_scr, yb_scr)
    rr = lax.broadcasted_iota(jnp.int32, (lc, lc), 0)
    cc = lax.broadcasted_iota(jnp.int32, (lc, lc), 1)
    causal = (rr >= cc, cc >= rr)

    def chunk(d, r0):
        rows = pl.ds(r0, lc)
        dtc = dt_scr[rows, :]
        cum = _dot_hi(tri_ref[d], dtc * a_row)
        cum_t = cum.T
        xs = xs_scr[rows, :]
        bm = bm_scr[rows, :]
        cm = cm_scr[rows, :].astype(BF16)
        scores = _dot_nt(cm, bm.astype(BF16))
        ys = []
        for hh in range(hpg):
            ln = d * hpg + hh
            col = cum[:, ln:ln + 1]
            seg = jnp.exp(jnp.where(causal[d], col - cum_t[ln:ln + 1, :], NEG_BIG))
            xdt = (xs[:, hh * C_HEADDIM:(hh + 1) * C_HEADDIM] * dtc[:, ln:ln + 1]).astype(BF16)
            st = st_scr[d, hh]
            y = _dot((scores * seg).astype(BF16), xdt)
            y = y + _dot_nt(cm, st.astype(BF16)) * jnp.exp(col)
            end = col[lc - 1:lc] if d == 0 else col[0:1]
            st_scr[d, hh] = jnp.exp(end) * st + _dot_tn(xdt, (bm * jnp.exp(end - col)).astype(BF16))
            ys.append(y)
        y_scrs[d][rows, :] = jnp.concatenate(ys, axis=-1)

    def seq_body(s, carry):
        for d in range(2):
            for hh in range(hpg):
                st_scr[d, hh] = s0_ref[d, hh] if chained else jnp.zeros((C_HEADDIM, C_STATE), F32)

        def ch_body(i, carry2):
            base = s * seq_len
            chunk(0, pl.multiple_of(base + i * lc, lc))
            chunk(1, pl.multiple_of(base + (n_ch - 1 - i) * lc, lc))
            return carry2

        lax.fori_loop(0, n_ch, ch_body, 0)
        if not chained:
            for d in range(2):
                for hh in range(hpg):
                    sout_ref[s, d, hh] = st_scr[d, hh]
        return carry

    lax.fori_loop(0, n_seq, seq_body, 0)
    y = yf_scr[...] + yb_scr[...] + dsk_ref[...] * xs_scr[...]
    y = y * _silu(z_ref[...])
    y = y * lax.rsqrt(jnp.mean(y * y, axis=-1, keepdims=True) + EPS) * ng_ref[...]
    o_ref[...] = y.astype(o_ref.dtype)


def _ssd(proj, dtp, p, layer, blk0, n_blk, seq_len, row_len, state=None):
    tb = TOK_BLOCK
    chained = state is not None
    hpg = C_HEADS // C_GROUPS
    gw = hpg * C_HEADDIM
    col_x, col_b, col_c = COL_XBC, COL_XBC + C_INNER, COL_XBC + C_INNER + C_GROUPS * C_STATE
    tri = np.stack([np.tril(np.ones((SSD_L, SSD_L), np.float32)), np.triu(np.ones((SSD_L, SSD_L), np.float32))])
    in_specs = [pl.BlockSpec((tb, gw), lambda b, g: (b + blk0, col_x // gw + g)),
                pl.BlockSpec((tb, C_STATE), lambda b, g: (b + blk0, col_b // C_STATE + g)),
                pl.BlockSpec((tb, C_STATE), lambda b, g: (b + blk0, col_c // C_STATE + g)),
                pl.BlockSpec((tb, gw), lambda b, g: (b + blk0, COL_Z // gw + g)),
                pl.BlockSpec((tb, 128), lambda b, g: (b + blk0, g)),
                pl.BlockSpec((None, CONV_W, gw), lambda b, g: (layer, 0, g)),
                pl.BlockSpec((None, CONV_W, C_STATE), lambda b, g: (layer, 0, C_INNER // C_STATE + g)),
                pl.BlockSpec((None, CONV_W, C_STATE), lambda b, g: (layer, 0, C_INNER // C_STATE + C_GROUPS + g)),
                pl.BlockSpec((None, 1, gw), lambda b, g: (layer, 0, g)),
                pl.BlockSpec((None, 1, C_STATE), lambda b, g: (layer, 0, C_INNER // C_STATE + g)),
                pl.BlockSpec((None, 1, C_STATE), lambda b, g: (layer, 0, C_INNER // C_STATE + C_GROUPS + g)),
                pl.BlockSpec((None, None, 1, 128), lambda b, g: (layer, g, 0, 0)),
                pl.BlockSpec((None, None, 1, 128), lambda b, g: (layer, g, 0, 0)),
                pl.BlockSpec((None, 1, gw), lambda b, g: (layer, 0, g)),
                pl.BlockSpec((None, 1, gw), lambda b, g: (layer, 0, g)),
                pl.BlockSpec(tri.shape, lambda b, g: (0, 0, 0))]
    cw = p["conv_c_w"]
    cbias = p["conv_c_b"].reshape(DEPTH, 1, -1)
    args = [proj, proj, proj, proj, dtp, cw, cw, cw, cbias, cbias, cbias,
            p["dt_bias_g"], p["a_log_g"], p["ssd_d_rep"], p["ssd_norm_g"].reshape(DEPTH, 1, C_INNER),
            jnp.asarray(tri)]
    out_specs = [pl.BlockSpec((tb, gw), lambda b, g: (b, g))]
    out_shape = [jax.ShapeDtypeStruct((n_blk * tb, C_INNER), BF16)]
    if chained:
        in_specs.append(pl.BlockSpec((None, None, 2, hpg, C_HEADDIM, C_STATE),
                                     lambda b, g: (b, layer, 0, g, 0, 0)))
        args.append(state)
    else:
        n_seq = tb // seq_len
        out_specs.append(pl.BlockSpec((n_seq, 2, hpg, C_HEADDIM, C_STATE), lambda b, g: (b, 0, g, 0, 0)))
        out_shape.append(jax.ShapeDtypeStruct((n_blk * n_seq, 2, C_HEADS, C_HEADDIM, C_STATE), F32))
    res = pl.pallas_call(
        functools.partial(_ssd_kernel, seq_len=seq_len, row_len=row_len, chained=chained),
        grid=(n_blk, C_GROUPS),
        in_specs=in_specs, out_specs=out_specs, out_shape=out_shape,
        scratch_shapes=[pltpu.VMEM((tb, gw), F32), pltpu.VMEM((tb, C_STATE), F32), pltpu.VMEM((tb, C_STATE), F32),
                        pltpu.VMEM((tb, 128), F32), pltpu.VMEM((tb, gw), F32), pltpu.VMEM((tb, gw), F32),
                        pltpu.VMEM((2, hpg, C_HEADDIM, C_STATE), F32)],
        compiler_params=_cp("arbitrary", "arbitrary"),
        name="ssd",
    )(*args)
    return res[0], (None if chained else res[1])


def _route_kernel(lg_ref, bias_ref, e_ref, w_ref):
    lg = lg_ref[...]
    ne = lg.shape[0]
    epg = ne // N_EXPERT_GROUPS
    mx = jnp.max(lg, axis=0, keepdims=True)
    ex = jnp.exp(lg - mx)
    probs = ex / jnp.sum(ex, axis=0, keepdims=True)
    sel = probs + bias_ref[...]
    rows = [sel[e:e + 1] for e in range(ne)]
    top2 = []
    for e in range(ne):
        g0 = (e // epg) * epg
        rank = jnp.zeros_like(rows[e])
        for o in range(g0, g0 + epg):
            if o == e:
                continue
            ahead = (rows[o] > rows[e]) | ((rows[o] == rows[e]) & (o < e))
            rank = rank + jnp.where(ahead, 1.0, 0.0)
        top2.append(rank < 1.5)
    score = []
    for g in range(N_EXPERT_GROUPS):
        sc = jnp.zeros_like(rows[0])
        for e in range(g * epg, (g + 1) * epg):
            sc = sc + jnp.where(top2[e], rows[e], 0.0)
        score.append(sc)
    best = []
    for g in range(N_EXPERT_GROUPS):
        ok = jnp.ones(rows[0].shape, jnp.bool_)
        for o in range(N_EXPERT_GROUPS):
            if o < g:
                ok = ok & (score[g] > score[o])
            elif o > g:
                ok = ok & (score[g] >= score[o])
        best.append(ok)
    first = jnp.full(rows[0].shape, float(ne), F32)
    second = jnp.full(rows[0].shape, -1.0, F32)
    p_first = jnp.zeros_like(rows[0])
    p_second = jnp.zeros_like(rows[0])
    for e in range(ne - 1, -1, -1):
        ch = top2[e] & best[e // epg]
        first = jnp.where(ch, float(e), first)
        p_first = jnp.where(ch, probs[e:e + 1], p_first)
    for e in range(ne):
        ch = top2[e] & best[e // epg]
        second = jnp.where(ch, float(e), second)
        p_second = jnp.where(ch, probs[e:e + 1], p_second)
    tot = p_first + p_second
    e_ref[0:1, :] = first.astype(jnp.int32)
    e_ref[1:2, :] = second.astype(jnp.int32)
    w_ref[0:1, :] = p_first / tot
    w_ref[1:2, :] = p_second / tot


def _route(logits_t, router_bias):
    ne, n = logits_t.shape
    return pl.pallas_call(
        _route_kernel,
        grid=(1,),
        in_specs=[pl.BlockSpec((ne, n), lambda i: (0, 0)), pl.BlockSpec((ne, 1), lambda i: (0, 0))],
        out_specs=[pl.BlockSpec((2, n), lambda i: (0, 0)), pl.BlockSpec((2, n), lambda i: (0, 0))],
        out_shape=[jax.ShapeDtypeStruct((2, n), jnp.int32), jax.ShapeDtypeStruct((2, n), F32)],
        compiler_params=_cp("arbitrary"),
        name="route",
    )(logits_t, router_bias.reshape(ne, 1))


def _dispatch_plan(e2, tm, n_tiles):
    n = e2.shape[1]
    e_flat = e2.reshape(-1)
    onehot = (e_flat[:, None] == jnp.arange(N_EXPERTS, dtype=jnp.int32)[None, :]).astype(jnp.int32)
    counts = jnp.sum(onehot, axis=0)
    rank = jnp.take_along_axis(jnp.cumsum(onehot, axis=0), e_flat[:, None], axis=1)[:, 0] - 1
    tiles_per = (counts + tm - 1) // tm
    tile_end = jnp.cumsum(tiles_per)
    starts = (tile_end - tiles_per) * tm
    dest = starts[e_flat] + rank
    src = jnp.zeros((n_tiles * tm,), jnp.int32).at[dest].set(jnp.arange(2 * n, dtype=jnp.int32) % n)
    n_used = tile_end[-1]
    t_idx = jnp.minimum(jnp.arange(n_tiles, dtype=jnp.int32), n_used - 1)
    tile_expert = jnp.minimum(jnp.searchsorted(tile_end, t_idx, side="right"), N_EXPERTS - 1).astype(jnp.int32)
    return dest.reshape(2, n), src, tile_expert, n_used.reshape(1).astype(jnp.int32)


def _moe_kernel(te_ref, nu_ref, xs_ref, wg_ref, wu_ref, wd_ref, o_ref):
    t, j = pl.program_id(0), pl.program_id(1)

    @pl.when(j == 0)
    def _():
        o_ref[...] = jnp.zeros_like(o_ref)

    @pl.when(t < nu_ref[0])
    def _():
        x = xs_ref[...]
        hg = _dot(x, wg_ref[...].astype(BF16))
        hu = _dot(x, wu_ref[...].astype(BF16))
        act = (_silu(hg) * hu).astype(BF16)
        o_ref[...] += _dot(act, wd_ref[...].astype(BF16))


def _moe(xs, tile_expert, n_used, w_gate, w_up, w_down, layer, tm=MOE_TM, tf=MOE_TF):
    r, d = xs.shape
    n_tiles = r // tm
    nj = D_EXPERT // tf

    def jj(t, j, nu):
        return jnp.where(t < nu[0], j, nj - 1)

    grid_spec = pltpu.PrefetchScalarGridSpec(
        num_scalar_prefetch=2,
        grid=(n_tiles, nj),
        in_specs=[pl.BlockSpec((tm, d), lambda t, j, te, nu: (jnp.minimum(t, nu[0] - 1), 0)),
                  pl.BlockSpec((None, None, d, tf), lambda t, j, te, nu: (layer, te[t], 0, jj(t, j, nu))),
                  pl.BlockSpec((None, None, d, tf), lambda t, j, te, nu: (layer, te[t], 0, jj(t, j, nu))),
                  pl.BlockSpec((None, None, tf, d), lambda t, j, te, nu: (layer, te[t], jj(t, j, nu), 0))],
        out_specs=pl.BlockSpec((tm, d), lambda t, j, te, nu: (t, 0)),
    )
    return pl.pallas_call(
        _moe_kernel,
        grid_spec=grid_spec,
        out_shape=jax.ShapeDtypeStruct((r, d), F32),
        compiler_params=_cp("arbitrary", "arbitrary"),
        name="moe_experts",
    )(tile_expert, n_used, xs, w_gate, w_up, w_down)


def _combine_kernel(x_ref, y0_ref, y1_ref, w_ref, ga_ref, o_ref):
    w = w_ref[...]
    y = w[:, 0:1] * y0_ref[...] + w[:, 1:2] * y1_ref[...]
    o_ref[...] = x_ref[...] + ga_ref[...] * y


def _combine(x, y0, y1, w2, mod_l, ga_chunk, tm=512):
    n, d = x.shape
    row = pl.BlockSpec((tm, d), lambda i: (i, 0))
    return pl.pallas_call(
        _combine_kernel,
        grid=(n // tm,),
        in_specs=[row, row, row, pl.BlockSpec((tm, 2), lambda i: (i, 0)),
                  pl.BlockSpec((None, 1, d), lambda i: (_cond_row(i, tm), 0, ga_chunk))],
        out_specs=row,
        out_shape=jax.ShapeDtypeStruct((n, d), F32),
        compiler_params=_cp("arbitrary"),
        name="combine",
    )(x, y0, y1, w2, mod_l)


def _layer(x, l, mod_l, p, lbs, w_router, router_bias, states):
    n = x.shape[0]
    state_hgrn, state_rglru, state_ssd = states
    h = _normmod(x, p["norm1_g"][l], mod_l, chunk=(0, 1))
    proj = _mm(h, p["w_in"], l, 0, N_MAIN)
    dtp = _mm(h, p["w_dt"], l, 0, 512, tn=512)
    gates = _mm(h, p["w_merge"], l, 0, 3 * D_MODEL, act="sigmoid", out_dtype=BF16)
    nb = n // TOK_BLOCK
    ns = nb - N_CTX_BLOCKS
    oa_c, hg_new = _hgrn(proj, lbs[l], p["hgrn_norm_g"][l], 0, N_CTX_BLOCKS, SEQ)
    oa_s, _ = _hgrn(proj, lbs[l], p["hgrn_norm_g"][l], N_CTX_BLOCKS, ns, TOK_BLOCK, state_hgrn, l)
    ob_c, lru_new = _lru(proj, p, l, 0, N_CTX_BLOCKS, SEQ)
    ob_s, _ = _lru(proj, p, l, N_CTX_BLOCKS, ns, GRID_W, state_rglru)
    oc_c, ssd_new = _ssd(proj, dtp, p, l, 0, N_CTX_BLOCKS, SEQ, SEQ)
    oc_s, _ = _ssd(proj, dtp, p, l, N_CTX_BLOCKS, ns, TOK_BLOCK, GRID_W, state_ssd)
    oa = jnp.concatenate([oa_c, oa_s], axis=0)
    ob = jnp.concatenate([ob_c, ob_s], axis=0)
    oc = jnp.concatenate([oc_c, oc_s], axis=0)
    merged = _merge(oa, ob, oc, p["w_branch_a"], p["w_branch_b"], p["w_branch_c"], gates, l)
    x = _mm_residual(merged, p["w_out"], l, x, mod_l, 2)
    h2, logits = _normmod(x, p["norm2_g"][l], mod_l, chunk=(3, 4), w_router=w_router)
    e2, w2 = _route(logits.T, router_bias)
    n_tiles = (2 * n) // MOE_TM + N_EXPERTS
    dest, src, tile_expert, n_used = _dispatch_plan(e2, MOE_TM, n_tiles)
    xs = jnp.take(h2, src, axis=0)
    y = _moe(xs, tile_expert, n_used, p["w_e_gate"], p["w_e_up"], p["w_e_down"], l)
    y0 = jnp.take(y, dest[0], axis=0)
    y1 = jnp.take(y, dest[1], axis=0)
    x = _combine(x, y0, y1, w2.T, mod_l, 5)
    return x, hg_new, lru_new.transpose(1, 0, 2), ssd_new


def _hgrn_lower_bounds(lb_raw):
    pr = jax.nn.softmax(lb_raw.astype(F32), axis=0)
    cum = jnp.cumsum(pr, axis=0)
    return cum - cum[0]


def kernel(x_prompt, x_sample, state_hgrn, state_rglru, state_ssd, c, c_ctx, w_mod, b_mod, norm1_g, norm2_g, w_in, hgrn_lb, hgrn_norm_g, conv_b_w, conv_b_b, lru_wa, lru_ba, lru_wx, lru_bx, lru_lambda, conv_c_w, conv_c_b, ssd_a_log, ssd_dt_bias, ssd_d, ssd_norm_g, w_branch_a, w_branch_b, w_branch_c, w_out, w_router, router_bias, w_e_gate, w_e_up, w_e_down, final_g):
    bsz, seq, d = x_prompt.shape
    dbsz, dseq, _ = x_sample.shape
    assert seq == SEQ and dseq == TOK_BLOCK and (bsz * seq) == N_CTX_BLOCKS * TOK_BLOCK and d == D_MODEL
    x = jnp.concatenate([x_prompt.reshape(-1, d), x_sample.reshape(-1, d)], axis=0)

    cond8 = jnp.zeros((8, d), F32).at[0].set(c_ctx).at[1:1 + dbsz].set(c)
    mod = _modulation(cond8, w_mod, b_mod).reshape(DEPTH, 8, 1, N_MOD * d)

    hpg = C_HEADS // C_GROUPS
    w_dt_raw = w_in[:, :, COL_DT:COL_DT + 2 * C_HEADS].reshape(DEPTH, d, 2, C_GROUPS, hpg)
    w_dt = jnp.zeros((DEPTH, d, C_GROUPS, 128), F32).at[:, :, :, :2 * hpg].set(
        w_dt_raw.transpose(0, 1, 3, 2, 4).reshape(DEPTH, d, C_GROUPS, 2 * hpg)).reshape(DEPTH, d, C_GROUPS * 128)

    def group_lanes(a):
        g = a.reshape(DEPTH, 2, C_GROUPS, hpg).transpose(0, 2, 1, 3).reshape(DEPTH, C_GROUPS, 1, 2 * hpg)
        return jnp.zeros((DEPTH, C_GROUPS, 1, 128), F32).at[..., :2 * hpg].set(g)

    p = dict(norm1_g=norm1_g, norm2_g=norm2_g, w_in=w_in, w_dt=w_dt,
             w_merge=w_in[:, :, COL_MERGE:], hgrn_norm_g=hgrn_norm_g,
             conv_b_w=conv_b_w, conv_b_b=conv_b_b, lru_wa=lru_wa, lru_ba=lru_ba, lru_wx=lru_wx,
             lru_bx=lru_bx, lru_lambda=lru_lambda, conv_c_w=conv_c_w, conv_c_b=conv_c_b,
             dt_bias_g=group_lanes(ssd_dt_bias), a_log_g=group_lanes(ssd_a_log),
             ssd_d_rep=jnp.repeat(ssd_d, C_HEADDIM, axis=-1).reshape(DEPTH, 1, C_INNER),
             ssd_norm_g=ssd_norm_g, w_branch_a=w_branch_a, w_branch_b=w_branch_b, w_branch_c=w_branch_c,
             w_out=w_out, w_e_gate=w_e_gate, w_e_up=w_e_up, w_e_down=w_e_down)
    lbs = _hgrn_lower_bounds(hgrn_lb)

    hg_list, lru_list, ssd_list = [], [], []
    for l in range(DEPTH):
        x, hg_new, lru_new, ssd_new = _layer(x, l, mod[l], p, lbs, w_router, router_bias,
                                             (state_hgrn, state_rglru, state_ssd))
        hg_list.append(hg_new)
        lru_list.append(lru_new)
        ssd_list.append(ssd_new)
    y = _normmod(x, final_g, out_dtype=F32)
    n_ctx = bsz * seq
    y_prompt = y[:n_ctx].reshape(bsz, seq, d)
    y_sample = y[n_ctx:].reshape(dbsz, dseq, d)
    return (y_prompt, y_sample, jnp.stack(hg_list, axis=1), jnp.stack(lru_list, axis=1),
            jnp.stack(ssd_list, axis=1))
```

```python
import functools
import math

import numpy as np
import jax
import jax.numpy as jnp
from jax import lax
from jax.experimental import pallas as pl
from jax.experimental.pallas import tpu as pltpu

F32 = jnp.float32
BF16 = jnp.bfloat16
HIGHEST = lax.Precision.HIGHEST

EPS = 1e-6
D_MODEL = 2048
DEPTH = 4
N_MOD = 6
GRID_W = 64
SEQ = 256
TOK_BLOCK = 2048
N_CTX_BLOCKS = 2

A_HEADS = 8
A_DK = 128
A_DV = 128
A_WIDTH = 1024
HGRN_C = 128

B_WIDTH = 1024
B_BLOCK = 128
LRU_C = 8.0
LRU_SEG = 256

C_INNER = 1024
C_HEADDIM = 64
C_HEADS = 16
C_GROUPS = 4
C_STATE = 128
SSD_L = 128
CONV_W = 4

N_EXPERTS = 16
N_EXPERT_GROUPS = 4
D_EXPERT = 1024
MOE_TM = 1024
MOE_TF = 256

COL_Q, COL_FF, COL_FB, COL_V, COL_GA = 0, 1024, 2048, 3072, 4096
COL_XB, COL_GB, COL_Z, COL_XBC, COL_DT, COL_MERGE = 5120, 6144, 7168, 8192, 10240, 10272
N_MAIN = 10240

VMEM_LIMIT_BYTES = 56 * 1024 * 1024
NEG_BIG = -1e30


def _cp(*sem):
    return pltpu.CompilerParams(dimension_semantics=sem, vmem_limit_bytes=VMEM_LIMIT_BYTES)


def _silu(x):
    return x * jax.nn.sigmoid(x)


def _dot(a, b):
    return jnp.dot(a, b, preferred_element_type=F32)


def _dot_nt(a, b):
    return lax.dot_general(a, b, (((1,), (1,)), ((), ())), preferred_element_type=F32)


def _dot_tn(a, b):
    return lax.dot_general(a, b, (((0,), (0,)), ((), ())), preferred_element_type=F32)


def _dot_hi(a, b):
    return jnp.dot(a, b, preferred_element_type=F32, precision=HIGHEST)


def _cond_row(i, tm):
    return jnp.maximum((i * tm) // TOK_BLOCK - (N_CTX_BLOCKS - 1), 0)


def _mod_kernel(c_ref, w_ref, b_ref, o_ref):
    s = _silu(c_ref[...]).astype(BF16)
    o_ref[...] = _dot(s, w_ref[...].astype(BF16)) + b_ref[...]


def _modulation(cond8, w_mod, b_mod):
    depth, d, n = w_mod.shape
    tn = 1024
    return pl.pallas_call(
        _mod_kernel,
        grid=(depth, n // tn),
        in_specs=[pl.BlockSpec((8, d), lambda l, j: (0, 0)),
                  pl.BlockSpec((None, d, tn), lambda l, j: (l, 0, j)),
                  pl.BlockSpec((None, 1, tn), lambda l, j: (l, 0, j))],
        out_specs=pl.BlockSpec((None, 8, tn), lambda l, j: (l, 0, j)),
        out_shape=jax.ShapeDtypeStruct((depth, 8, n), F32),
        compiler_params=_cp("arbitrary", "arbitrary"),
        name="modulation",
    )(cond8, w_mod, b_mod.reshape(depth, 1, n))


def _normmod_kernel(*refs, modulate, router):
    it = iter(refs)
    x_ref, g_ref = next(it), next(it)
    sc_ref = sh_ref = wr_ref = lg_ref = None
    if modulate:
        sc_ref, sh_ref = next(it), next(it)
    if router:
        wr_ref = next(it)
    h_ref = next(it)
    if router:
        lg_ref = next(it)
    x = x_ref[...]
    h = x * lax.rsqrt(jnp.mean(x * x, axis=-1, keepdims=True) + EPS) * g_ref[...]
    if modulate:
        h = h * (1.0 + sc_ref[...]) + sh_ref[...]
    h_ref[...] = h.astype(h_ref.dtype)
    if router:
        lg_ref[...] = _dot_hi(h, wr_ref[...])


def _normmod(x, g, mod_l=None, chunk=None, w_router=None, out_dtype=BF16, tm=512):
    n, d = x.shape
    modulate = mod_l is not None
    router = w_router is not None
    in_specs = [pl.BlockSpec((tm, d), lambda i: (i, 0)),
                pl.BlockSpec((1, d), lambda i: (0, 0))]
    args = [x, g.reshape(1, d)]
    if modulate:
        sh_c, sc_c = chunk
        in_specs += [pl.BlockSpec((None, 1, d), lambda i: (_cond_row(i, tm), 0, sc_c)),
                     pl.BlockSpec((None, 1, d), lambda i: (_cond_row(i, tm), 0, sh_c))]
        args += [mod_l, mod_l]
    out_specs = [pl.BlockSpec((tm, d), lambda i: (i, 0))]
    out_shape = [jax.ShapeDtypeStruct((n, d), out_dtype)]
    if router:
        ne = w_router.shape[1]
        in_specs.append(pl.BlockSpec((d, ne), lambda i: (0, 0)))
        args.append(w_router)
        out_specs.append(pl.BlockSpec((tm, ne), lambda i: (i, 0)))
        out_shape.append(jax.ShapeDtypeStruct((n, ne), F32))
    res = pl.pallas_call(
        functools.partial(_normmod_kernel, modulate=modulate, router=router),
        grid=(n // tm,),
        in_specs=in_specs, out_specs=out_specs, out_shape=out_shape,
        compiler_params=_cp("arbitrary"),
        name="normmod",
    )(*args)
    return res if router else res[0]


def _mm_kernel(a_ref, w_ref, o_ref, *, act):
    acc = _dot(a_ref[...], w_ref[...].astype(BF16))
    if act == "sigmoid":
        acc = jax.nn.sigmoid(acc)
    o_ref[...] = acc.astype(o_ref.dtype)


def _mm(a, w, layer, col0, n_out, act=None, out_dtype=F32, tm=2048, tn=512):
    m, k = a.shape
    tn = min(tn, n_out)
    off = col0 // tn
    assert col0 % tn == 0 and n_out % tn == 0 and m % tm == 0
    if layer is None:
        w_spec = pl.BlockSpec((k, tn), lambda i, j: (0, j + off))
    else:
        w_spec = pl.BlockSpec((None, k, tn), lambda i, j: (layer, 0, j + off))
    return pl.pallas_call(
        functools.partial(_mm_kernel, act=act),
        grid=(m // tm, n_out // tn),
        in_specs=[pl.BlockSpec((tm, k), lambda i, j: (i, 0)), w_spec],
        out_specs=pl.BlockSpec((tm, tn), lambda i, j: (i, j)),
        out_shape=jax.ShapeDtypeStruct((m, n_out), out_dtype),
        compiler_params=_cp("arbitrary", "arbitrary"),
        name="matmul",
    )(a, w)


def _mm_res_kernel(a_ref, w_ref, x_ref, ga_ref, o_ref):
    acc = _dot(a_ref[...], w_ref[...].astype(BF16))
    o_ref[...] = x_ref[...] + ga_ref[...] * acc


def _mm_residual(a, w, layer, x, mod_l, ga_chunk, tm=1024, tn=512):
    m, k = a.shape
    n = w.shape[-1]
    gs = n // tn
    return pl.pallas_call(
        _mm_res_kernel,
        grid=(m // tm, n // tn),
        in_specs=[pl.BlockSpec((tm, k), lambda i, j: (i, 0)),
                  pl.BlockSpec((None, k, tn), lambda i, j: (layer, 0, j)),
                  pl.BlockSpec((tm, tn), lambda i, j: (i, j)),
                  pl.BlockSpec((None, 1, tn), lambda i, j: (_cond_row(i, tm), 0, ga_chunk * gs + j))],
        out_specs=pl.BlockSpec((tm, tn), lambda i, j: (i, j)),
        out_shape=jax.ShapeDtypeStruct((m, n), F32),
        compiler_params=_cp("arbitrary", "arbitrary"),
        name="out_proj",
    )(a, w, x, mod_l)


def _merge_kernel(oa_ref, ob_ref, oc_ref, wa_ref, wb_ref, wc_ref, g1_ref, g2_ref, g3_ref, o_ref):
    ya = _dot(oa_ref[...], wa_ref[...].astype(BF16))
    yb = _dot(ob_ref[...], wb_ref[...].astype(BF16))
    yc = _dot(oc_ref[...], wc_ref[...].astype(BF16))
    m = (g1_ref[...].astype(F32) * ya + g2_ref[...].astype(F32) * yb + g3_ref[...].astype(F32) * yc)
    o_ref[...] = m.astype(o_ref.dtype)


def _merge(oa, ob, oc, wa, wb, wc, gates, layer, tm=1024, tn=512):
    m, k = oa.shape
    n = wa.shape[-1]
    gs = n // tn
    a_spec = pl.BlockSpec((tm, k), lambda i, j: (i, 0))
    w_spec = pl.BlockSpec((None, k, tn), lambda i, j: (layer, 0, j))
    return pl.pallas_call(
        _merge_kernel,
        grid=(m // tm, n // tn),
        in_specs=[a_spec, a_spec, a_spec, w_spec, w_spec, w_spec,
                  pl.BlockSpec((tm, tn), lambda i, j: (i, j)),
                  pl.BlockSpec((tm, tn), lambda i, j: (i, j + gs)),
                  pl.BlockSpec((tm, tn), lambda i, j: (i, j + 2 * gs))],
        out_specs=pl.BlockSpec((tm, tn), lambda i, j: (i, j)),
        out_shape=jax.ShapeDtypeStruct((m, n), BF16),
        compiler_params=_cp("arbitrary", "arbitrary"),
        name="merge",
    )(oa, ob, oc, wa, wb, wc, gates, gates, gates)


def _dwconv(x, w, b, row_len):
    t = x.shape[0]
    pos = lax.broadcasted_iota(jnp.int32, x.shape, 0) % row_len
    xm2 = jnp.where(pos >= 2, pltpu.roll(x, 2, axis=0), 0.0)
    xm1 = jnp.where(pos >= 1, pltpu.roll(x, 1, axis=0), 0.0)
    xp1 = jnp.where(pos <= row_len - 2, pltpu.roll(x, t - 1, axis=0), 0.0)
    y = b + xm2 * w[0:1]
    y = y + xm1 * w[1:2]
    y = y + x * w[2:3]
    y = y + xp1 * w[3:4]
    return y


def _hgrn_consts(c):
    nl = int(math.log2(c))
    seg = np.zeros((2, nl + 1, c, c), np.float32)
    msk = np.zeros((2, nl + 1, c, c), np.float32)
    up = np.zeros((2, nl, c, 128), np.float32)
    for d in range(2):
        tt = np.arange(c) if d == 0 else c - 1 - np.arange(c)
        tr, tc = tt[:, None], tt[None, :]
        seg[d, 0] = tc <= tr
        for l in range(nl):
            s = 1 << l
            blk, upper = tt // (2 * s), (tt % (2 * s)) >= s
            mid = (blk * 2 * s + s)[:, None]
            seg_u = (tc >= mid) & (tc <= tr)
            seg_l = (tc >= tr + 1) & (tc <= mid - 1)
            seg[d, l + 1] = np.where(upper[:, None], seg_u, seg_l)
            msk[d, l] = upper[:, None] & (~upper)[None, :] & (blk[:, None] == blk[None, :])
            up[d, l] = upper[:, None]
        msk[d, nl] = np.eye(c)
    return seg.reshape(2, (nl + 1) * c, c), msk, up


def _hgrn_kernel(*refs, seq_len, chained, aliased):
    it = iter(refs)
    q_ref, ff_ref, fb_ref, v_ref, ga_ref, lb_ref, ng_ref = (next(it) for _ in range(7))
    seg_ref, msk_ref, up_ref = next(it), next(it), next(it)
    s0_ref = next(it) if chained else None
    if aliased:
        next(it)
    o_ref = next(it)
    sout_ref = None if chained else next(it)
    o_scr, qp_scr, u_scr, dec_scr, snap_scr = (next(it) for _ in range(5))

    c = HGRN_C
    nl = int(math.log2(c))
    tb = q_ref.shape[0]
    n_chunks = tb // c
    n_ch = seq_len // c
    lb = lb_ref[...]
    f_refs = (ff_ref, fb_ref)

    def phase_a(n, carry):
        rows = pl.ds(pl.multiple_of(n * c, c), c)
        q = _silu(q_ref[rows, :]) * (A_DK ** -0.5)
        qb = q.astype(BF16)
        v = v_ref[rows, :].astype(BF16)
        o = jnp.zeros((c, A_DV), F32)
        qps, kps, decs = [], [], []
        for d in range(2):
            f = lb + (1.0 - lb) * jax.nn.sigmoid(f_refs[d][rows, :])
            k = 1.0 - f
            lf = jnp.log(f)
            hi = lf.astype(BF16)
            mid = (lf - hi.astype(F32)).astype(BF16)
            e2 = _dot(seg_ref[d], jnp.concatenate([hi, mid], axis=1))
            e = e2[:, :A_DK] + e2[:, A_DK:]
            b = e[0:c]
            att = msk_ref[d, nl] * _dot_nt(qb, k.astype(BF16))
            for l in range(nl):
                x = (jnp.where(up_ref[d, l] > 0.5, q, k) * jnp.exp(e[(l + 1) * c:(l + 2) * c])).astype(BF16)
                att = att + msk_ref[d, l] * _dot_nt(x, x)
            o = o + _dot(att.astype(BF16), v)
            b_end = b[c - 1:c] if d == 0 else b[0:1]
            qps.append((q * jnp.exp(b)).astype(BF16))
            kps.append((k * jnp.exp(b_end - b)).astype(BF16))
            decs.append(jnp.exp(b_end))
        o_scr[rows, :] = o
        qp_scr[rows, :] = jnp.concatenate(qps, axis=1)
        u_scr[n] = _dot_tn(v, jnp.concatenate(kps, axis=1))
        dec_scr[n] = jnp.concatenate(decs, axis=1)
        return carry

    lax.fori_loop(0, n_chunks, phase_a, 0)

    def init_state(d):
        return s0_ref[d].T if chained else jnp.zeros((A_DV, A_DK), F32)

    def phase_b1(j, carry):
        sf, sb = carry
        nf = j
        nb = n_chunks - 1 - j
        if not chained:
            sf = jnp.where(nf % n_ch == 0, 0.0, sf)
            sb = jnp.where(nb % n_ch == n_ch - 1, 0.0, sb)
        snap_scr[nf, :, 0:A_DK] = sf.astype(BF16)
        snap_scr[nb, :, A_DK:2 * A_DK] = sb.astype(BF16)
        sf = sf * dec_scr[nf][:, 0:A_DK] + u_scr[nf][:, 0:A_DK]
        sb = sb * dec_scr[nb][:, A_DK:2 * A_DK] + u_scr[nb][:, A_DK:2 * A_DK]
        if not chained:
            @pl.when(nf % n_ch == n_ch - 1)
            def _():
                sout_ref[nf // n_ch, 0] = sf.T

            @pl.when(nb % n_ch == 0)
            def _():
                sout_ref[nb // n_ch, 1] = sb.T
        return sf, sb

    lax.fori_loop(0, n_chunks, phase_b1, (init_state(0), init_state(1)))

    ng = ng_ref[...]

    def phase_b2(n, carry):
        rows = pl.ds(pl.multiple_of(n * c, c), c)
        o = o_scr[rows, :] + _dot_nt(qp_scr[rows, :], snap_scr[n])
        o = o * lax.rsqrt(jnp.mean(o * o, axis=-1, keepdims=True) + EPS) * ng
        o_ref[rows, :] = (o * _silu(ga_ref[rows, :])).astype(o_ref.dtype)
        return carry

    lax.fori_loop(0, n_chunks, phase_b2, 0, unroll=4)


def _alias_out(prev, in_specs, args):
    if prev is None:
        return {}
    in_specs.append(pl.BlockSpec(memory_space=pl.ANY))
    args.append(prev)
    return {len(args) - 1: 0}


def _hgrn(proj, lb_l, ng_l, blk0, n_blk, seq_len, state=None, layer=0, prev=None):
    tb = TOK_BLOCK
    chained = state is not None
    seg, msk, up = _hgrn_consts(HGRN_C)
    cb = lambda col: col // A_DK

    def col_spec(col):
        return pl.BlockSpec((tb, A_DK), lambda b, h: (b + blk0, cb(col) + h))

    def full(a):
        nd = a.ndim
        return pl.BlockSpec(a.shape, lambda b, h: (0,) * nd)

    in_specs = [col_spec(COL_Q), col_spec(COL_FF), col_spec(COL_FB), col_spec(COL_V), col_spec(COL_GA),
                pl.BlockSpec((1, A_DK), lambda b, h: (0, h)),
                pl.BlockSpec((1, A_DV), lambda b, h: (0, 0)),
                full(seg), full(msk), full(up)]
    args = [proj, proj, proj, proj, proj, lb_l.reshape(1, -1), ng_l.reshape(1, -1),
            jnp.asarray(seg, BF16), jnp.asarray(msk), jnp.asarray(up)]
    out_specs = [pl.BlockSpec((tb, A_DV), lambda b, h: (b + blk0, h))]
    out_shape = [jax.ShapeDtypeStruct((proj.shape[0], A_WIDTH), BF16)]
    if chained:
        in_specs.append(pl.BlockSpec((None, None, 2, None, A_DK, A_DV), lambda b, h: (b, layer, 0, h, 0, 0)))
        args.append(state)
    else:
        n_seq = tb // seq_len
        out_specs.append(pl.BlockSpec((n_seq, 2, None, A_DK, A_DV), lambda b, h: (b, 0, h, 0, 0)))
        out_shape.append(jax.ShapeDtypeStruct((n_blk * n_seq, 2, A_HEADS, A_DK, A_DV), F32))
    aliases = _alias_out(prev, in_specs, args)
    n_chunks = tb // HGRN_C
    res = pl.pallas_call(
        functools.partial(_hgrn_kernel, seq_len=seq_len, chained=chained, aliased=prev is not None),
        grid=(n_blk, A_HEADS),
        in_specs=in_specs, out_specs=out_specs, out_shape=out_shape,
        input_output_aliases=aliases,
        scratch_shapes=[pltpu.VMEM((tb, A_DV), F32), pltpu.VMEM((tb, 2 * A_DK), BF16),
                        pltpu.VMEM((n_chunks, A_DV, 2 * A_DK), F32), pltpu.VMEM((n_chunks, 1, 2 * A_DK), F32),
                        pltpu.VMEM((n_chunks, A_DV, 2 * A_DK), BF16)],
        compiler_params=_cp("arbitrary", "arbitrary"),
        name="hgrn2",
    )(*args)
    return res[0], (None if chained else res[1])


def _lru_kernel(*refs, row_len, chained, aliased):
    it = iter(refs)
    x_ref, gb_ref, cw_ref, cb_ref, wa_ref, ba_ref, wx_ref, bx_ref, lam_ref = (next(it) for _ in range(9))
    h0_ref = next(it) if chained else None
    if aliased:
        next(it)
    o_ref = next(it)
    hout_ref = None if chained else next(it)
    a_scr, u_scr, h_scr, p_scr = (next(it) for _ in range(4))

    tb = x_ref.shape[0]
    seg = LRU_SEG
    n_seg = tb // seg
    xc = _dwconv(x_ref[...], cw_ref[...], cb_ref[...], row_len)
    xcb = xc.astype(BF16)
    for d in range(2):
        r = jax.nn.sigmoid(_dot(xcb, wa_ref[d].astype(BF16)) + ba_ref[d])
        g = jax.nn.sigmoid(_dot(xcb, wx_ref[d].astype(BF16)) + bx_ref[d])
        log_a = LRU_C * r * jax.nn.log_sigmoid(lam_ref[d])
        a = jnp.exp(log_a)
        a_scr[d] = a
        u_scr[d] = jnp.sqrt(1.0 - a * a) * (g * xc)

    def step(i, carry):
        hf, pf, hb, pb = carry
        tf = i
        tr = seg - 1 - i
        af = a_scr[0, pl.ds(tf, n_seg, stride=seg), :]
        hf = af * hf + u_scr[0, pl.ds(tf, n_seg, stride=seg), :]
        h_scr[0, pl.ds(tf, n_seg, stride=seg), :] = hf
        ab = a_scr[1, pl.ds(tr, n_seg, stride=seg), :]
        hb = ab * hb + u_scr[1, pl.ds(tr, n_seg, stride=seg), :]
        h_scr[1, pl.ds(tr, n_seg, stride=seg), :] = hb
        if chained:
            pf = af * pf
            pb = ab * pb
            p_scr[0, pl.ds(tf, n_seg, stride=seg), :] = pf
            p_scr[1, pl.ds(tr, n_seg, stride=seg), :] = pb
        return hf, pf, hb, pb

    zeros = jnp.zeros((n_seg, B_BLOCK), F32)
    ones = jnp.ones((n_seg, B_BLOCK), F32)
    hf, pf, hb, pb = lax.fori_loop(0, seg, step, (zeros, ones, zeros, ones))

    if chained:
        hin = h0_ref[0:1, :]
        for s in range(n_seg):
            rows = pl.ds(s * seg, seg)
            h_scr[0, rows, :] = h_scr[0, rows, :] + p_scr[0, rows, :] * hin
            hin = hf[s:s + 1] + pf[s:s + 1] * hin
        hin = h0_ref[1:2, :]
        for s in range(n_seg - 1, -1, -1):
            rows = pl.ds(s * seg, seg)
            h_scr[1, rows, :] = h_scr[1, rows, :] + p_scr[1, rows, :] * hin
            hin = hb[s:s + 1] + pb[s:s + 1] * hin
    else:
        hout_ref[0] = hf
        hout_ref[1] = hb
    o_ref[...] = ((h_scr[0] + h_scr[1]) * jax.nn.gelu(gb_ref[...])).astype(o_ref.dtype)


def _lru(proj, p, layer, blk0, n_blk, row_len, state=None, prev=None):
    tb = TOK_BLOCK
    chained = state is not None
    nb = B_WIDTH // B_BLOCK
    cbx, cbg = COL_XB // B_BLOCK, COL_GB // B_BLOCK
    in_specs = [pl.BlockSpec((tb, B_BLOCK), lambda b, n: (b + blk0, cbx + n)),
                pl.BlockSpec((tb, B_BLOCK), lambda b, n: (b + blk0, cbg + n)),
                pl.BlockSpec((None, CONV_W, B_BLOCK), lambda b, n: (layer, 0, n)),
                pl.BlockSpec((None, 1, B_BLOCK), lambda b, n: (layer, 0, n)),
                pl.BlockSpec((None, 2, None, B_BLOCK, B_BLOCK), lambda b, n: (layer, 0, n, 0, 0)),
                pl.BlockSpec((None, 2, 1, B_BLOCK), lambda b, n: (layer, 0, 0, n)),
                pl.BlockSpec((None, 2, None, B_BLOCK, B_BLOCK), lambda b, n: (layer, 0, n, 0, 0)),
                pl.BlockSpec((None, 2, 1, B_BLOCK), lambda b, n: (layer, 0, 0, n)),
                pl.BlockSpec((None, 2, 1, B_BLOCK), lambda b, n: (layer, 0, 0, n))]
    d4 = lambda a: a.reshape(DEPTH, 2, 1, B_WIDTH)
    args = [proj, proj, p["conv_b_w"], p["conv_b_b"].reshape(DEPTH, 1, B_WIDTH),
            p["lru_wa"], d4(p["lru_ba"]), p["lru_wx"], d4(p["lru_bx"]), d4(p["lru_lambda"])]
    out_specs = [pl.BlockSpec((tb, B_BLOCK), lambda b, n: (b + blk0, n))]
    out_shape = [jax.ShapeDtypeStruct((proj.shape[0], B_WIDTH), BF16)]
    if chained:
        in_specs.append(pl.BlockSpec((None, None, 2, B_BLOCK), lambda b, n: (b, layer, 0, n)))
        args.append(state)
    else:
        n_seq = tb // LRU_SEG
        out_specs.append(pl.BlockSpec((2, n_seq, B_BLOCK), lambda b, n: (0, b, n)))
        out_shape.append(jax.ShapeDtypeStruct((2, n_blk * n_seq, B_WIDTH), F32))
    aliases = _alias_out(prev, in_specs, args)
    res = pl.pallas_call(
        functools.partial(_lru_kernel, row_len=row_len, chained=chained, aliased=prev is not None),
        grid=(n_blk, nb),
        in_specs=in_specs, out_specs=out_specs, out_shape=out_shape,
        input_output_aliases=aliases,
        scratch_shapes=[pltpu.VMEM((2, tb, B_BLOCK), F32) for _ in range(4)],
        compiler_params=_cp("arbitrary", "arbitrary"),
        name="rglru",
    )(*args)
    return res[0], (None if chained else res[1])


def _ssd_kernel(*refs, seq_len, row_len, chained, aliased):
    it = iter(refs)
    (xs_ref, bm_ref, cm_ref, z_ref, dt_ref, cwx_ref, cwb_ref, cwc_ref, cbx_ref, cbb_ref, cbc_ref,
     dtb_ref, a_ref, dsk_ref, ng_ref, tri_ref) = (next(it) for _ in range(16))
    s0_ref = next(it) if chained else None
    if aliased:
        next(it)
    o_ref = next(it)
    sout_ref = None if chained else next(it)
    xs_scr, bm_scr, cm_scr, dt_scr, yf_scr, yb_scr, st_scr = (next(it) for _ in range(7))

    lc = SSD_L
    hpg = C_HEADS // C_GROUPS
    tb = xs_ref.shape[0]
    n_seq = tb // seq_len
    n_ch = seq_len // lc
    xs_scr[...] = _silu(_dwconv(xs_ref[...], cwx_ref[...], cbx_ref[...], row_len))
    bm_scr[...] = _silu(_dwconv(bm_ref[...], cwb_ref[...], cbb_ref[...], row_len))
    cm_scr[...] = _silu(_dwconv(cm_ref[...], cwc_ref[...], cbc_ref[...], row_len))
    dt_scr[...] = jax.nn.softplus(dt_ref[...] + dtb_ref[...])
    a_row = -jnp.exp(a_ref[...])
    y_scrs = (yf_scr, yb_scr)
    rr = lax.broadcasted_iota(jnp.int32, (lc, lc), 0)
    cc = lax.broadcasted_iota(jnp.int32, (lc, lc), 1)
    causal = (rr >= cc, cc >= rr)

    def chunk(d, r0):
        rows = pl.ds(r0, lc)
        dtc = dt_scr[rows, :]
        cum = _dot_hi(tri_ref[d], dtc * a_row)
        cum_t = cum.T
        xs = xs_scr[rows, :]
        bm = bm_scr[rows, :]
        cm = cm_scr[rows, :].astype(BF16)
        scores = _dot_nt(cm, bm.astype(BF16))
        ys = []
        for hh in range(hpg):
            ln = d * hpg + hh
            col = cum[:, ln:ln + 1]
            seg = jnp.exp(jnp.where(causal[d], col - cum_t[ln:ln + 1, :], NEG_BIG))
            xdt = (xs[:, hh * C_HEADDIM:(hh + 1) * C_HEADDIM] * dtc[:, ln:ln + 1]).astype(BF16)
            st = st_scr[d, hh]
            y = _dot((scores * seg).astype(BF16), xdt)
            y = y + _dot_nt(cm, st.astype(BF16)) * jnp.exp(col)
            end = col[lc - 1:lc] if d == 0 else col[0:1]
            st_scr[d, hh] = jnp.exp(end) * st + _dot_tn(xdt, (bm * jnp.exp(end - col)).astype(BF16))
            ys.append(y)
        y_scrs[d][rows, :] = jnp.concatenate(ys, axis=-1)

    def seq_body(s, carry):
        for d in range(2):
            for hh in range(hpg):
                st_scr[d, hh] = s0_ref[d, hh] if chained else jnp.zeros((C_HEADDIM, C_STATE), F32)

        def ch_body(i, carry2):
            base = s * seq_len
            chunk(0, pl.multiple_of(base + i * lc, lc))
            chunk(1, pl.multiple_of(base + (n_ch - 1 - i) * lc, lc))
            return carry2

        lax.fori_loop(0, n_ch, ch_body, 0)
        if not chained:
            for d in range(2):
                for hh in range(hpg):
                    sout_ref[s, d, hh] = st_scr[d, hh]
        return carry

    lax.fori_loop(0, n_seq, seq_body, 0)
    y = yf_scr[...] + yb_scr[...] + dsk_ref[...] * xs_scr[...]
    y = y * _silu(z_ref[...])
    y = y * lax.rsqrt(jnp.mean(y * y, axis=-1, keepdims=True) + EPS) * ng_ref[...]
    o_ref[...] = y.astype(o_ref.dtype)


def _ssd(proj, dtp, p, layer, blk0, n_blk, seq_len, row_len, state=None, prev=None):
    tb = TOK_BLOCK
    chained = state is not None
    hpg = C_HEADS // C_GROUPS
    gw = hpg * C_HEADDIM
    col_x, col_b, col_c = COL_XBC, COL_XBC + C_INNER, COL_XBC + C_INNER + C_GROUPS * C_STATE
    tri = np.stack([np.tril(np.ones((SSD_L, SSD_L), np.float32)), np.triu(np.ones((SSD_L, SSD_L), np.float32))])
    in_specs = [pl.BlockSpec((tb, gw), lambda b, g: (b + blk0, col_x // gw + g)),
                pl.BlockSpec((tb, C_STATE), lambda b, g: (b + blk0, col_b // C_STATE + g)),
                pl.BlockSpec((tb, C_STATE), lambda b, g: (b + blk0, col_c // C_STATE + g)),
                pl.BlockSpec((tb, gw), lambda b, g: (b + blk0, COL_Z // gw + g)),
                pl.BlockSpec((tb, 128), lambda b, g: (b + blk0, g)),
                pl.BlockSpec((None, CONV_W, gw), lambda b, g: (layer, 0, g)),
                pl.BlockSpec((None, CONV_W, C_STATE), lambda b, g: (layer, 0, C_INNER // C_STATE + g)),
                pl.BlockSpec((None, CONV_W, C_STATE), lambda b, g: (layer, 0, C_INNER // C_STATE + C_GROUPS + g)),
                pl.BlockSpec((None, 1, gw), lambda b, g: (layer, 0, g)),
                pl.BlockSpec((None, 1, C_STATE), lambda b, g: (layer, 0, C_INNER // C_STATE + g)),
                pl.BlockSpec((None, 1, C_STATE), lambda b, g: (layer, 0, C_INNER // C_STATE + C_GROUPS + g)),
                pl.BlockSpec((None, None, 1, 128), lambda b, g: (layer, g, 0, 0)),
                pl.BlockSpec((None, None, 1, 128), lambda b, g: (layer, g, 0, 0)),
                pl.BlockSpec((None, 1, gw), lambda b, g: (layer, 0, g)),
                pl.BlockSpec((None, 1, gw), lambda b, g: (layer, 0, g)),
                pl.BlockSpec(tri.shape, lambda b, g: (0, 0, 0))]
    cw = p["conv_c_w"]
    cbias = p["conv_c_b"].reshape(DEPTH, 1, -1)
    args = [proj, proj, proj, proj, dtp, cw, cw, cw, cbias, cbias, cbias,
            p["dt_bias_g"], p["a_log_g"], p["ssd_d_rep"], p["ssd_norm_g"].reshape(DEPTH, 1, C_INNER),
            jnp.asarray(tri)]
    out_specs = [pl.BlockSpec((tb, gw), lambda b, g: (b + blk0, g))]
    out_shape = [jax.ShapeDtypeStruct((proj.shape[0], C_INNER), BF16)]
    if chained:
        in_specs.append(pl.BlockSpec((None, None, 2, hpg, C_HEADDIM, C_STATE),
                                     lambda b, g: (b, layer, 0, g, 0, 0)))
        args.append(state)
    else:
        n_seq = tb // seq_len
        out_specs.append(pl.BlockSpec((n_seq, 2, hpg, C_HEADDIM, C_STATE), lambda b, g: (b, 0, g, 0, 0)))
        out_shape.append(jax.ShapeDtypeStruct((n_blk * n_seq, 2, C_HEADS, C_HEADDIM, C_STATE), F32))
    aliases = _alias_out(prev, in_specs, args)
    res = pl.pallas_call(
        functools.partial(_ssd_kernel, seq_len=seq_len, row_len=row_len, chained=chained,
                          aliased=prev is not None),
        grid=(n_blk, C_GROUPS),
        in_specs=in_specs, out_specs=out_specs, out_shape=out_shape,
        input_output_aliases=aliases,
        scratch_shapes=[pltpu.VMEM((tb, gw), F32), pltpu.VMEM((tb, C_STATE), F32), pltpu.VMEM((tb, C_STATE), F32),
                        pltpu.VMEM((tb, 128), F32), pltpu.VMEM((tb, gw), F32), pltpu.VMEM((tb, gw), F32),
                        pltpu.VMEM((2, hpg, C_HEADDIM, C_STATE), F32)],
        compiler_params=_cp("arbitrary", "arbitrary"),
        name="ssd",
    )(*args)
    return res[0], (None if chained else res[1])


def _route_kernel(lg_ref, bias_ref, e_ref, w_ref):
    lg = lg_ref[...]
    ne = lg.shape[0]
    epg = ne // N_EXPERT_GROUPS
    mx = jnp.max(lg, axis=0, keepdims=True)
    ex = jnp.exp(lg - mx)
    probs = ex / jnp.sum(ex, axis=0, keepdims=True)
    sel = probs + bias_ref[...]
    rows = [sel[e:e + 1] for e in range(ne)]
    top2 = []
    for e in range(ne):
        g0 = (e // epg) * epg
        rank = jnp.zeros_like(rows[e])
        for o in range(g0, g0 + epg):
            if o == e:
                continue
            ahead = (rows[o] > rows[e]) | ((rows[o] == rows[e]) & (o < e))
            rank = rank + jnp.where(ahead, 1.0, 0.0)
        top2.append(rank < 1.5)
    score = []
    for g in range(N_EXPERT_GROUPS):
        sc = jnp.zeros_like(rows[0])
        for e in range(g * epg, (g + 1) * epg):
            sc = sc + jnp.where(top2[e], rows[e], 0.0)
        score.append(sc)
    best = []
    for g in range(N_EXPERT_GROUPS):
        ok = jnp.ones(rows[0].shape, jnp.bool_)
        for o in range(N_EXPERT_GROUPS):
            if o < g:
                ok = ok & (score[g] > score[o])
            elif o > g:
                ok = ok & (score[g] >= score[o])
        best.append(ok)
    first = jnp.full(rows[0].shape, float(ne), F32)
    second = jnp.full(rows[0].shape, -1.0, F32)
    p_first = jnp.zeros_like(rows[0])
    p_second = jnp.zeros_like(rows[0])
    for e in range(ne - 1, -1, -1):
        ch = top2[e] & best[e // epg]
        first = jnp.where(ch, float(e), first)
        p_first = jnp.where(ch, probs[e:e + 1], p_first)
    for e in range(ne):
        ch = top2[e] & best[e // epg]
        second = jnp.where(ch, float(e), second)
        p_second = jnp.where(ch, probs[e:e + 1], p_second)
    tot = p_first + p_second
    e_ref[0:1, :] = first.astype(jnp.int32)
    e_ref[1:2, :] = second.astype(jnp.int32)
    w_ref[0:1, :] = p_first / tot
    w_ref[1:2, :] = p_second / tot


def _route(logits_t, router_bias):
    ne, n = logits_t.shape
    return pl.pallas_call(
        _route_kernel,
        grid=(1,),
        in_specs=[pl.BlockSpec((ne, n), lambda i: (0, 0)), pl.BlockSpec((ne, 1), lambda i: (0, 0))],
        out_specs=[pl.BlockSpec((2, n), lambda i: (0, 0)), pl.BlockSpec((2, n), lambda i: (0, 0))],
        out_shape=[jax.ShapeDtypeStruct((2, n), jnp.int32), jax.ShapeDtypeStruct((2, n), F32)],
        compiler_params=_cp("arbitrary"),
        name="route",
    )(logits_t, router_bias.reshape(ne, 1))


def _dispatch_plan(e2, tm, n_tiles):
    n = e2.shape[1]
    e_flat = e2.reshape(-1)
    onehot = (e_flat[:, None] == jnp.arange(N_EXPERTS, dtype=jnp.int32)[None, :]).astype(jnp.int32)
    counts = jnp.sum(onehot, axis=0)
    rank = jnp.sum(jnp.cumsum(onehot, axis=0) * onehot, axis=1) - 1
    tiles_per = (counts + tm - 1) // tm
    tile_end = jnp.cumsum(tiles_per)
    starts = (tile_end - tiles_per) * tm
    dest = jnp.sum(onehot * starts[None, :], axis=1) + rank
    src = jnp.zeros((n_tiles * tm,), jnp.int32).at[dest].set(jnp.arange(2 * n, dtype=jnp.int32) % n)
    n_used = tile_end[-1]
    t_idx = jnp.minimum(jnp.arange(n_tiles, dtype=jnp.int32), n_used - 1)
    tile_expert = jnp.sum((tile_end[None, :] <= t_idx[:, None]).astype(jnp.int32), axis=1)
    tile_expert = jnp.minimum(tile_expert, N_EXPERTS - 1)
    return dest, src, tile_expert, n_used.reshape(1).astype(jnp.int32)


def _moe_kernel(te_ref, nu_ref, xs_ref, wg_ref, wu_ref, wd_ref, o_ref, acc_ref):
    t, j = pl.program_id(0), pl.program_id(1)

    @pl.when(j == 0)
    def _():
        acc_ref[...] = jnp.zeros_like(acc_ref)

    @pl.when(t < nu_ref[0])
    def _():
        x = xs_ref[...]
        hg = _dot(x, wg_ref[...].astype(BF16))
        hu = _dot(x, wu_ref[...].astype(BF16))
        act = (_silu(hg) * hu).astype(BF16)
        acc_ref[...] += _dot(act, wd_ref[...].astype(BF16))

    @pl.when(j == pl.num_programs(1) - 1)
    def _():
        o_ref[...] = acc_ref[...].astype(o_ref.dtype)


def _moe(xs, tile_expert, n_used, w_gate, w_up, w_down, layer, tm=MOE_TM, tf=MOE_TF):
    r, d = xs.shape
    n_tiles = r // tm
    nj = D_EXPERT // tf

    def jj(t, j, nu):
        return jnp.where(t < nu[0], j, nj - 1)

    grid_spec = pltpu.PrefetchScalarGridSpec(
        num_scalar_prefetch=2,
        grid=(n_tiles, nj),
        in_specs=[pl.BlockSpec((tm, d), lambda t, j, te, nu: (jnp.minimum(t, nu[0] - 1), 0)),
                  pl.BlockSpec((None, None, d, tf), lambda t, j, te, nu: (layer, te[t], 0, jj(t, j, nu))),
                  pl.BlockSpec((None, None, d, tf), lambda t, j, te, nu: (layer, te[t], 0, jj(t, j, nu))),
                  pl.BlockSpec((None, None, tf, d), lambda t, j, te, nu: (layer, te[t], jj(t, j, nu), 0))],
        out_specs=pl.BlockSpec((tm, d), lambda t, j, te, nu: (t, 0)),
        scratch_shapes=[pltpu.VMEM((tm, d), F32)],
    )
    return pl.pallas_call(
        _moe_kernel,
        grid_spec=grid_spec,
        out_shape=jax.ShapeDtypeStruct((r, d), BF16),
        compiler_params=_cp("arbitrary", "arbitrary"),
        name="moe_experts",
    )(tile_expert, n_used, xs, w_gate, w_up, w_down)


def _combine_kernel(x_ref, y0_ref, y1_ref, w_ref, ga_ref, o_ref):
    w = w_ref[...]
    y = w[:, 0:1] * y0_ref[...].astype(F32) + w[:, 1:2] * y1_ref[...].astype(F32)
    o_ref[...] = x_ref[...] + ga_ref[...] * y


def _combine(x, yg, w2, mod_l, ga_chunk, tm=512):
    n, d = x.shape
    nt = n // tm
    row = pl.BlockSpec((tm, d), lambda i: (i, 0))
    return pl.pallas_call(
        _combine_kernel,
        grid=(nt,),
        in_specs=[row, row, pl.BlockSpec((tm, d), lambda i: (i + nt, 0)),
                  pl.BlockSpec((tm, 2), lambda i: (i, 0)),
                  pl.BlockSpec((None, 1, d), lambda i: (_cond_row(i, tm), 0, ga_chunk))],
        out_specs=row,
        out_shape=jax.ShapeDtypeStruct((n, d), F32),
        compiler_params=_cp("arbitrary"),
        name="combine",
    )(x, yg, yg, w2, mod_l)


def _layer(x, l, mod_l, p, lbs, w_router, router_bias, states):
    n = x.shape[0]
    state_hgrn, state_rglru, state_ssd = states
    h = _normmod(x, p["norm1_g"][l], mod_l, chunk=(0, 1))
    proj = _mm(h, p["w_in"], l, 0, N_MAIN)
    dtp = _mm(h, p["w_dt"], l, 0, 512, tn=512)
    gates = _mm(h, p["w_merge"], l, 0, 3 * D_MODEL, act="sigmoid", out_dtype=BF16)
    nb = n // TOK_BLOCK
    ns = nb - N_CTX_BLOCKS
    oa, hg_new = _hgrn(proj, lbs[l], p["hgrn_norm_g"][l], 0, N_CTX_BLOCKS, SEQ)
    oa, _ = _hgrn(proj, lbs[l], p["hgrn_norm_g"][l], N_CTX_BLOCKS, ns, TOK_BLOCK, state_hgrn, l, prev=oa)
    ob, lru_new = _lru(proj, p, l, 0, N_CTX_BLOCKS, SEQ)
    ob, _ = _lru(proj, p, l, N_CTX_BLOCKS, ns, GRID_W, state_rglru, prev=ob)
    oc, ssd_new = _ssd(proj, dtp, p, l, 0, N_CTX_BLOCKS, SEQ, SEQ)
    oc, _ = _ssd(proj, dtp, p, l, N_CTX_BLOCKS, ns, TOK_BLOCK, GRID_W, state_ssd, prev=oc)
    merged = _merge(oa, ob, oc, p["w_branch_a"], p["w_branch_b"], p["w_branch_c"], gates, l)
    x = _mm_residual(merged, p["w_out"], l, x, mod_l, 2)
    h2, logits = _normmod(x, p["norm2_g"][l], mod_l, chunk=(3, 4), w_router=w_router)
    e2, w2 = _route(logits.T, router_bias)
    n_tiles = (2 * n) // MOE_TM + N_EXPERTS
    dest, src, tile_expert, n_used = _dispatch_plan(e2, MOE_TM, n_tiles)
    xs = jnp.take(h2, src, axis=0)
    y = _moe(xs, tile_expert, n_used, p["w_e_gate"], p["w_e_up"], p["w_e_down"], l)
    yg = jnp.take(y, dest, axis=0)
    x = _combine(x, yg, w2.T, mod_l, 5)
    return x, hg_new, lru_new.transpose(1, 0, 2), ssd_new


def _hgrn_lower_bounds(lb_raw):
    pr = jax.nn.softmax(lb_raw.astype(F32), axis=0)
    cum = jnp.cumsum(pr, axis=0)
    return cum - cum[0]


def kernel(x_prompt, x_sample, state_hgrn, state_rglru, state_ssd, c, c_ctx, w_mod, b_mod, norm1_g, norm2_g, w_in, hgrn_lb, hgrn_norm_g, conv_b_w, conv_b_b, lru_wa, lru_ba, lru_wx, lru_bx, lru_lambda, conv_c_w, conv_c_b, ssd_a_log, ssd_dt_bias, ssd_d, ssd_norm_g, w_branch_a, w_branch_b, w_branch_c, w_out, w_router, router_bias, w_e_gate, w_e_up, w_e_down, final_g):
    bsz, seq, d = x_prompt.shape
    dbsz, dseq, _ = x_sample.shape
    assert seq == SEQ and dseq == TOK_BLOCK and (bsz * seq) == N_CTX_BLOCKS * TOK_BLOCK and d == D_MODEL
    x = jnp.concatenate([x_prompt.reshape(-1, d), x_sample.reshape(-1, d)], axis=0)

    cond8 = jnp.zeros((8, d), F32).at[0].set(c_ctx).at[1:1 + dbsz].set(c)
    mod = _modulation(cond8, w_mod, b_mod).reshape(DEPTH, 8, 1, N_MOD * d)

    hpg = C_HEADS // C_GROUPS
    w_dt_raw = w_in[:, :, COL_DT:COL_DT + 2 * C_HEADS].reshape(DEPTH, d, 2, C_GROUPS, hpg)
    w_dt = jnp.zeros((DEPTH, d, C_GROUPS, 128), F32).at[:, :, :, :2 * hpg].set(
        w_dt_raw.transpose(0, 1, 3, 2, 4).reshape(DEPTH, d, C_GROUPS, 2 * hpg)).reshape(DEPTH, d, C_GROUPS * 128)

    def group_lanes(a):
        g = a.reshape(DEPTH, 2, C_GROUPS, hpg).transpose(0, 2, 1, 3).reshape(DEPTH, C_GROUPS, 1, 2 * hpg)
        return jnp.zeros((DEPTH, C_GROUPS, 1, 128), F32).at[..., :2 * hpg].set(g)

    p = dict(norm1_g=norm1_g, norm2_g=norm2_g, w_in=w_in, w_dt=w_dt,
             w_merge=w_in[:, :, COL_MERGE:], hgrn_norm_g=hgrn_norm_g,
             conv_b_w=conv_b_w, conv_b_b=conv_b_b, lru_wa=lru_wa, lru_ba=lru_ba, lru_wx=lru_wx,
             lru_bx=lru_bx, lru_lambda=lru_lambda, conv_c_w=conv_c_w, conv_c_b=conv_c_b,
             dt_bias_g=group_lanes(ssd_dt_bias), a_log_g=group_lanes(ssd_a_log),
             ssd_d_rep=jnp.repeat(ssd_d, C_HEADDIM, axis=-1).reshape(DEPTH, 1, C_INNER),
             ssd_norm_g=ssd_norm_g, w_branch_a=w_branch_a, w_branch_b=w_branch_b, w_branch_c=w_branch_c,
             w_out=w_out, w_e_gate=w_e_gate, w_e_up=w_e_up, w_e_down=w_e_down)
    lbs = _hgrn_lower_bounds(hgrn_lb)

    hg_list, lru_list, ssd_list = [], [], []
    for l in range(DEPTH):
        x, hg_new, lru_new, ssd_new = _layer(x, l, mod[l], p, lbs, w_router, router_bias,
                                             (state_hgrn, state_rglru, state_ssd))
        hg_list.append(hg_new)
        lru_list.append(lru_new)
        ssd_list.append(ssd_new)
    y = _normmod(x, final_g, out_dtype=F32)
    n_ctx = bsz * seq
    y_prompt = y[:n_ctx].reshape(bsz, seq, d)
    y_sample = y[n_ctx:].reshape(dbsz, dseq, d)
    return (y_prompt, y_sample, jnp.stack(hg_list, axis=1), jnp.stack(lru_list, axis=1),
            jnp.stack(ssd_list, axis=1))
```

```python
import functools
import math

import numpy as np
import jax
import jax.numpy as jnp
from jax import lax
from jax.experimental import pallas as pl
from jax.experimental.pallas import tpu as pltpu

F32 = jnp.float32
BF16 = jnp.bfloat16
HIGHEST = lax.Precision.HIGHEST

EPS = 1e-6
D_MODEL = 2048
DEPTH = 4
N_MOD = 6
GRID_W = 64
SEQ = 256
TOK_BLOCK = 2048
N_CTX_BLOCKS = 2

A_HEADS = 8
A_DK = 128
A_DV = 128
A_WIDTH = 1024
HGRN_C = 128

B_WIDTH = 1024
B_BLOCK = 128
LRU_C = 8.0
LRU_SEG = 256

C_INNER = 1024
C_HEADDIM = 64
C_HEADS = 16
C_GROUPS = 4
C_STATE = 128
SSD_L = 128
CONV_W = 4

N_EXPERTS = 16
N_EXPERT_GROUPS = 4
D_EXPERT = 1024
MOE_TM = 1024
MOE_TF = 256

COL_Q, COL_FF, COL_FB, COL_V, COL_GA = 0, 1024, 2048, 3072, 4096
COL_XB, COL_GB, COL_Z, COL_XBC, COL_DT, COL_MERGE = 5120, 6144, 7168, 8192, 10240, 10272
N_MAIN = 10240

VMEM_LIMIT_BYTES = 56 * 1024 * 1024
NEG_BIG = -1e30


def _cp(*sem):
    return pltpu.CompilerParams(dimension_semantics=sem, vmem_limit_bytes=VMEM_LIMIT_BYTES)


def _silu(x):
    return x * jax.nn.sigmoid(x)


def _dot(a, b):
    return jnp.dot(a, b, preferred_element_type=F32)


def _dot_nt(a, b):
    return lax.dot_general(a, b, (((1,), (1,)), ((), ())), preferred_element_type=F32)


def _dot_tn(a, b):
    return lax.dot_general(a, b, (((0,), (0,)), ((), ())), preferred_element_type=F32)


def _dot_hi(a, b):
    return jnp.dot(a, b, preferred_element_type=F32, precision=HIGHEST)


def _cond_row(i, tm):
    return jnp.maximum((i * tm) // TOK_BLOCK - (N_CTX_BLOCKS - 1), 0)


def _mod_kernel(c_ref, w_ref, b_ref, o_ref):
    s = _silu(c_ref[...]).astype(BF16)
    o_ref[...] = _dot(s, w_ref[...].astype(BF16)) + b_ref[...]


def _modulation(cond8, w_mod, b_mod):
    depth, d, n = w_mod.shape
    tn = 1024
    return pl.pallas_call(
        _mod_kernel,
        grid=(depth, n // tn),
        in_specs=[pl.BlockSpec((8, d), lambda l, j: (0, 0)),
                  pl.BlockSpec((None, d, tn), lambda l, j: (l, 0, j)),
                  pl.BlockSpec((None, 1, tn), lambda l, j: (l, 0, j))],
        out_specs=pl.BlockSpec((None, 8, tn), lambda l, j: (l, 0, j)),
        out_shape=jax.ShapeDtypeStruct((depth, 8, n), F32),
        compiler_params=_cp("arbitrary", "arbitrary"),
        name="modulation",
    )(cond8, w_mod, b_mod.reshape(depth, 1, n))


def _normmod_kernel(*refs, modulate, router):
    it = iter(refs)
    x_ref, g_ref = next(it), next(it)
    sc_ref = sh_ref = wr_ref = lg_ref = None
    if modulate:
        sc_ref, sh_ref = next(it), next(it)
    if router:
        wr_ref = next(it)
    h_ref = next(it)
    if router:
        lg_ref = next(it)
    x = x_ref[...]
    h = x * lax.rsqrt(jnp.mean(x * x, axis=-1, keepdims=True) + EPS) * g_ref[...]
    if modulate:
        h = h * (1.0 + sc_ref[...]) + sh_ref[...]
    h_ref[...] = h.astype(h_ref.dtype)
    if router:
        lg_ref[...] = _dot_hi(h, wr_ref[...])


def _normmod(x, g, mod_l=None, chunk=None, w_router=None, out_dtype=BF16, tm=512, row0=0, n_rows=None):
    d = x.shape[1]
    n = x.shape[0] if n_rows is None else n_rows
    r0 = row0 // tm
    modulate = mod_l is not None
    router = w_router is not None
    assert not (modulate and row0)
    in_specs = [pl.BlockSpec((tm, d), lambda i: (i + r0, 0)),
                pl.BlockSpec((1, d), lambda i: (0, 0))]
    args = [x, g.reshape(1, d)]
    if modulate:
        sh_c, sc_c = chunk
        in_specs += [pl.BlockSpec((None, 1, d), lambda i: (_cond_row(i, tm), 0, sc_c)),
                     pl.BlockSpec((None, 1, d), lambda i: (_cond_row(i, tm), 0, sh_c))]
        args += [mod_l, mod_l]
    out_specs = [pl.BlockSpec((tm, d), lambda i: (i, 0))]
    out_shape = [jax.ShapeDtypeStruct((n, d), out_dtype)]
    if router:
        ne = w_router.shape[1]
        in_specs.append(pl.BlockSpec((d, ne), lambda i: (0, 0)))
        args.append(w_router)
        out_specs.append(pl.BlockSpec((tm, ne), lambda i: (i, 0)))
        out_shape.append(jax.ShapeDtypeStruct((n, ne), F32))
    res = pl.pallas_call(
        functools.partial(_normmod_kernel, modulate=modulate, router=router),
        grid=(n // tm,),
        in_specs=in_specs, out_specs=out_specs, out_shape=out_shape,
        compiler_params=_cp("arbitrary"),
        name="normmod",
    )(*args)
    return res if router else res[0]


def _mm_kernel(a_ref, w_ref, o_ref, *, act):
    acc = _dot(a_ref[...], w_ref[...].astype(BF16))
    if act == "sigmoid":
        acc = jax.nn.sigmoid(acc)
    o_ref[...] = acc.astype(o_ref.dtype)


def _mm(a, w, layer, col0, n_out, act=None, out_dtype=F32, tm=2048, tn=512):
    m, k = a.shape
    tn = min(tn, n_out)
    off = col0 // tn
    assert col0 % tn == 0 and n_out % tn == 0 and m % tm == 0
    if layer is None:
        w_spec = pl.BlockSpec((k, tn), lambda i, j: (0, j + off))
    else:
        w_spec = pl.BlockSpec((None, k, tn), lambda i, j: (layer, 0, j + off))
    return pl.pallas_call(
        functools.partial(_mm_kernel, act=act),
        grid=(m // tm, n_out // tn),
        in_specs=[pl.BlockSpec((tm, k), lambda i, j: (i, 0)), w_spec],
        out_specs=pl.BlockSpec((tm, tn), lambda i, j: (i, j)),
        out_shape=jax.ShapeDtypeStruct((m, n_out), out_dtype),
        compiler_params=_cp("arbitrary", "arbitrary"),
        name="matmul",
    )(a, w)


def _mm_res_kernel(a_ref, w_ref, x_ref, ga_ref, o_ref):
    acc = _dot(a_ref[...], w_ref[...].astype(BF16))
    o_ref[...] = x_ref[...] + ga_ref[...] * acc


def _mm_residual(a, w, layer, x, mod_l, ga_chunk, tm=1024, tn=512):
    m, k = a.shape
    n = w.shape[-1]
    gs = n // tn
    return pl.pallas_call(
        _mm_res_kernel,
        grid=(m // tm, n // tn),
        in_specs=[pl.BlockSpec((tm, k), lambda i, j: (i, 0)),
                  pl.BlockSpec((None, k, tn), lambda i, j: (layer, 0, j)),
                  pl.BlockSpec((tm, tn), lambda i, j: (i, j)),
                  pl.BlockSpec((None, 1, tn), lambda i, j: (_cond_row(i, tm), 0, ga_chunk * gs + j))],
        out_specs=pl.BlockSpec((tm, tn), lambda i, j: (i, j)),
        out_shape=jax.ShapeDtypeStruct((m, n), F32),
        compiler_params=_cp("arbitrary", "arbitrary"),
        name="out_proj",
    )(a, w, x, mod_l)


def _merge_kernel(oa_ref, ob_ref, oc_ref, wa_ref, wb_ref, wc_ref, g1_ref, g2_ref, g3_ref, o_ref):
    ya = _dot(oa_ref[...], wa_ref[...].astype(BF16))
    yb = _dot(ob_ref[...], wb_ref[...].astype(BF16))
    yc = _dot(oc_ref[...], wc_ref[...].astype(BF16))
    m = (g1_ref[...].astype(F32) * ya + g2_ref[...].astype(F32) * yb + g3_ref[...].astype(F32) * yc)
    o_ref[...] = m.astype(o_ref.dtype)


def _merge(oa, ob, oc, wa, wb, wc, gates, layer, tm=1024, tn=512):
    m, k = oa.shape
    n = wa.shape[-1]
    gs = n // tn
    a_spec = pl.BlockSpec((tm, k), lambda i, j: (i, 0))
    w_spec = pl.BlockSpec((None, k, tn), lambda i, j: (layer, 0, j))
    return pl.pallas_call(
        _merge_kernel,
        grid=(m // tm, n // tn),
        in_specs=[a_spec, a_spec, a_spec, w_spec, w_spec, w_spec,
                  pl.BlockSpec((tm, tn), lambda i, j: (i, j)),
                  pl.BlockSpec((tm, tn), lambda i, j: (i, j + gs)),
                  pl.BlockSpec((tm, tn), lambda i, j: (i, j + 2 * gs))],
        out_specs=pl.BlockSpec((tm, tn), lambda i, j: (i, j)),
        out_shape=jax.ShapeDtypeStruct((m, n), BF16),
        compiler_params=_cp("arbitrary", "arbitrary"),
        name="merge",
    )(oa, ob, oc, wa, wb, wc, gates, gates, gates)


def _dwconv(x, w, b, row_len):
    t = x.shape[0]
    pos = lax.broadcasted_iota(jnp.int32, x.shape, 0) % row_len
    xm2 = jnp.where(pos >= 2, pltpu.roll(x, 2, axis=0), 0.0)
    xm1 = jnp.where(pos >= 1, pltpu.roll(x, 1, axis=0), 0.0)
    xp1 = jnp.where(pos <= row_len - 2, pltpu.roll(x, t - 1, axis=0), 0.0)
    y = b + xm2 * w[0:1]
    y = y + xm1 * w[1:2]
    y = y + x * w[2:3]
    y = y + xp1 * w[3:4]
    return y


def _hgrn_consts(c):
    nl = int(math.log2(c))
    seg = np.zeros((2, nl + 1, c, c), np.float32)
    msk = np.zeros((2, nl + 1, c, c), np.float32)
    up = np.zeros((2, nl, c, 128), np.float32)
    for d in range(2):
        tt = np.arange(c) if d == 0 else c - 1 - np.arange(c)
        tr, tc = tt[:, None], tt[None, :]
        seg[d, 0] = tc <= tr
        for l in range(nl):
            s = 1 << l
            blk, upper = tt // (2 * s), (tt % (2 * s)) >= s
            mid = (blk * 2 * s + s)[:, None]
            seg_u = (tc >= mid) & (tc <= tr)
            seg_l = (tc >= tr + 1) & (tc <= mid - 1)
            seg[d, l + 1] = np.where(upper[:, None], seg_u, seg_l)
            msk[d, l] = upper[:, None] & (~upper)[None, :] & (blk[:, None] == blk[None, :])
            up[d, l] = upper[:, None]
        msk[d, nl] = np.eye(c)
    return seg.reshape(2, (nl + 1) * c, c), msk, up


def _hgrn_kernel(*refs, seq_len, chained, aliased):
    it = iter(refs)
    q_ref, ff_ref, fb_ref, v_ref, ga_ref, lb_ref, ng_ref = (next(it) for _ in range(7))
    seg_ref, msk_ref, up_ref = next(it), next(it), next(it)
    s0_ref = next(it) if chained else None
    for _ in range(aliased):
        next(it)
    o_ref = next(it)
    sout_ref = None if chained else next(it)
    o_scr, qp_scr, u_scr, dec_scr, snap_scr = (next(it) for _ in range(5))

    c = HGRN_C
    nl = int(math.log2(c))
    tb = q_ref.shape[0]
    n_chunks = tb // c
    n_ch = seq_len // c
    lb = lb_ref[...]
    f_refs = (ff_ref, fb_ref)

    def phase_a(n, carry):
        rows = pl.ds(pl.multiple_of(n * c, c), c)
        q = _silu(q_ref[rows, :]) * (A_DK ** -0.5)
        qb = q.astype(BF16)
        v = v_ref[rows, :].astype(BF16)
        o = jnp.zeros((c, A_DV), F32)
        qps, kps, decs = [], [], []
        for d in range(2):
            f = lb + (1.0 - lb) * jax.nn.sigmoid(f_refs[d][rows, :])
            k = 1.0 - f
            lf = jnp.log(f)
            hi = lf.astype(BF16)
            mid = (lf - hi.astype(F32)).astype(BF16)
            e2 = _dot(seg_ref[d], jnp.concatenate([hi, mid], axis=1))
            e = e2[:, :A_DK] + e2[:, A_DK:]
            b = e[0:c]
            att = msk_ref[d, nl] * _dot_nt(qb, k.astype(BF16))
            for l in range(nl):
                x = (jnp.where(up_ref[d, l] > 0.5, q, k) * jnp.exp(e[(l + 1) * c:(l + 2) * c])).astype(BF16)
                att = att + msk_ref[d, l] * _dot_nt(x, x)
            o = o + _dot(att.astype(BF16), v)
            b_end = b[c - 1:c] if d == 0 else b[0:1]
            qps.append((q * jnp.exp(b)).astype(BF16))
            kps.append((k * jnp.exp(b_end - b)).astype(BF16))
            decs.append(jnp.exp(b_end))
        o_scr[rows, :] = o
        qp_scr[rows, :] = jnp.concatenate(qps, axis=1)
        u_scr[n] = _dot_tn(v, jnp.concatenate(kps, axis=1))
        dec_scr[n] = jnp.concatenate(decs, axis=1)
        return carry

    lax.fori_loop(0, n_chunks, phase_a, 0)

    def init_state(d):
        return s0_ref[d].T if chained else jnp.zeros((A_DV, A_DK), F32)

    def phase_b1(j, carry):
        sf, sb = carry
        nf = j
        nb = n_chunks - 1 - j
        if not chained:
            sf = jnp.where(nf % n_ch == 0, 0.0, sf)
            sb = jnp.where(nb % n_ch == n_ch - 1, 0.0, sb)
        snap_scr[nf, :, 0:A_DK] = sf.astype(BF16)
        snap_scr[nb, :, A_DK:2 * A_DK] = sb.astype(BF16)
        sf = sf * dec_scr[nf][:, 0:A_DK] + u_scr[nf][:, 0:A_DK]
        sb = sb * dec_scr[nb][:, A_DK:2 * A_DK] + u_scr[nb][:, A_DK:2 * A_DK]
        if not chained:
            @pl.when(nf % n_ch == n_ch - 1)
            def _():
                sout_ref[nf // n_ch, 0] = sf.T

            @pl.when(nb % n_ch == 0)
            def _():
                sout_ref[nb // n_ch, 1] = sb.T
        return sf, sb

    lax.fori_loop(0, n_chunks, phase_b1, (init_state(0), init_state(1)))

    ng = ng_ref[...]

    def phase_b2(n, carry):
        rows = pl.ds(pl.multiple_of(n * c, c), c)
        o = o_scr[rows, :] + _dot_nt(qp_scr[rows, :], snap_scr[n])
        o = o * lax.rsqrt(jnp.mean(o * o, axis=-1, keepdims=True) + EPS) * ng
        o_ref[rows, :] = (o * _silu(ga_ref[rows, :])).astype(o_ref.dtype)
        return carry

    lax.fori_loop(0, n_chunks, phase_b2, 0, unroll=4)


def _alias_out(prevs, in_specs, args):
    aliases = {}
    for out_idx, prev in prevs:
        if prev is not None:
            in_specs.append(pl.BlockSpec(memory_space=pl.ANY))
            args.append(prev)
            aliases[len(args) - 1] = out_idx
    return aliases


def _hgrn(proj, lb_l, ng_l, blk0, n_blk, seq_len, state=None, layer=0, prev=None, st_prev=None):
    tb = TOK_BLOCK
    chained = state is not None
    seg, msk, up = _hgrn_consts(HGRN_C)
    cb = lambda col: col // A_DK

    def col_spec(col):
        return pl.BlockSpec((tb, A_DK), lambda b, h: (b + blk0, cb(col) + h))

    def full(a):
        nd = a.ndim
        return pl.BlockSpec(a.shape, lambda b, h: (0,) * nd)

    in_specs = [col_spec(COL_Q), col_spec(COL_FF), col_spec(COL_FB), col_spec(COL_V), col_spec(COL_GA),
                pl.BlockSpec((1, A_DK), lambda b, h: (0, h)),
                pl.BlockSpec((1, A_DV), lambda b, h: (0, 0)),
                full(seg), full(msk), full(up)]
    args = [proj, proj, proj, proj, proj, lb_l.reshape(1, -1), ng_l.reshape(1, -1),
            jnp.asarray(seg, BF16), jnp.asarray(msk), jnp.asarray(up)]
    out_specs = [pl.BlockSpec((tb, A_DV), lambda b, h: (b + blk0, h))]
    out_shape = [jax.ShapeDtypeStruct((proj.shape[0], A_WIDTH), BF16)]
    if chained:
        in_specs.append(pl.BlockSpec((None, None, 2, None, A_DK, A_DV), lambda b, h: (b, layer, 0, h, 0, 0)))
        args.append(state)
    else:
        n_seq = tb // seq_len
        out_specs.append(pl.BlockSpec((n_seq, None, 2, None, A_DK, A_DV), lambda b, h: (b, layer, 0, h, 0, 0)))
        out_shape.append(jax.ShapeDtypeStruct((n_blk * n_seq, DEPTH, 2, A_HEADS, A_DK, A_DV), F32))
    aliases = _alias_out([(0, prev), (1, st_prev)], in_specs, args)
    n_chunks = tb // HGRN_C
    res = pl.pallas_call(
        functools.partial(_hgrn_kernel, seq_len=seq_len, chained=chained, aliased=len(aliases)),
        grid=(n_blk, A_HEADS),
        in_specs=in_specs, out_specs=out_specs, out_shape=out_shape,
        input_output_aliases=aliases,
        scratch_shapes=[pltpu.VMEM((tb, A_DV), F32), pltpu.VMEM((tb, 2 * A_DK), BF16),
                        pltpu.VMEM((n_chunks, A_DV, 2 * A_DK), F32), pltpu.VMEM((n_chunks, 1, 2 * A_DK), F32),
                        pltpu.VMEM((n_chunks, A_DV, 2 * A_DK), BF16)],
        compiler_params=_cp("arbitrary", "arbitrary"),
        name="hgrn2",
    )(*args)
    return res[0], (None if chained else res[1])


def _lru_kernel(*refs, row_len, chained, aliased):
    it = iter(refs)
    x_ref, gb_ref, cw_ref, cb_ref, wa_ref, ba_ref, wx_ref, bx_ref, lam_ref = (next(it) for _ in range(9))
    h0_ref = next(it) if chained else None
    for _ in range(aliased):
        next(it)
    o_ref = next(it)
    hout_ref = None if chained else next(it)
    a_scr, u_scr, h_scr, p_scr = (next(it) for _ in range(4))

    tb = x_ref.shape[0]
    seg = LRU_SEG
    n_seg = tb // seg
    xc = _dwconv(x_ref[...], cw_ref[...], cb_ref[...], row_len)
    xcb = xc.astype(BF16)
    for d in range(2):
        r = jax.nn.sigmoid(_dot(xcb, wa_ref[d].astype(BF16)) + ba_ref[d])
        g = jax.nn.sigmoid(_dot(xcb, wx_ref[d].astype(BF16)) + bx_ref[d])
        log_a = LRU_C * r * jax.nn.log_sigmoid(lam_ref[d])
        a = jnp.exp(log_a)
        a_scr[d] = a
        u_scr[d] = jnp.sqrt(1.0 - a * a) * (g * xc)

    def step(i, carry):
        hf, pf, hb, pb = carry
        tf = i
        tr = seg - 1 - i
        af = a_scr[0, pl.ds(tf, n_seg, stride=seg), :]
        hf = af * hf + u_scr[0, pl.ds(tf, n_seg, stride=seg), :]
        h_scr[0, pl.ds(tf, n_seg, stride=seg), :] = hf
        ab = a_scr[1, pl.ds(tr, n_seg, stride=seg), :]
        hb = ab * hb + u_scr[1, pl.ds(tr, n_seg, stride=seg), :]
        h_scr[1, pl.ds(tr, n_seg, stride=seg), :] = hb
        if chained:
            pf = af * pf
            pb = ab * pb
            p_scr[0, pl.ds(tf, n_seg, stride=seg), :] = pf
            p_scr[1, pl.ds(tr, n_seg, stride=seg), :] = pb
        return hf, pf, hb, pb

    zeros = jnp.zeros((n_seg, B_BLOCK), F32)
    ones = jnp.ones((n_seg, B_BLOCK), F32)
    hf, pf, hb, pb = lax.fori_loop(0, seg, step, (zeros, ones, zeros, ones), unroll=8)

    if chained:
        hin = h0_ref[0:1, :]
        for s in range(n_seg):
            rows = pl.ds(s * seg, seg)
            h_scr[0, rows, :] = h_scr[0, rows, :] + p_scr[0, rows, :] * hin
            hin = hf[s:s + 1] + pf[s:s + 1] * hin
        hin = h0_ref[1:2, :]
        for s in range(n_seg - 1, -1, -1):
            rows = pl.ds(s * seg, seg)
            h_scr[1, rows, :] = h_scr[1, rows, :] + p_scr[1, rows, :] * hin
            hin = hb[s:s + 1] + pb[s:s + 1] * hin
    else:
        hout_ref[0] = hf
        hout_ref[1] = hb
    o_ref[...] = ((h_scr[0] + h_scr[1]) * jax.nn.gelu(gb_ref[...])).astype(o_ref.dtype)


def _lru(proj, p, layer, blk0, n_blk, row_len, state=None, prev=None):
    tb = TOK_BLOCK
    chained = state is not None
    nb = B_WIDTH // B_BLOCK
    cbx, cbg = COL_XB // B_BLOCK, COL_GB // B_BLOCK
    in_specs = [pl.BlockSpec((tb, B_BLOCK), lambda b, n: (b + blk0, cbx + n)),
                pl.BlockSpec((tb, B_BLOCK), lambda b, n: (b + blk0, cbg + n)),
                pl.BlockSpec((None, CONV_W, B_BLOCK), lambda b, n: (layer, 0, n)),
                pl.BlockSpec((None, 1, B_BLOCK), lambda b, n: (layer, 0, n)),
                pl.BlockSpec((None, 2, None, B_BLOCK, B_BLOCK), lambda b, n: (layer, 0, n, 0, 0)),
                pl.BlockSpec((None, 2, 1, B_BLOCK), lambda b, n: (layer, 0, 0, n)),
                pl.BlockSpec((None, 2, None, B_BLOCK, B_BLOCK), lambda b, n: (layer, 0, n, 0, 0)),
                pl.BlockSpec((None, 2, 1, B_BLOCK), lambda b, n: (layer, 0, 0, n)),
                pl.BlockSpec((None, 2, 1, B_BLOCK), lambda b, n: (layer, 0, 0, n))]
    d4 = lambda a: a.reshape(DEPTH, 2, 1, B_WIDTH)
    args = [proj, proj, p["conv_b_w"], p["conv_b_b"].reshape(DEPTH, 1, B_WIDTH),
            p["lru_wa"], d4(p["lru_ba"]), p["lru_wx"], d4(p["lru_bx"]), d4(p["lru_lambda"])]
    out_specs = [pl.BlockSpec((tb, B_BLOCK), lambda b, n: (b + blk0, n))]
    out_shape = [jax.ShapeDtypeStruct((proj.shape[0], B_WIDTH), BF16)]
    if chained:
        in_specs.append(pl.BlockSpec((None, None, 2, B_BLOCK), lambda b, n: (b, layer, 0, n)))
        args.append(state)
    else:
        n_seq = tb // LRU_SEG
        out_specs.append(pl.BlockSpec((2, n_seq, B_BLOCK), lambda b, n: (0, b, n)))
        out_shape.append(jax.ShapeDtypeStruct((2, n_blk * n_seq, B_WIDTH), F32))
    aliases = _alias_out([(0, prev)], in_specs, args)
    res = pl.pallas_call(
        functools.partial(_lru_kernel, row_len=row_len, chained=chained, aliased=len(aliases)),
        grid=(n_blk, nb),
        in_specs=in_specs, out_specs=out_specs, out_shape=out_shape,
        input_output_aliases=aliases,
        scratch_shapes=[pltpu.VMEM((2, tb, B_BLOCK), F32) for _ in range(4)],
        compiler_params=_cp("arbitrary", "arbitrary"),
        name="rglru",
    )(*args)
    return res[0], (None if chained else res[1])


def _ssd_kernel(*refs, seq_len, row_len, chained, aliased):
    it = iter(refs)
    (xs_ref, bm_ref, cm_ref, z_ref, dt_ref, cwx_ref, cwb_ref, cwc_ref, cbx_ref, cbb_ref, cbc_ref,
     dtb_ref, a_ref, dsk_ref, ng_ref, tri_ref) = (next(it) for _ in range(16))
    s0_ref = next(it) if chained else None
    for _ in range(aliased):
        next(it)
    o_ref = next(it)
    sout_ref = None if chained else next(it)
    xs_scr, bm_scr, cm_scr, dt_scr, yf_scr, yb_scr, st_scr = (next(it) for _ in range(7))

    lc = SSD_L
    hpg = C_HEADS // C_GROUPS
    tb = xs_ref.shape[0]
    n_seq = tb // seq_len
    n_ch = seq_len // lc
    xs_scr[...] = _silu(_dwconv(xs_ref[...], cwx_ref[...], cbx_ref[...], row_len))
    bm_scr[...] = _silu(_dwconv(bm_ref[...], cwb_ref[...], cbb_ref[...], row_len))
    cm_scr[...] = _silu(_dwconv(cm_ref[...], cwc_ref[...], cbc_ref[...], row_len))
    dt_scr[...] = jax.nn.softplus(dt_ref[...] + dtb_ref[...])
    a_row = -jnp.exp(a_ref[...])
    y_scrs = (yf_scr, yb_scr)
    rr = lax.broadcasted_iota(jnp.int32, (lc, lc), 0)
    cc = lax.broadcasted_iota(jnp.int32, (lc, lc), 1)
    causal = (rr >= cc, cc >= rr)

    def chunk(d, r0):
        rows = pl.ds(r0, lc)
        dtc = dt_scr[rows, :]
        cum = _dot_hi(tri_ref[d], dtc * a_row)
        cum_t = cum.T
        xs = xs_scr[rows, :]
        bm = bm_scr[rows, :]
        cm = cm_scr[rows, :].astype(BF16)
        scores = _dot_nt(cm, bm.astype(BF16))
        ys = []
        for hh in range(hpg):
            ln = d * hpg + hh
            col = cum[:, ln:ln + 1]
            seg = jnp.exp(jnp.where(causal[d], col - cum_t[ln:ln + 1, :], NEG_BIG))
            xdt = (xs[:, hh * C_HEADDIM:(hh + 1) * C_HEADDIM] * dtc[:, ln:ln + 1]).astype(BF16)
            st = st_scr[d, hh]
            y = _dot((scores * seg).astype(BF16), xdt)
            y = y + _dot_nt(cm, st.astype(BF16)) * jnp.exp(col)
            end = col[lc - 1:lc] if d == 0 else col[0:1]
            st_scr[d, hh] = jnp.exp(end) * st + _dot_tn(xdt, (bm * jnp.exp(end - col)).astype(BF16))
            ys.append(y)
        y_scrs[d][rows, :] = jnp.concatenate(ys, axis=-1)

    def seq_body(s, carry):
        for d in range(2):
            for hh in range(hpg):
                st_scr[d, hh] = s0_ref[d, hh] if chained else jnp.zeros((C_HEADDIM, C_STATE), F32)

        def ch_body(i, carry2):
            base = s * seq_len
            chunk(0, pl.multiple_of(base + i * lc, lc))
            chunk(1, pl.multiple_of(base + (n_ch - 1 - i) * lc, lc))
            return carry2

        lax.fori_loop(0, n_ch, ch_body, 0)
        if not chained:
            for d in range(2):
                for hh in range(hpg):
                    sout_ref[s, d, hh] = st_scr[d, hh]
        return carry

    lax.fori_loop(0, n_seq, seq_body, 0)
    y = yf_scr[...] + yb_scr[...] + dsk_ref[...] * xs_scr[...]
    y = y * _silu(z_ref[...])
    y = y * lax.rsqrt(jnp.mean(y * y, axis=-1, keepdims=True) + EPS) * ng_ref[...]
    o_ref[...] = y.astype(o_ref.dtype)


def _ssd(proj, dtp, p, layer, blk0, n_blk, seq_len, row_len, state=None, prev=None, st_prev=None):
    tb = TOK_BLOCK
    chained = state is not None
    hpg = C_HEADS // C_GROUPS
    gw = hpg * C_HEADDIM
    col_x, col_b, col_c = COL_XBC, COL_XBC + C_INNER, COL_XBC + C_INNER + C_GROUPS * C_STATE
    tri = np.stack([np.tril(np.ones((SSD_L, SSD_L), np.float32)), np.triu(np.ones((SSD_L, SSD_L), np.float32))])
    in_specs = [pl.BlockSpec((tb, gw), lambda b, g: (b + blk0, col_x // gw + g)),
                pl.BlockSpec((tb, C_STATE), lambda b, g: (b + blk0, col_b // C_STATE + g)),
                pl.BlockSpec((tb, C_STATE), lambda b, g: (b + blk0, col_c // C_STATE + g)),
                pl.BlockSpec((tb, gw), lambda b, g: (b + blk0, COL_Z // gw + g)),
                pl.BlockSpec((tb, 128), lambda b, g: (b + blk0, g)),
                pl.BlockSpec((None, CONV_W, gw), lambda b, g: (layer, 0, g)),
                pl.BlockSpec((None, CONV_W, C_STATE), lambda b, g: (layer, 0, C_INNER // C_STATE + g)),
                pl.BlockSpec((None, CONV_W, C_STATE), lambda b, g: (layer, 0, C_INNER // C_STATE + C_GROUPS + g)),
                pl.BlockSpec((None, 1, gw), lambda b, g: (layer, 0, g)),
                pl.BlockSpec((None, 1, C_STATE), lambda b, g: (layer, 0, C_INNER // C_STATE + g)),
                pl.BlockSpec((None, 1, C_STATE), lambda b, g: (layer, 0, C_INNER // C_STATE + C_GROUPS + g)),
                pl.BlockSpec((None, None, 1, 128), lambda b, g: (layer, g, 0, 0)),
                pl.BlockSpec((None, None, 1, 128), lambda b, g: (layer, g, 0, 0)),
                pl.BlockSpec((None, 1, gw), lambda b, g: (layer, 0, g)),
                pl.BlockSpec((None, 1, gw), lambda b, g: (layer, 0, g)),
                pl.BlockSpec(tri.shape, lambda b, g: (0, 0, 0))]
    cw = p["conv_c_w"]
    cbias = p["conv_c_b"].reshape(DEPTH, 1, -1)
    args = [proj, proj, proj, proj, dtp, cw, cw, cw, cbias, cbias, cbias,
            p["dt_bias_g"], p["a_log_g"], p["ssd_d_rep"], p["ssd_norm_g"].reshape(DEPTH, 1, C_INNER),
            jnp.asarray(tri)]
    out_specs = [pl.BlockSpec((tb, gw), lambda b, g: (b + blk0, g))]
    out_shape = [jax.ShapeDtypeStruct((proj.shape[0], C_INNER), BF16)]
    if chained:
        in_specs.append(pl.BlockSpec((None, None, 2, hpg, C_HEADDIM, C_STATE),
                                     lambda b, g: (b, layer, 0, g, 0, 0)))
        args.append(state)
    else:
        n_seq = tb // seq_len
        out_specs.append(pl.BlockSpec((n_seq, None, 2, hpg, C_HEADDIM, C_STATE),
                                      lambda b, g: (b, layer, 0, g, 0, 0)))
        out_shape.append(jax.ShapeDtypeStruct((n_blk * n_seq, DEPTH, 2, C_HEADS, C_HEADDIM, C_STATE), F32))
    aliases = _alias_out([(0, prev), (1, st_prev)], in_specs, args)
    res = pl.pallas_call(
        functools.partial(_ssd_kernel, seq_len=seq_len, row_len=row_len, chained=chained,
                          aliased=len(aliases)),
        grid=(n_blk, C_GROUPS),
        in_specs=in_specs, out_specs=out_specs, out_shape=out_shape,
        input_output_aliases=aliases,
        scratch_shapes=[pltpu.VMEM((tb, gw), F32), pltpu.VMEM((tb, C_STATE), F32), pltpu.VMEM((tb, C_STATE), F32),
                        pltpu.VMEM((tb, 128), F32), pltpu.VMEM((tb, gw), F32), pltpu.VMEM((tb, gw), F32),
                        pltpu.VMEM((2, hpg, C_HEADDIM, C_STATE), F32)],
        compiler_params=_cp("arbitrary", "arbitrary"),
        name="ssd",
    )(*args)
    return res[0], (None if chained else res[1])


def _route_kernel(lg_ref, bias_ref, e_ref, w_ref):
    lg = lg_ref[...]
    ne = lg.shape[0]
    epg = ne // N_EXPERT_GROUPS
    mx = jnp.max(lg, axis=0, keepdims=True)
    ex = jnp.exp(lg - mx)
    probs = ex / jnp.sum(ex, axis=0, keepdims=True)
    sel = probs + bias_ref[...]
    rows = [sel[e:e + 1] for e in range(ne)]
    top2 = []
    for e in range(ne):
        g0 = (e // epg) * epg
        rank = jnp.zeros_like(rows[e])
        for o in range(g0, g0 + epg):
            if o == e:
                continue
            ahead = (rows[o] > rows[e]) | ((rows[o] == rows[e]) & (o < e))
            rank = rank + jnp.where(ahead, 1.0, 0.0)
        top2.append(rank < 1.5)
    score = []
    for g in range(N_EXPERT_GROUPS):
        sc = jnp.zeros_like(rows[0])
        for e in range(g * epg, (g + 1) * epg):
            sc = sc + jnp.where(top2[e], rows[e], 0.0)
        score.append(sc)
    best = []
    for g in range(N_EXPERT_GROUPS):
        ok = jnp.ones(rows[0].shape, jnp.bool_)
        for o in range(N_EXPERT_GROUPS):
            if o < g:
                ok = ok & (score[g] > score[o])
            elif o > g:
                ok = ok & (score[g] >= score[o])
        best.append(ok)
    first = jnp.full(rows[0].shape, float(ne), F32)
    second = jnp.full(rows[0].shape, -1.0, F32)
    p_first = jnp.zeros_like(rows[0])
    p_second = jnp.zeros_like(rows[0])
    for e in range(ne - 1, -1, -1):
        ch = top2[e] & best[e // epg]
        first = jnp.where(ch, float(e), first)
        p_first = jnp.where(ch, probs[e:e + 1], p_first)
    for e in range(ne):
        ch = top2[e] & best[e // epg]
        second = jnp.where(ch, float(e), second)
        p_second = jnp.where(ch, probs[e:e + 1], p_second)
    tot = p_first + p_second
    e_ref[0:1, :] = first.astype(jnp.int32)
    e_ref[1:2, :] = second.astype(jnp.int32)
    w_ref[0:1, :] = p_first / tot
    w_ref[1:2, :] = p_second / tot


def _route(logits_t, router_bias):
    ne, n = logits_t.shape
    return pl.pallas_call(
        _route_kernel,
        grid=(1,),
        in_specs=[pl.BlockSpec((ne, n), lambda i: (0, 0)), pl.BlockSpec((ne, 1), lambda i: (0, 0))],
        out_specs=[pl.BlockSpec((2, n), lambda i: (0, 0)), pl.BlockSpec((2, n), lambda i: (0, 0))],
        out_shape=[jax.ShapeDtypeStruct((2, n), jnp.int32), jax.ShapeDtypeStruct((2, n), F32)],
        compiler_params=_cp("arbitrary"),
        name="route",
    )(logits_t, router_bias.reshape(ne, 1))


def _dispatch_plan(e2, tm, n_tiles):
    n = e2.shape[1]
    e_flat = e2.reshape(-1)
    onehot = (e_flat[:, None] == jnp.arange(N_EXPERTS, dtype=jnp.int32)[None, :]).astype(jnp.int32)
    counts = jnp.sum(onehot, axis=0)
    rank = jnp.sum(jnp.cumsum(onehot, axis=0) * onehot, axis=1) - 1
    tiles_per = (counts + tm - 1) // tm
    tile_end = jnp.cumsum(tiles_per)
    starts = (tile_end - tiles_per) * tm
    dest = jnp.sum(onehot * starts[None, :], axis=1) + rank
    src = jnp.zeros((n_tiles * tm,), jnp.int32).at[dest].set(jnp.arange(2 * n, dtype=jnp.int32) % n)
    n_used = tile_end[-1]
    t_idx = jnp.minimum(jnp.arange(n_tiles, dtype=jnp.int32), n_used - 1)
    tile_expert = jnp.sum((tile_end[None, :] <= t_idx[:, None]).astype(jnp.int32), axis=1)
    tile_expert = jnp.minimum(tile_expert, N_EXPERTS - 1)
    return dest, src, tile_expert, n_used.reshape(1).astype(jnp.int32)


def _moe_kernel(te_ref, nu_ref, xs_ref, wg_ref, wu_ref, wd_ref, o_ref):
    t, j = pl.program_id(0), pl.program_id(1)

    @pl.when(j == 0)
    def _():
        o_ref[...] = jnp.zeros_like(o_ref)

    @pl.when(t < nu_ref[0])
    def _():
        x = xs_ref[...]
        hg = _dot(x, wg_ref[...].astype(BF16))
        hu = _dot(x, wu_ref[...].astype(BF16))
        act = (_silu(hg) * hu).astype(BF16)
        o_ref[...] += _dot(act, wd_ref[...].astype(BF16))


def _moe(xs, tile_expert, n_used, w_gate, w_up, w_down, layer, tm=MOE_TM, tf=MOE_TF):
    r, d = xs.shape
    n_tiles = r // tm
    nj = D_EXPERT // tf

    def jj(t, j, nu):
        return jnp.where(t < nu[0], j, nj - 1)

    grid_spec = pltpu.PrefetchScalarGridSpec(
        num_scalar_prefetch=2,
        grid=(n_tiles, nj),
        in_specs=[pl.BlockSpec((tm, d), lambda t, j, te, nu: (jnp.minimum(t, nu[0] - 1), 0)),
                  pl.BlockSpec((None, None, d, tf), lambda t, j, te, nu: (layer, te[t], 0, jj(t, j, nu))),
                  pl.BlockSpec((None, None, d, tf), lambda t, j, te, nu: (layer, te[t], 0, jj(t, j, nu))),
                  pl.BlockSpec((None, None, tf, d), lambda t, j, te, nu: (layer, te[t], jj(t, j, nu), 0))],
        out_specs=pl.BlockSpec((tm, d), lambda t, j, te, nu: (t, 0)),
    )
    return pl.pallas_call(
        _moe_kernel,
        grid_spec=grid_spec,
        out_shape=jax.ShapeDtypeStruct((r, d), F32),
        compiler_params=_cp("arbitrary", "arbitrary"),
        name="moe_experts",
    )(tile_expert, n_used, xs, w_gate, w_up, w_down)


def _combine_kernel(dest_ref, x_ref, y_hbm, w_ref, ga_ref, o_ref, ybuf, sem, *, tm, n):
    i = pl.program_id(0)
    nt = pl.num_programs(0)

    def row_copy(tile, slot, s, k):
        r = dest_ref[s * n + tile * tm + k]
        return pltpu.make_async_copy(y_hbm.at[pl.ds(r, 1), :], ybuf.at[slot, s, pl.ds(k, 1), :], sem.at[slot])

    def issue(tile, slot):
        def body(k, c):
            row_copy(tile, slot, 0, k).start()
            row_copy(tile, slot, 1, k).start()
            return c
        lax.fori_loop(0, tm, body, 0, unroll=8)

    def drain(tile, slot):
        def body(k, c):
            row_copy(tile, slot, 0, k).wait()
            row_copy(tile, slot, 1, k).wait()
            return c
        lax.fori_loop(0, tm, body, 0, unroll=8)

    @pl.when(i == 0)
    def _():
        issue(0, 0)

    @pl.when(i + 1 < nt)
    def _():
        issue(i + 1, (i + 1) % 2)

    slot = i % 2
    drain(i, slot)
    w = w_ref[...]
    y = w[:, 0:1] * ybuf[slot, 0] + w[:, 1:2] * ybuf[slot, 1]
    o_ref[...] = x_ref[...] + ga_ref[...] * y


def _combine(x, y, dest, w2, mod_l, ga_chunk, tm=256):
    n, d = x.shape
    grid_spec = pltpu.PrefetchScalarGridSpec(
        num_scalar_prefetch=1,
        grid=(n // tm,),
        in_specs=[pl.BlockSpec((tm, d), lambda i, dst: (i, 0)),
                  pl.BlockSpec(memory_space=pl.ANY),
                  pl.BlockSpec((tm, 2), lambda i, dst: (i, 0)),
                  pl.BlockSpec((None, 1, d), lambda i, dst: (_cond_row(i, tm), 0, ga_chunk))],
        out_specs=pl.BlockSpec((tm, d), lambda i, dst: (i, 0)),
        scratch_shapes=[pltpu.VMEM((2, 2, tm, d), F32), pltpu.SemaphoreType.DMA((2,))],
    )
    return pl.pallas_call(
        functools.partial(_combine_kernel, tm=tm, n=n),
        grid_spec=grid_spec,
        out_shape=jax.ShapeDtypeStruct((n, d), F32),
        compiler_params=_cp("arbitrary"),
        name="combine",
    )(dest, x, y, w2, mod_l)


def _layer(x, l, mod_l, p, lbs, w_router, router_bias, states, new_states):
    n = x.shape[0]
    state_hgrn, state_rglru, state_ssd = states
    h = _normmod(x, p["norm1_g"][l], mod_l, chunk=(0, 1))
    proj = _mm(h, p["w_in"], l, 0, N_MAIN)
    dtp = _mm(h, p["w_dt"], l, 0, 512, tn=512)
    gates = _mm(h, p["w_merge"], l, 0, 3 * D_MODEL, act="sigmoid", out_dtype=BF16)
    nb = n // TOK_BLOCK
    ns = nb - N_CTX_BLOCKS
    hg_prev, ssd_prev = new_states
    oa, hg_new = _hgrn(proj, lbs[l], p["hgrn_norm_g"][l], 0, N_CTX_BLOCKS, SEQ, layer=l, st_prev=hg_prev)
    oa, _ = _hgrn(proj, lbs[l], p["hgrn_norm_g"][l], N_CTX_BLOCKS, ns, TOK_BLOCK, state_hgrn, l, prev=oa)
    ob, lru_new = _lru(proj, p, l, 0, N_CTX_BLOCKS, SEQ)
    ob, _ = _lru(proj, p, l, N_CTX_BLOCKS, ns, GRID_W, state_rglru, prev=ob)
    oc, ssd_new = _ssd(proj, dtp, p, l, 0, N_CTX_BLOCKS, SEQ, SEQ, st_prev=ssd_prev)
    oc, _ = _ssd(proj, dtp, p, l, N_CTX_BLOCKS, ns, TOK_BLOCK, GRID_W, state_ssd, prev=oc)
    merged = _merge(oa, ob, oc, p["w_branch_a"], p["w_branch_b"], p["w_branch_c"], gates, l)
    x = _mm_residual(merged, p["w_out"], l, x, mod_l, 2)
    h2, logits = _normmod(x, p["norm2_g"][l], mod_l, chunk=(3, 4), w_router=w_router)
    e2, w2 = _route(logits.T, router_bias)
    n_tiles = (2 * n) // MOE_TM + N_EXPERTS
    dest, src, tile_expert, n_used = _dispatch_plan(e2, MOE_TM, n_tiles)
    xs = jnp.take(h2, src, axis=0)
    y = _moe(xs, tile_expert, n_used, p["w_e_gate"], p["w_e_up"], p["w_e_down"], l)
    x = _combine(x, y, dest, w2.T, mod_l, 5)
    return x, hg_new, lru_new.transpose(1, 0, 2), ssd_new


def _hgrn_lower_bounds(lb_raw):
    pr = jax.nn.softmax(lb_raw.astype(F32), axis=0)
    cum = jnp.cumsum(pr, axis=0)
    return cum - cum[0]


def kernel(x_prompt, x_sample, state_hgrn, state_rglru, state_ssd, c, c_ctx, w_mod, b_mod, norm1_g, norm2_g, w_in, hgrn_lb, hgrn_norm_g, conv_b_w, conv_b_b, lru_wa, lru_ba, lru_wx, lru_bx, lru_lambda, conv_c_w, conv_c_b, ssd_a_log, ssd_dt_bias, ssd_d, ssd_norm_g, w_branch_a, w_branch_b, w_branch_c, w_out, w_router, router_bias, w_e_gate, w_e_up, w_e_down, final_g):
    bsz, seq, d = x_prompt.shape
    dbsz, dseq, _ = x_sample.shape
    assert seq == SEQ and dseq == TOK_BLOCK and (bsz * seq) == N_CTX_BLOCKS * TOK_BLOCK and d == D_MODEL
    x = jnp.concatenate([x_prompt.reshape(-1, d), x_sample.reshape(-1, d)], axis=0)

    cond8 = jnp.zeros((8, d), F32).at[0].set(c_ctx).at[1:1 + dbsz].set(c)
    mod = _modulation(cond8, w_mod, b_mod).reshape(DEPTH, 8, 1, N_MOD * d)

    hpg = C_HEADS // C_GROUPS
    w_dt_raw = w_in[:, :, COL_DT:COL_DT + 2 * C_HEADS].reshape(DEPTH, d, 2, C_GROUPS, hpg)
    w_dt = jnp.zeros((DEPTH, d, C_GROUPS, 128), F32).at[:, :, :, :2 * hpg].set(
        w_dt_raw.transpose(0, 1, 3, 2, 4).reshape(DEPTH, d, C_GROUPS, 2 * hpg)).reshape(DEPTH, d, C_GROUPS * 128)

    def group_lanes(a):
        g = a.reshape(DEPTH, 2, C_GROUPS, hpg).transpose(0, 2, 1, 3).reshape(DEPTH, C_GROUPS, 1, 2 * hpg)
        return jnp.zeros((DEPTH, C_GROUPS, 1, 128), F32).at[..., :2 * hpg].set(g)

    p = dict(norm1_g=norm1_g, norm2_g=norm2_g, w_in=w_in, w_dt=w_dt,
             w_merge=w_in[:, :, COL_MERGE:], hgrn_norm_g=hgrn_norm_g,
             conv_b_w=conv_b_w, conv_b_b=conv_b_b, lru_wa=lru_wa, lru_ba=lru_ba, lru_wx=lru_wx,
             lru_bx=lru_bx, lru_lambda=lru_lambda, conv_c_w=conv_c_w, conv_c_b=conv_c_b,
             dt_bias_g=group_lanes(ssd_dt_bias), a_log_g=group_lanes(ssd_a_log),
             ssd_d_rep=jnp.repeat(ssd_d, C_HEADDIM, axis=-1).reshape(DEPTH, 1, C_INNER),
             ssd_norm_g=ssd_norm_g, w_branch_a=w_branch_a, w_branch_b=w_branch_b, w_branch_c=w_branch_c,
             w_out=w_out, w_e_gate=w_e_gate, w_e_up=w_e_up, w_e_down=w_e_down)
    lbs = _hgrn_lower_bounds(hgrn_lb)

    hg_new = ssd_new = None
    lru_list = []
    for l in range(DEPTH):
        x, hg_new, lru_new, ssd_new = _layer(x, l, mod[l], p, lbs, w_router, router_bias,
                                             (state_hgrn, state_rglru, state_ssd), (hg_new, ssd_new))
        lru_list.append(lru_new)
    n_ctx = bsz * seq
    y_prompt = _normmod(x, final_g, out_dtype=F32, row0=0, n_rows=n_ctx).reshape(bsz, seq, d)
    y_sample = _normmod(x, final_g, out_dtype=F32, row0=n_ctx, n_rows=dbsz * dseq).reshape(dbsz, dseq, d)
    return (y_prompt, y_sample, hg_new, jnp.stack(lru_list, axis=1), ssd_new)
```

```python
import functools
import math

import numpy as np
import jax
import jax.numpy as jnp
from jax import lax
from jax.experimental import pallas as pl
from jax.experimental.pallas import tpu as pltpu

F32 = jnp.float32
BF16 = jnp.bfloat16
HIGHEST = lax.Precision.HIGHEST

EPS = 1e-6
D_MODEL = 2048
DEPTH = 4
N_MOD = 6
GRID_W = 64
SEQ = 256
TOK_BLOCK = 2048
N_CTX_BLOCKS = 2

A_HEADS = 8
A_DK = 128
A_DV = 128
A_WIDTH = 1024
HGRN_C = 128

B_WIDTH = 1024
B_BLOCK = 128
LRU_C = 8.0
LRU_SEG = 256

C_INNER = 1024
C_HEADDIM = 64
C_HEADS = 16
C_GROUPS = 4
C_STATE = 128
SSD_L = 128
CONV_W = 4

N_EXPERTS = 16
N_EXPERT_GROUPS = 4
D_EXPERT = 1024
MOE_TM = 1024
MOE_TF = 256

COL_Q, COL_FF, COL_FB, COL_V, COL_GA = 0, 1024, 2048, 3072, 4096
COL_XB, COL_GB, COL_Z, COL_XBC, COL_DT, COL_MERGE = 5120, 6144, 7168, 8192, 10240, 10272
N_MAIN = 10240

VMEM_LIMIT_BYTES = 56 * 1024 * 1024
NEG_BIG = -1e30


def _cp(*sem):
    return pltpu.CompilerParams(dimension_semantics=sem, vmem_limit_bytes=VMEM_LIMIT_BYTES)


def _silu(x):
    return x * jax.nn.sigmoid(x)


def _dot(a, b):
    return jnp.dot(a, b, preferred_element_type=F32)


def _dot_nt(a, b):
    return lax.dot_general(a, b, (((1,), (1,)), ((), ())), preferred_element_type=F32)


def _dot_tn(a, b):
    return lax.dot_general(a, b, (((0,), (0,)), ((), ())), preferred_element_type=F32)


def _dot_hi(a, b):
    return jnp.dot(a, b, preferred_element_type=F32, precision=HIGHEST)


def _cond_row(i, tm):
    return jnp.maximum((i * tm) // TOK_BLOCK - (N_CTX_BLOCKS - 1), 0)


def _mod_kernel(c_ref, w_ref, b_ref, o_ref):
    s = _silu(c_ref[...]).astype(BF16)
    o_ref[...] = _dot(s, w_ref[...].astype(BF16)) + b_ref[...]


def _modulation(cond8, w_mod, b_mod):
    depth, d, n = w_mod.shape
    tn = 1024
    return pl.pallas_call(
        _mod_kernel,
        grid=(depth, n // tn),
        in_specs=[pl.BlockSpec((8, d), lambda l, j: (0, 0)),
                  pl.BlockSpec((None, d, tn), lambda l, j: (l, 0, j)),
                  pl.BlockSpec((None, 1, tn), lambda l, j: (l, 0, j))],
        out_specs=pl.BlockSpec((None, 8, tn), lambda l, j: (l, 0, j)),
        out_shape=jax.ShapeDtypeStruct((depth, 8, n), F32),
        compiler_params=_cp("arbitrary", "arbitrary"),
        name="modulation",
    )(cond8, w_mod, b_mod.reshape(depth, 1, n))


def _normmod_kernel(*refs, modulate, router):
    it = iter(refs)
    x_ref, g_ref = next(it), next(it)
    sc_ref = sh_ref = wr_ref = lg_ref = None
    if modulate:
        sc_ref, sh_ref = next(it), next(it)
    if router:
        wr_ref = next(it)
    h_ref = next(it)
    if router:
        lg_ref = next(it)
    x = x_ref[...]
    h = x * lax.rsqrt(jnp.mean(x * x, axis=-1, keepdims=True) + EPS) * g_ref[...]
    if modulate:
        h = h * (1.0 + sc_ref[...]) + sh_ref[...]
    h_ref[...] = h.astype(h_ref.dtype)
    if router:
        lg_ref[...] = _dot_hi(h, wr_ref[...])


def _normmod(x, g, mod_l=None, chunk=None, w_router=None, out_dtype=BF16, tm=512, row0=0, n_rows=None):
    d = x.shape[1]
    n = x.shape[0] if n_rows is None else n_rows
    r0 = row0 // tm
    modulate = mod_l is not None
    router = w_router is not None
    assert not (modulate and row0)
    in_specs = [pl.BlockSpec((tm, d), lambda i: (i + r0, 0)),
                pl.BlockSpec((1, d), lambda i: (0, 0))]
    args = [x, g.reshape(1, d)]
    if modulate:
        sh_c, sc_c = chunk
        in_specs += [pl.BlockSpec((None, 1, d), lambda i: (_cond_row(i, tm), 0, sc_c)),
                     pl.BlockSpec((None, 1, d), lambda i: (_cond_row(i, tm), 0, sh_c))]
        args += [mod_l, mod_l]
    out_specs = [pl.BlockSpec((tm, d), lambda i: (i, 0))]
    out_shape = [jax.ShapeDtypeStruct((n, d), out_dtype)]
    if router:
        ne = w_router.shape[1]
        in_specs.append(pl.BlockSpec((d, ne), lambda i: (0, 0)))
        args.append(w_router)
        out_specs.append(pl.BlockSpec((tm, ne), lambda i: (i, 0)))
        out_shape.append(jax.ShapeDtypeStruct((n, ne), F32))
    res = pl.pallas_call(
        functools.partial(_normmod_kernel, modulate=modulate, router=router),
        grid=(n // tm,),
        in_specs=in_specs, out_specs=out_specs, out_shape=out_shape,
        compiler_params=_cp("arbitrary"),
        name="normmod",
    )(*args)
    return res if router else res[0]


def _mm_kernel(a_ref, w_ref, o_ref, *, act):
    acc = _dot(a_ref[...], w_ref[...].astype(BF16))
    if act == "sigmoid":
        acc = jax.nn.sigmoid(acc)
    o_ref[...] = acc.astype(o_ref.dtype)


def _mm(a, w, layer, col0, n_out, act=None, out_dtype=F32, tm=2048, tn=512):
    m, k = a.shape
    tn = min(tn, n_out)
    off = col0 // tn
    assert col0 % tn == 0 and n_out % tn == 0 and m % tm == 0
    if layer is None:
        w_spec = pl.BlockSpec((k, tn), lambda i, j: (0, j + off))
    else:
        w_spec = pl.BlockSpec((None, k, tn), lambda i, j: (layer, 0, j + off))
    return pl.pallas_call(
        functools.partial(_mm_kernel, act=act),
        grid=(m // tm, n_out // tn),
        in_specs=[pl.BlockSpec((tm, k), lambda i, j: (i, 0)), w_spec],
        out_specs=pl.BlockSpec((tm, tn), lambda i, j: (i, j)),
        out_shape=jax.ShapeDtypeStruct((m, n_out), out_dtype),
        compiler_params=_cp("arbitrary", "arbitrary"),
        name="matmul",
    )(a, w)


def _mm_res_kernel(a_ref, w_ref, x_ref, ga_ref, o_ref):
    acc = _dot(a_ref[...], w_ref[...].astype(BF16))
    o_ref[...] = x_ref[...] + ga_ref[...] * acc


def _mm_residual(a, w, layer, x, mod_l, ga_chunk, tm=1024, tn=512):
    m, k = a.shape
    n = w.shape[-1]
    gs = n // tn
    return pl.pallas_call(
        _mm_res_kernel,
        grid=(m // tm, n // tn),
        in_specs=[pl.BlockSpec((tm, k), lambda i, j: (i, 0)),
                  pl.BlockSpec((None, k, tn), lambda i, j: (layer, 0, j)),
                  pl.BlockSpec((tm, tn), lambda i, j: (i, j)),
                  pl.BlockSpec((None, 1, tn), lambda i, j: (_cond_row(i, tm), 0, ga_chunk * gs + j))],
        out_specs=pl.BlockSpec((tm, tn), lambda i, j: (i, j)),
        out_shape=jax.ShapeDtypeStruct((m, n), F32),
        compiler_params=_cp("arbitrary", "arbitrary"),
        name="out_proj",
    )(a, w, x, mod_l)


def _merge_kernel(oa_ref, ob_ref, oc_ref, wa_ref, wb_ref, wc_ref, g1_ref, g2_ref, g3_ref, o_ref):
    ya = _dot(oa_ref[...], wa_ref[...].astype(BF16))
    yb = _dot(ob_ref[...], wb_ref[...].astype(BF16))
    yc = _dot(oc_ref[...], wc_ref[...].astype(BF16))
    m = (g1_ref[...].astype(F32) * ya + g2_ref[...].astype(F32) * yb + g3_ref[...].astype(F32) * yc)
    o_ref[...] = m.astype(o_ref.dtype)


def _merge(oa, ob, oc, wa, wb, wc, gates, layer, tm=1024, tn=512):
    m, k = oa.shape
    n = wa.shape[-1]
    gs = n // tn
    a_spec = pl.BlockSpec((tm, k), lambda i, j: (i, 0))
    w_spec = pl.BlockSpec((None, k, tn), lambda i, j: (layer, 0, j))
    return pl.pallas_call(
        _merge_kernel,
        grid=(m // tm, n // tn),
        in_specs=[a_spec, a_spec, a_spec, w_spec, w_spec, w_spec,
                  pl.BlockSpec((tm, tn), lambda i, j: (i, j)),
                  pl.BlockSpec((tm, tn), lambda i, j: (i, j + gs)),
                  pl.BlockSpec((tm, tn), lambda i, j: (i, j + 2 * gs))],
        out_specs=pl.BlockSpec((tm, tn), lambda i, j: (i, j)),
        out_shape=jax.ShapeDtypeStruct((m, n), BF16),
        compiler_params=_cp("arbitrary", "arbitrary"),
        name="merge",
    )(oa, ob, oc, wa, wb, wc, gates, gates, gates)


def _dwconv(x, w, b, row_len):
    t = x.shape[0]
    pos = lax.broadcasted_iota(jnp.int32, x.shape, 0) % row_len
    xm2 = jnp.where(pos >= 2, pltpu.roll(x, 2, axis=0), 0.0)
    xm1 = jnp.where(pos >= 1, pltpu.roll(x, 1, axis=0), 0.0)
    xp1 = jnp.where(pos <= row_len - 2, pltpu.roll(x, t - 1, axis=0), 0.0)
    y = b + xm2 * w[0:1]
    y = y + xm1 * w[1:2]
    y = y + x * w[2:3]
    y = y + xp1 * w[3:4]
    return y


def _hgrn_consts(c):
    nl = int(math.log2(c))
    seg = np.zeros((2, nl + 1, c, c), np.float32)
    msk = np.zeros((2, nl + 1, c, c), np.float32)
    up = np.zeros((2, nl, c, 128), np.float32)
    for d in range(2):
        tt = np.arange(c) if d == 0 else c - 1 - np.arange(c)
        tr, tc = tt[:, None], tt[None, :]
        seg[d, 0] = tc <= tr
        for l in range(nl):
            s = 1 << l
            blk, upper = tt // (2 * s), (tt % (2 * s)) >= s
            mid = (blk * 2 * s + s)[:, None]
            seg_u = (tc >= mid) & (tc <= tr)
            seg_l = (tc >= tr + 1) & (tc <= mid - 1)
            seg[d, l + 1] = np.where(upper[:, None], seg_u, seg_l)
            msk[d, l] = upper[:, None] & (~upper)[None, :] & (blk[:, None] == blk[None, :])
            up[d, l] = upper[:, None]
        msk[d, nl] = np.eye(c)
    return seg.reshape(2, (nl + 1) * c, c), msk, up


def _hgrn_kernel(*refs, seq_len, chained, aliased):
    it = iter(refs)
    q_ref, ff_ref, fb_ref, v_ref, ga_ref, lb_ref, ng_ref = (next(it) for _ in range(7))
    seg_ref, msk_ref, up_ref = next(it), next(it), next(it)
    s0_ref = next(it) if chained else None
    for _ in range(aliased):
        next(it)
    o_ref = next(it)
    sout_ref = None if chained else next(it)
    o_scr, qp_scr, u_scr, dec_scr, snap_scr = (next(it) for _ in range(5))

    c = HGRN_C
    nl = int(math.log2(c))
    tb = q_ref.shape[0]
    n_chunks = tb // c
    n_ch = seq_len // c
    lb = lb_ref[...]
    f_refs = (ff_ref, fb_ref)

    def phase_a(n, carry):
        rows = pl.ds(pl.multiple_of(n * c, c), c)
        q = _silu(q_ref[rows, :]) * (A_DK ** -0.5)
        qb = q.astype(BF16)
        v = v_ref[rows, :].astype(BF16)
        o = jnp.zeros((c, A_DV), F32)
        qps, kps, decs = [], [], []
        for d in range(2):
            f = lb + (1.0 - lb) * jax.nn.sigmoid(f_refs[d][rows, :])
            k = 1.0 - f
            lf = jnp.log(f)
            hi = lf.astype(BF16)
            mid = (lf - hi.astype(F32)).astype(BF16)
            e2 = _dot(seg_ref[d], jnp.concatenate([hi, mid], axis=1))
            e = e2[:, :A_DK] + e2[:, A_DK:]
            b = e[0:c]
            att = msk_ref[d, nl] * _dot_nt(qb, k.astype(BF16))
            for l in range(nl):
                x = (jnp.where(up_ref[d, l] > 0.5, q, k) * jnp.exp(e[(l + 1) * c:(l + 2) * c])).astype(BF16)
                att = att + msk_ref[d, l] * _dot_nt(x, x)
            o = o + _dot(att.astype(BF16), v)
            b_end = b[c - 1:c] if d == 0 else b[0:1]
            qps.append((q * jnp.exp(b)).astype(BF16))
            kps.append((k * jnp.exp(b_end - b)).astype(BF16))
            decs.append(jnp.exp(b_end))
        o_scr[rows, :] = o
        qp_scr[rows, :] = jnp.concatenate(qps, axis=1)
        u_scr[n] = _dot_tn(v, jnp.concatenate(kps, axis=1))
        dec_scr[n] = jnp.concatenate(decs, axis=1)
        return carry

    lax.fori_loop(0, n_chunks, phase_a, 0)

    def init_state(d):
        return s0_ref[d].T if chained else jnp.zeros((A_DV, A_DK), F32)

    def phase_b1(j, carry):
        sf, sb = carry
        nf = j
        nb = n_chunks - 1 - j
        if not chained:
            sf = jnp.where(nf % n_ch == 0, 0.0, sf)
            sb = jnp.where(nb % n_ch == n_ch - 1, 0.0, sb)
        snap_scr[nf, :, 0:A_DK] = sf.astype(BF16)
        snap_scr[nb, :, A_DK:2 * A_DK] = sb.astype(BF16)
        sf = sf * dec_scr[nf][:, 0:A_DK] + u_scr[nf][:, 0:A_DK]
        sb = sb * dec_scr[nb][:, A_DK:2 * A_DK] + u_scr[nb][:, A_DK:2 * A_DK]
        if not chained:
            @pl.when(nf % n_ch == n_ch - 1)
            def _():
                sout_ref[nf // n_ch, 0] = sf.T

            @pl.when(nb % n_ch == 0)
            def _():
                sout_ref[nb // n_ch, 1] = sb.T
        return sf, sb

    lax.fori_loop(0, n_chunks, phase_b1, (init_state(0), init_state(1)))

    ng = ng_ref[...]

    def phase_b2(n, carry):
        rows = pl.ds(pl.multiple_of(n * c, c), c)
        o = o_scr[rows, :] + _dot_nt(qp_scr[rows, :], snap_scr[n])
        o = o * lax.rsqrt(jnp.mean(o * o, axis=-1, keepdims=True) + EPS) * ng
        o_ref[rows, :] = (o * _silu(ga_ref[rows, :])).astype(o_ref.dtype)
        return carry

    lax.fori_loop(0, n_chunks, phase_b2, 0, unroll=4)


def _alias_out(prevs, in_specs, args):
    aliases = {}
    for out_idx, prev in prevs:
        if prev is not None:
            in_specs.append(pl.BlockSpec(memory_space=pl.ANY))
            args.append(prev)
            aliases[len(args) - 1] = out_idx
    return aliases


def _hgrn(proj, lb_l, ng_l, blk0, n_blk, seq_len, state=None, layer=0, prev=None, st_prev=None):
    tb = TOK_BLOCK
    chained = state is not None
    seg, msk, up = _hgrn_consts(HGRN_C)
    cb = lambda col: col // A_DK

    def col_spec(col):
        return pl.BlockSpec((tb, A_DK), lambda b, h: (b + blk0, cb(col) + h))

    def full(a):
        nd = a.ndim
        return pl.BlockSpec(a.shape, lambda b, h: (0,) * nd)

    in_specs = [col_spec(COL_Q), col_spec(COL_FF), col_spec(COL_FB), col_spec(COL_V), col_spec(COL_GA),
                pl.BlockSpec((1, A_DK), lambda b, h: (0, h)),
                pl.BlockSpec((1, A_DV), lambda b, h: (0, 0)),
                full(seg), full(msk), full(up)]
    args = [proj, proj, proj, proj, proj, lb_l.reshape(1, -1), ng_l.reshape(1, -1),
            jnp.asarray(seg, BF16), jnp.asarray(msk), jnp.asarray(up)]
    out_specs = [pl.BlockSpec((tb, A_DV), lambda b, h: (b + blk0, h))]
    out_shape = [jax.ShapeDtypeStruct((proj.shape[0], A_WIDTH), BF16)]
    if chained:
        in_specs.append(pl.BlockSpec((None, None, 2, None, A_DK, A_DV), lambda b, h: (b, layer, 0, h, 0, 0)))
        args.append(state)
    else:
        n_seq = tb // seq_len
        out_specs.append(pl.BlockSpec((n_seq, None, 2, None, A_DK, A_DV), lambda b, h: (b, layer, 0, h, 0, 0)))
        out_shape.append(jax.ShapeDtypeStruct((n_blk * n_seq, DEPTH, 2, A_HEADS, A_DK, A_DV), F32))
    aliases = _alias_out([(0, prev), (1, st_prev)], in_specs, args)
    n_chunks = tb // HGRN_C
    res = pl.pallas_call(
        functools.partial(_hgrn_kernel, seq_len=seq_len, chained=chained, aliased=len(aliases)),
        grid=(n_blk, A_HEADS),
        in_specs=in_specs, out_specs=out_specs, out_shape=out_shape,
        input_output_aliases=aliases,
        scratch_shapes=[pltpu.VMEM((tb, A_DV), F32), pltpu.VMEM((tb, 2 * A_DK), BF16),
                        pltpu.VMEM((n_chunks, A_DV, 2 * A_DK), F32), pltpu.VMEM((n_chunks, 1, 2 * A_DK), F32),
                        pltpu.VMEM((n_chunks, A_DV, 2 * A_DK), BF16)],
        compiler_params=_cp("arbitrary", "arbitrary"),
        name="hgrn2",
    )(*args)
    return res[0], (None if chained else res[1])


def _lru_kernel(*refs, row_len, chained, aliased):
    it = iter(refs)
    x_ref, gb_ref, cw_ref, cb_ref, wa_ref, ba_ref, wx_ref, bx_ref, lam_ref = (next(it) for _ in range(9))
    h0_ref = next(it) if chained else None
    for _ in range(aliased):
        next(it)
    o_ref = next(it)
    hout_ref = None if chained else next(it)
    a_scr, u_scr, h_scr, p_scr = (next(it) for _ in range(4))

    tb = x_ref.shape[0]
    seg = LRU_SEG
    n_seg = tb // seg
    xc = _dwconv(x_ref[...], cw_ref[...], cb_ref[...], row_len)
    xcb = xc.astype(BF16)
    for d in range(2):
        r = jax.nn.sigmoid(_dot(xcb, wa_ref[d].astype(BF16)) + ba_ref[d])
        g = jax.nn.sigmoid(_dot(xcb, wx_ref[d].astype(BF16)) + bx_ref[d])
        log_a = LRU_C * r * jax.nn.log_sigmoid(lam_ref[d])
        a = jnp.exp(log_a)
        a_scr[d] = a
        u_scr[d] = jnp.sqrt(1.0 - a * a) * (g * xc)

    def step(i, carry):
        hf, pf, hb, pb = carry
        tf = i
        tr = seg - 1 - i
        af = a_scr[0, pl.ds(tf, n_seg, stride=seg), :]
        hf = af * hf + u_scr[0, pl.ds(tf, n_seg, stride=seg), :]
        h_scr[0, pl.ds(tf, n_seg, stride=seg), :] = hf
        ab = a_scr[1, pl.ds(tr, n_seg, stride=seg), :]
        hb = ab * hb + u_scr[1, pl.ds(tr, n_seg, stride=seg), :]
        h_scr[1, pl.ds(tr, n_seg, stride=seg), :] = hb
        if chained:
            pf = af * pf
            pb = ab * pb
            p_scr[0, pl.ds(tf, n_seg, stride=seg), :] = pf
            p_scr[1, pl.ds(tr, n_seg, stride=seg), :] = pb
        return hf, pf, hb, pb

    zeros = jnp.zeros((n_seg, B_BLOCK), F32)
    ones = jnp.ones((n_seg, B_BLOCK), F32)
    hf, pf, hb, pb = lax.fori_loop(0, seg, step, (zeros, ones, zeros, ones), unroll=8)

    if chained:
        hin = h0_ref[0:1, :]
        for s in range(n_seg):
            rows = pl.ds(s * seg, seg)
            h_scr[0, rows, :] = h_scr[0, rows, :] + p_scr[0, rows, :] * hin
            hin = hf[s:s + 1] + pf[s:s + 1] * hin
        hin = h0_ref[1:2, :]
        for s in range(n_seg - 1, -1, -1):
            rows = pl.ds(s * seg, seg)
            h_scr[1, rows, :] = h_scr[1, rows, :] + p_scr[1, rows, :] * hin
            hin = hb[s:s + 1] + pb[s:s + 1] * hin
    else:
        hout_ref[0] = hf
        hout_ref[1] = hb
    o_ref[...] = ((h_scr[0] + h_scr[1]) * jax.nn.gelu(gb_ref[...])).astype(o_ref.dtype)


def _lru(proj, p, layer, blk0, n_blk, row_len, state=None, prev=None):
    tb = TOK_BLOCK
    chained = state is not None
    nb = B_WIDTH // B_BLOCK
    cbx, cbg = COL_XB // B_BLOCK, COL_GB // B_BLOCK
    in_specs = [pl.BlockSpec((tb, B_BLOCK), lambda b, n: (b + blk0, cbx + n)),
                pl.BlockSpec((tb, B_BLOCK), lambda b, n: (b + blk0, cbg + n)),
                pl.BlockSpec((None, CONV_W, B_BLOCK), lambda b, n: (layer, 0, n)),
                pl.BlockSpec((None, 1, B_BLOCK), lambda b, n: (layer, 0, n)),
                pl.BlockSpec((None, 2, None, B_BLOCK, B_BLOCK), lambda b, n: (layer, 0, n, 0, 0)),
                pl.BlockSpec((None, 2, 1, B_BLOCK), lambda b, n: (layer, 0, 0, n)),
                pl.BlockSpec((None, 2, None, B_BLOCK, B_BLOCK), lambda b, n: (layer, 0, n, 0, 0)),
                pl.BlockSpec((None, 2, 1, B_BLOCK), lambda b, n: (layer, 0, 0, n)),
                pl.BlockSpec((None, 2, 1, B_BLOCK), lambda b, n: (layer, 0, 0, n))]
    d4 = lambda a: a.reshape(DEPTH, 2, 1, B_WIDTH)
    args = [proj, proj, p["conv_b_w"], p["conv_b_b"].reshape(DEPTH, 1, B_WIDTH),
            p["lru_wa"], d4(p["lru_ba"]), p["lru_wx"], d4(p["lru_bx"]), d4(p["lru_lambda"])]
    out_specs = [pl.BlockSpec((tb, B_BLOCK), lambda b, n: (b + blk0, n))]
    out_shape = [jax.ShapeDtypeStruct((proj.shape[0], B_WIDTH), BF16)]
    if chained:
        in_specs.append(pl.BlockSpec((None, None, 2, B_BLOCK), lambda b, n: (b, layer, 0, n)))
        args.append(state)
    else:
        n_seq = tb // LRU_SEG
        out_specs.append(pl.BlockSpec((2, n_seq, B_BLOCK), lambda b, n: (0, b, n)))
        out_shape.append(jax.ShapeDtypeStruct((2, n_blk * n_seq, B_WIDTH), F32))
    aliases = _alias_out([(0, prev)], in_specs, args)
    res = pl.pallas_call(
        functools.partial(_lru_kernel, row_len=row_len, chained=chained, aliased=len(aliases)),
        grid=(n_blk, nb),
        in_specs=in_specs, out_specs=out_specs, out_shape=out_shape,
        input_output_aliases=aliases,
        scratch_shapes=[pltpu.VMEM((2, tb, B_BLOCK), F32) for _ in range(4)],
        compiler_params=_cp("arbitrary", "arbitrary"),
        name="rglru",
    )(*args)
    return res[0], (None if chained else res[1])


def _ssd_kernel(*refs, seq_len, row_len, chained, aliased):
    it = iter(refs)
    (xs_ref, bm_ref, cm_ref, z_ref, dt_ref, cwx_ref, cwb_ref, cwc_ref, cbx_ref, cbb_ref, cbc_ref,
     dtb_ref, a_ref, dsk_ref, ng_ref, tri_ref, ecat_ref, hmask_ref) = (next(it) for _ in range(18))
    s0_ref = next(it) if chained else None
    for _ in range(aliased):
        next(it)
    o_ref = next(it)
    sout_ref = None if chained else next(it)
    (xs_scr, bm_scr, cm_scr, dt_scr, y_scr, dcy_scr, u_scr, dec_scr, snap_scr,
     st_scr) = (next(it) for _ in range(10))

    lc = SSD_L
    hpg = C_HEADS // C_GROUPS
    gw = hpg * C_HEADDIM
    tb = xs_ref.shape[0]
    n_chunks = tb // lc
    n_ch = seq_len // lc
    xs_scr[...] = _silu(_dwconv(xs_ref[...], cwx_ref[...], cbx_ref[...], row_len))
    bm_scr[...] = _silu(_dwconv(bm_ref[...], cwb_ref[...], cbb_ref[...], row_len)).astype(BF16)
    cm_scr[...] = _silu(_dwconv(cm_ref[...], cwc_ref[...], cbc_ref[...], row_len)).astype(BF16)
    dt_scr[...] = jax.nn.softplus(dt_ref[...] + dtb_ref[...])
    a_row = -jnp.exp(a_ref[...])
    rr = lax.broadcasted_iota(jnp.int32, (lc, lc), 0)
    cc = lax.broadcasted_iota(jnp.int32, (lc, lc), 1)
    causal = (rr >= cc, cc >= rr)
    rep0 = hpg * lc

    def split2(x):
        hi = x.astype(BF16)
        return hi, (x - hi.astype(F32)).astype(BF16)

    def phase_a(n, carry):
        rows = pl.ds(pl.multiple_of(n * lc, lc), lc)
        dtc = dt_scr[rows, :]
        hi, mid = split2(dtc * a_row)
        c2 = _dot(tri_ref[...], jnp.concatenate([hi, mid], axis=1))
        cum2 = c2[:, :128] + c2[:, 128:]
        dhi, dmid = split2(dtc)
        xs = xs_scr[rows, :]
        bm = bm_scr[rows, :]
        cm = cm_scr[rows, :]
        scores = _dot_nt(cm, bm)
        y = dsk_ref[...] * xs
        for d in range(2):
            cum = cum2[d * lc:(d + 1) * lc]
            chi, cmid = split2(cum)
            r4 = _dot(jnp.concatenate([chi, cmid, dhi, dmid], axis=0), ecat_ref[d])
            rep = r4[0:lc] + r4[lc:2 * lc]
            rep64 = rep[:, rep0:]
            dtrep = r4[2 * lc:3 * lc, rep0:] + r4[3 * lc:, rep0:]
            cum_t = cum.T
            ps = []
            for hh in range(hpg):
                ln = d * hpg + hh
                seg = jnp.exp(jnp.where(causal[d], rep[:, lc * hh:lc * (hh + 1)] - cum_t[ln:ln + 1, :], NEG_BIG))
                ps.append((scores * seg).astype(BF16))
            xdt = xs * dtrep
            rhs = jnp.concatenate([(xdt * hmask_ref[hh]).astype(BF16) for hh in range(hpg)], axis=0)
            y = y + _dot(jnp.concatenate(ps, axis=1), rhs)
            end = rep64[lc - 1:lc] if d == 0 else rep64[0:1]
            u_scr[d, n] = _dot_tn(bm, (xdt * jnp.exp(end - rep64)).astype(BF16))
            dec_scr[d, n] = jnp.exp(end)
            dcy_scr[d, rows, :] = jnp.exp(rep64)
        y_scr[rows, :] = y
        return carry

    lax.fori_loop(0, n_chunks, phase_a, 0)

    for d in range(2):
        if chained:
            st_scr[d] = jnp.concatenate([s0_ref[d, hh] for hh in range(hpg)], axis=0).T
        else:
            st_scr[d] = jnp.zeros((C_STATE, gw), F32)

    def phase_b1(j, carry):
        nf = j
        nb = n_chunks - 1 - j
        sf = st_scr[0]
        sb = st_scr[1]
        if not chained:
            sf = jnp.where(nf % n_ch == 0, 0.0, sf)
            sb = jnp.where(nb % n_ch == n_ch - 1, 0.0, sb)
        snap_scr[0, nf] = sf.astype(BF16)
        snap_scr[1, nb] = sb.astype(BF16)
        sf = sf * dec_scr[0, nf] + u_scr[0, nf]
        sb = sb * dec_scr[1, nb] + u_scr[1, nb]
        st_scr[0] = sf
        st_scr[1] = sb
        if not chained:
            @pl.when(nf % n_ch == n_ch - 1)
            def _():
                sft = sf.T
                for hh in range(hpg):
                    sout_ref[nf // n_ch, 0, hh] = sft[hh * C_HEADDIM:(hh + 1) * C_HEADDIM]

            @pl.when(nb % n_ch == 0)
            def _():
                sbt = sb.T
                for hh in range(hpg):
                    sout_ref[nb // n_ch, 1, hh] = sbt[hh * C_HEADDIM:(hh + 1) * C_HEADDIM]
        return carry

    lax.fori_loop(0, n_chunks, phase_b1, 0)

    ng = ng_ref[...]

    def phase_b2(n, carry):
        rows = pl.ds(pl.multiple_of(n * lc, lc), lc)
        cm = cm_scr[rows, :]
        y = y_scr[rows, :] + _dot(cm, snap_scr[0, n]) * dcy_scr[0, rows, :]
        y = y + _dot(cm, snap_scr[1, n]) * dcy_scr[1, rows, :]
        y = y * _silu(z_ref[rows, :])
        y = y * lax.rsqrt(jnp.mean(y * y, axis=-1, keepdims=True) + EPS) * ng
        o_ref[rows, :] = y.astype(o_ref.dtype)
        return carry

    lax.fori_loop(0, n_chunks, phase_b2, 0, unroll=2)


def _ssd(proj, dtp, p, layer, blk0, n_blk, seq_len, row_len, state=None, prev=None, st_prev=None):
    tb = TOK_BLOCK
    chained = state is not None
    hpg = C_HEADS // C_GROUPS
    gw = hpg * C_HEADDIM
    col_x, col_b, col_c = COL_XBC, COL_XBC + C_INNER, COL_XBC + C_INNER + C_GROUPS * C_STATE
    lc = SSD_L
    tri = np.concatenate([np.tril(np.ones((lc, lc), np.float32)), np.triu(np.ones((lc, lc), np.float32))])
    ecat = np.zeros((2, 128, hpg * lc + gw), np.float32)
    hmask = np.zeros((hpg, 1, gw), np.float32)
    for hh in range(hpg):
        hmask[hh, 0, hh * C_HEADDIM:(hh + 1) * C_HEADDIM] = 1.0
        for d in range(2):
            ecat[d, d * hpg + hh, hh * lc:(hh + 1) * lc] = 1.0
            ecat[d, d * hpg + hh, hpg * lc + hh * C_HEADDIM:hpg * lc + (hh + 1) * C_HEADDIM] = 1.0
    in_specs = [pl.BlockSpec((tb, gw), lambda b, g: (b + blk0, col_x // gw + g)),
                pl.BlockSpec((tb, C_STATE), lambda b, g: (b + blk0, col_b // C_STATE + g)),
                pl.BlockSpec((tb, C_STATE), lambda b, g: (b + blk0, col_c // C_STATE + g)),
                pl.BlockSpec((tb, gw), lambda b, g: (b + blk0, COL_Z // gw + g)),
                pl.BlockSpec((tb, 128), lambda b, g: (b + blk0, g)),
                pl.BlockSpec((None, CONV_W, gw), lambda b, g: (layer, 0, g)),
                pl.BlockSpec((None, CONV_W, C_STATE), lambda b, g: (layer, 0, C_INNER // C_STATE + g)),
                pl.BlockSpec((None, CONV_W, C_STATE), lambda b, g: (layer, 0, C_INNER // C_STATE + C_GROUPS + g)),
                pl.BlockSpec((None, 1, gw), lambda b, g: (layer, 0, g)),
                pl.BlockSpec((None, 1, C_STATE), lambda b, g: (layer, 0, C_INNER // C_STATE + g)),
                pl.BlockSpec((None, 1, C_STATE), lambda b, g: (layer, 0, C_INNER // C_STATE + C_GROUPS + g)),
                pl.BlockSpec((None, None, 1, 128), lambda b, g: (layer, g, 0, 0)),
                pl.BlockSpec((None, None, 1, 128), lambda b, g: (layer, g, 0, 0)),
                pl.BlockSpec((None, 1, gw), lambda b, g: (layer, 0, g)),
                pl.BlockSpec((None, 1, gw), lambda b, g: (layer, 0, g)),
                pl.BlockSpec(tri.shape, lambda b, g: (0, 0)),
                pl.BlockSpec(ecat.shape, lambda b, g: (0, 0, 0)),
                pl.BlockSpec(hmask.shape, lambda b, g: (0, 0, 0))]
    cw = p["conv_c_w"]
    cbias = p["conv_c_b"].reshape(DEPTH, 1, -1)
    args = [proj, proj, proj, proj, dtp, cw, cw, cw, cbias, cbias, cbias,
            p["dt_bias_g"], p["a_log_g"], p["ssd_d_rep"], p["ssd_norm_g"].reshape(DEPTH, 1, C_INNER),
            jnp.asarray(tri, BF16), jnp.asarray(ecat, BF16), jnp.asarray(hmask)]
    out_specs = [pl.BlockSpec((tb, gw), lambda b, g: (b + blk0, g))]
    out_shape = [jax.ShapeDtypeStruct((proj.shape[0], C_INNER), BF16)]
    if chained:
        in_specs.append(pl.BlockSpec((None, None, 2, hpg, C_HEADDIM, C_STATE),
                                     lambda b, g: (b, layer, 0, g, 0, 0)))
        args.append(state)
    else:
        n_seq = tb // seq_len
        out_specs.append(pl.BlockSpec((n_seq, None, 2, hpg, C_HEADDIM, C_STATE),
                                      lambda b, g: (b, layer, 0, g, 0, 0)))
        out_shape.append(jax.ShapeDtypeStruct((n_blk * n_seq, DEPTH, 2, C_HEADS, C_HEADDIM, C_STATE), F32))
    aliases = _alias_out([(0, prev), (1, st_prev)], in_specs, args)
    res = pl.pallas_call(
        functools.partial(_ssd_kernel, seq_len=seq_len, row_len=row_len, chained=chained,
                          aliased=len(aliases)),
        grid=(n_blk, C_GROUPS),
        in_specs=in_specs, out_specs=out_specs, out_shape=out_shape,
        input_output_aliases=aliases,
        scratch_shapes=[pltpu.VMEM((tb, gw), F32), pltpu.VMEM((tb, C_STATE), BF16), pltpu.VMEM((tb, C_STATE), BF16),
                        pltpu.VMEM((tb, 128), F32), pltpu.VMEM((tb, gw), F32), pltpu.VMEM((2, tb, gw), F32),
                        pltpu.VMEM((2, tb // lc, C_STATE, gw), F32), pltpu.VMEM((2, tb // lc, 1, gw), F32),
                        pltpu.VMEM((2, tb // lc, C_STATE, gw), BF16), pltpu.VMEM((2, C_STATE, gw), F32)],
        compiler_params=_cp("arbitrary", "arbitrary"),
        name="ssd",
    )(*args)
    return res[0], (None if chained else res[1])


def _route_kernel(lg_ref, bias_ref, e_ref, w_ref):
    lg = lg_ref[...]
    ne = lg.shape[0]
    epg = ne // N_EXPERT_GROUPS
    mx = jnp.max(lg, axis=0, keepdims=True)
    ex = jnp.exp(lg - mx)
    probs = ex / jnp.sum(ex, axis=0, keepdims=True)
    sel = probs + bias_ref[...]
    rows = [sel[e:e + 1] for e in range(ne)]
    top2 = []
    for e in range(ne):
        g0 = (e // epg) * epg
        rank = jnp.zeros_like(rows[e])
        for o in range(g0, g0 + epg):
            if o == e:
                continue
            ahead = (rows[o] > rows[e]) | ((rows[o] == rows[e]) & (o < e))
            rank = rank + jnp.where(ahead, 1.0, 0.0)
        top2.append(rank < 1.5)
    score = []
    for g in range(N_EXPERT_GROUPS):
        sc = jnp.zeros_like(rows[0])
        for e in range(g * epg, (g + 1) * epg):
            sc = sc + jnp.where(top2[e], rows[e], 0.0)
        score.append(sc)
    best = []
    for g in range(N_EXPERT_GROUPS):
        ok = jnp.ones(rows[0].shape, jnp.bool_)
        for o in range(N_EXPERT_GROUPS):
            if o < g:
                ok = ok & (score[g] > score[o])
            elif o > g:
                ok = ok & (score[g] >= score[o])
        best.append(ok)
    first = jnp.full(rows[0].shape, float(ne), F32)
    second = jnp.full(rows[0].shape, -1.0, F32)
    p_first = jnp.zeros_like(rows[0])
    p_second = jnp.zeros_like(rows[0])
    for e in range(ne - 1, -1, -1):
        ch = top2[e] & best[e // epg]
        first = jnp.where(ch, float(e), first)
        p_first = jnp.where(ch, probs[e:e + 1], p_first)
    for e in range(ne):
        ch = top2[e] & best[e // epg]
        second = jnp.where(ch, float(e), second)
        p_second = jnp.where(ch, probs[e:e + 1], p_second)
    tot = p_first + p_second
    e_ref[0:1, :] = first.astype(jnp.int32)
    e_ref[1:2, :] = second.astype(jnp.int32)
    w_ref[0:1, :] = p_first / tot
    w_ref[1:2, :] = p_second / tot


def _route(logits_t, router_bias):
    ne, n = logits_t.shape
    return pl.pallas_call(
        _route_kernel,
        grid=(1,),
        in_specs=[pl.BlockSpec((ne, n), lambda i: (0, 0)), pl.BlockSpec((ne, 1), lambda i: (0, 0))],
        out_specs=[pl.BlockSpec((2, n), lambda i: (0, 0)), pl.BlockSpec((2, n), lambda i: (0, 0))],
        out_shape=[jax.ShapeDtypeStruct((2, n), jnp.int32), jax.ShapeDtypeStruct((2, n), F32)],
        compiler_params=_cp("arbitrary"),
        name="route",
    )(logits_t, router_bias.reshape(ne, 1))


def _dispatch_plan(e2, tm, n_tiles):
    n = e2.shape[1]
    e_flat = e2.reshape(-1)
    onehot = (e_flat[:, None] == jnp.arange(N_EXPERTS, dtype=jnp.int32)[None, :]).astype(jnp.int32)
    counts = jnp.sum(onehot, axis=0)
    rank = jnp.sum(jnp.cumsum(onehot, axis=0) * onehot, axis=1) - 1
    tiles_per = (counts + tm - 1) // tm
    tile_end = jnp.cumsum(tiles_per)
    starts = (tile_end - tiles_per) * tm
    dest = jnp.sum(onehot * starts[None, :], axis=1) + rank
    src = jnp.zeros((n_tiles * tm,), jnp.int32).at[dest].set(jnp.arange(2 * n, dtype=jnp.int32) % n)
    n_used = tile_end[-1]
    t_idx = jnp.minimum(jnp.arange(n_tiles, dtype=jnp.int32), n_used - 1)
    tile_expert = jnp.sum((tile_end[None, :] <= t_idx[:, None]).astype(jnp.int32), axis=1)
    tile_expert = jnp.minimum(tile_expert, N_EXPERTS - 1)
    return dest, src, tile_expert, n_used.reshape(1).astype(jnp.int32)


def _moe_kernel(te_ref, nu_ref, src_ref, h_hbm, wg_ref, wu_ref, wd_ref, o_ref, xbuf, xb_scr, sem, *, tm):
    t, j = pl.program_id(0), pl.program_id(1)
    n_used = nu_ref[0]

    def row_copy(tile, slot, k):
        r = src_ref[tile * tm + k]
        return pltpu.make_async_copy(h_hbm.at[pl.ds(r, 1), :], xbuf.at[slot, pl.ds(k, 1), :], sem.at[slot])

    def issue(tile, slot):
        def body(k, c):
            row_copy(tile, slot, k).start()
            return c
        lax.fori_loop(0, tm, body, 0, unroll=8)

    def drain(tile, slot):
        def body(k, c):
            row_copy(tile, slot, k).wait()
            return c
        lax.fori_loop(0, tm, body, 0, unroll=8)

    @pl.when((j == 0) & (t == 0))
    def _():
        issue(0, 0)

    @pl.when((j == 0) & (t + 1 < n_used))
    def _():
        issue(t + 1, (t + 1) % 2)

    @pl.when((j == 0) & (t < n_used))
    def _():
        drain(t, t % 2)
        xb_scr[...] = xbuf[t % 2].astype(BF16)

    @pl.when(j == 0)
    def _():
        o_ref[...] = jnp.zeros_like(o_ref)

    @pl.when(t < n_used)
    def _():
        x = xb_scr[...]
        hg = _dot(x, wg_ref[...].astype(BF16))
        hu = _dot(x, wu_ref[...].astype(BF16))
        act = (_silu(hg) * hu).astype(BF16)
        o_ref[...] += _dot(act, wd_ref[...].astype(BF16))


def _moe(h, src, tile_expert, n_used, w_gate, w_up, w_down, layer, tm=MOE_TM, tf=MOE_TF):
    d = h.shape[1]
    r = src.shape[0]
    n_tiles = r // tm
    nj = D_EXPERT // tf

    def jj(t, j, nu):
        return jnp.where(t < nu[0], j, nj - 1)

    grid_spec = pltpu.PrefetchScalarGridSpec(
        num_scalar_prefetch=3,
        grid=(n_tiles, nj),
        in_specs=[pl.BlockSpec(memory_space=pl.ANY),
                  pl.BlockSpec((None, None, d, tf), lambda t, j, te, nu, sr: (layer, te[t], 0, jj(t, j, nu))),
                  pl.BlockSpec((None, None, d, tf), lambda t, j, te, nu, sr: (layer, te[t], 0, jj(t, j, nu))),
                  pl.BlockSpec((None, None, tf, d), lambda t, j, te, nu, sr: (layer, te[t], jj(t, j, nu), 0))],
        out_specs=pl.BlockSpec((tm, d), lambda t, j, te, nu, sr: (t, 0)),
        scratch_shapes=[pltpu.VMEM((2, tm, d), F32), pltpu.VMEM((tm, d), BF16), pltpu.SemaphoreType.DMA((2,))],
    )
    return pl.pallas_call(
        functools.partial(_moe_kernel, tm=tm),
        grid_spec=grid_spec,
        out_shape=jax.ShapeDtypeStruct((r, d), F32),
        compiler_params=_cp("arbitrary", "arbitrary"),
        name="moe_experts",
    )(tile_expert, n_used, src, h, w_gate, w_up, w_down)


def _combine_kernel(dest_ref, x_ref, y_hbm, w_ref, ga_ref, o_ref, ybuf, sem, *, tm, n):
    i = pl.program_id(0)
    nt = pl.num_programs(0)

    def row_copy(tile, slot, s, k):
        r = dest_ref[s * n + tile * tm + k]
        return pltpu.make_async_copy(y_hbm.at[pl.ds(r, 1), :], ybuf.at[slot, s, pl.ds(k, 1), :], sem.at[slot])

    def issue(tile, slot):
        def body(k, c):
            row_copy(tile, slot, 0, k).start()
            row_copy(tile, slot, 1, k).start()
            return c
        lax.fori_loop(0, tm, body, 0, unroll=8)

    def drain(tile, slot):
        def body(k, c):
            row_copy(tile, slot, 0, k).wait()
            row_copy(tile, slot, 1, k).wait()
            return c
        lax.fori_loop(0, tm, body, 0, unroll=8)

    @pl.when(i == 0)
    def _():
        issue(0, 0)

    @pl.when(i + 1 < nt)
    def _():
        issue(i + 1, (i + 1) % 2)

    slot = i % 2
    drain(i, slot)
    w = w_ref[...]
    y = w[:, 0:1] * ybuf[slot, 0] + w[:, 1:2] * ybuf[slot, 1]
    o_ref[...] = x_ref[...] + ga_ref[...] * y


def _combine(x, y, dest, w2, mod_l, ga_chunk, tm=256):
    n, d = x.shape
    grid_spec = pltpu.PrefetchScalarGridSpec(
        num_scalar_prefetch=1,
        grid=(n // tm,),
        in_specs=[pl.BlockSpec((tm, d), lambda i, dst: (i, 0)),
                  pl.BlockSpec(memory_space=pl.ANY),
                  pl.BlockSpec((tm, 2), lambda i, dst: (i, 0)),
                  pl.BlockSpec((None, 1, d), lambda i, dst: (_cond_row(i, tm), 0, ga_chunk))],
        out_specs=pl.BlockSpec((tm, d), lambda i, dst: (i, 0)),
        scratch_shapes=[pltpu.VMEM((2, 2, tm, d), F32), pltpu.SemaphoreType.DMA((2,))],
    )
    return pl.pallas_call(
        functools.partial(_combine_kernel, tm=tm, n=n),
        grid_spec=grid_spec,
        out_shape=jax.ShapeDtypeStruct((n, d), F32),
        compiler_params=_cp("arbitrary"),
        name="combine",
    )(dest, x, y, w2, mod_l)


def _layer(x, l, mod_l, p, lbs, w_router, router_bias, states, new_states):
    n = x.shape[0]
    state_hgrn, state_rglru, state_ssd = states
    h = _normmod(x, p["norm1_g"][l], mod_l, chunk=(0, 1))
    proj = _mm(h, p["w_in"], l, 0, N_MAIN)
    dtp = _mm(h, p["w_dt"], l, 0, 512, tn=512)
    gates = _mm(h, p["w_merge"], l, 0, 3 * D_MODEL, act="sigmoid", out_dtype=BF16)
    nb = n // TOK_BLOCK
    ns = nb - N_CTX_BLOCKS
    hg_prev, ssd_prev = new_states
    oa, hg_new = _hgrn(proj, lbs[l], p["hgrn_norm_g"][l], 0, N_CTX_BLOCKS, SEQ, layer=l, st_prev=hg_prev)
    oa, _ = _hgrn(proj, lbs[l], p["hgrn_norm_g"][l], N_CTX_BLOCKS, ns, TOK_BLOCK, state_hgrn, l, prev=oa)
    ob, lru_new = _lru(proj, p, l, 0, N_CTX_BLOCKS, SEQ)
    ob, _ = _lru(proj, p, l, N_CTX_BLOCKS, ns, GRID_W, state_rglru, prev=ob)
    oc, ssd_new = _ssd(proj, dtp, p, l, 0, N_CTX_BLOCKS, SEQ, SEQ, st_prev=ssd_prev)
    oc, _ = _ssd(proj, dtp, p, l, N_CTX_BLOCKS, ns, TOK_BLOCK, GRID_W, state_ssd, prev=oc)
    merged = _merge(oa, ob, oc, p["w_branch_a"], p["w_branch_b"], p["w_branch_c"], gates, l)
    x = _mm_residual(merged, p["w_out"], l, x, mod_l, 2)
    h2, logits = _normmod(x, p["norm2_g"][l], mod_l, chunk=(3, 4), w_router=w_router, out_dtype=F32)
    e2, w2 = _route(logits.T, router_bias)
    n_tiles = (2 * n) // MOE_TM + N_EXPERTS
    dest, src, tile_expert, n_used = _dispatch_plan(e2, MOE_TM, n_tiles)
    y = _moe(h2, src, tile_expert, n_used, p["w_e_gate"], p["w_e_up"], p["w_e_down"], l)
    x = _combine(x, y, dest, w2.T, mod_l, 5)
    return x, hg_new, lru_new.transpose(1, 0, 2), ssd_new


def _hgrn_lower_bounds(lb_raw):
    pr = jax.nn.softmax(lb_raw.astype(F32), axis=0)
    cum = jnp.cumsum(pr, axis=0)
    return cum - cum[0]


def kernel(x_prompt, x_sample, state_hgrn, state_rglru, state_ssd, c, c_ctx, w_mod, b_mod, norm1_g, norm2_g, w_in, hgrn_lb, hgrn_norm_g, conv_b_w, conv_b_b, lru_wa, lru_ba, lru_wx, lru_bx, lru_lambda, conv_c_w, conv_c_b, ssd_a_log, ssd_dt_bias, ssd_d, ssd_norm_g, w_branch_a, w_branch_b, w_branch_c, w_out, w_router, router_bias, w_e_gate, w_e_up, w_e_down, final_g):
    bsz, seq, d = x_prompt.shape
    dbsz, dseq, _ = x_sample.shape
    assert seq == SEQ and dseq == TOK_BLOCK and (bsz * seq) == N_CTX_BLOCKS * TOK_BLOCK and d == D_MODEL
    x = jnp.concatenate([x_prompt.reshape(-1, d), x_sample.reshape(-1, d)], axis=0)

    cond8 = jnp.zeros((8, d), F32).at[0].set(c_ctx).at[1:1 + dbsz].set(c)
    mod = _modulation(cond8, w_mod, b_mod).reshape(DEPTH, 8, 1, N_MOD * d)

    hpg = C_HEADS // C_GROUPS
    w_dt_raw = w_in[:, :, COL_DT:COL_DT + 2 * C_HEADS].reshape(DEPTH, d, 2, C_GROUPS, hpg)
    w_dt = jnp.zeros((DEPTH, d, C_GROUPS, 128), F32).at[:, :, :, :2 * hpg].set(
        w_dt_raw.transpose(0, 1, 3, 2, 4).reshape(DEPTH, d, C_GROUPS, 2 * hpg)).reshape(DEPTH, d, C_GROUPS * 128)

    def group_lanes(a):
        g = a.reshape(DEPTH, 2, C_GROUPS, hpg).transpose(0, 2, 1, 3).reshape(DEPTH, C_GROUPS, 1, 2 * hpg)
        return jnp.zeros((DEPTH, C_GROUPS, 1, 128), F32).at[..., :2 * hpg].set(g)

    p = dict(norm1_g=norm1_g, norm2_g=norm2_g, w_in=w_in, w_dt=w_dt,
             w_merge=w_in[:, :, COL_MERGE:], hgrn_norm_g=hgrn_norm_g,
             conv_b_w=conv_b_w, conv_b_b=conv_b_b, lru_wa=lru_wa, lru_ba=lru_ba, lru_wx=lru_wx,
             lru_bx=lru_bx, lru_lambda=lru_lambda, conv_c_w=conv_c_w, conv_c_b=conv_c_b,
             dt_bias_g=group_lanes(ssd_dt_bias), a_log_g=group_lanes(ssd_a_log),
             ssd_d_rep=jnp.repeat(ssd_d, C_HEADDIM, axis=-1).reshape(DEPTH, 1, C_INNER),
             ssd_norm_g=ssd_norm_g, w_branch_a=w_branch_a, w_branch_b=w_branch_b, w_branch_c=w_branch_c,
             w_out=w_out, w_e_gate=w_e_gate, w_e_up=w_e_up, w_e_down=w_e_down)
    lbs = _hgrn_lower_bounds(hgrn_lb)

    hg_new = ssd_new = None
    lru_list = []
    for l in range(DEPTH):
        x, hg_new, lru_new, ssd_new = _layer(x, l, mod[l], p, lbs, w_router, router_bias,
                                             (state_hgrn, state_rglru, state_ssd), (hg_new, ssd_new))
        lru_list.append(lru_new)
    n_ctx = bsz * seq
    y_prompt = _normmod(x, final_g, out_dtype=F32, row0=0, n_rows=n_ctx).reshape(bsz, seq, d)
    y_sample = _normmod(x, final_g, out_dtype=F32, row0=n_ctx, n_rows=dbsz * dseq).reshape(dbsz, dseq, d)
    return (y_prompt, y_sample, hg_new, jnp.stack(lru_list, axis=1), ssd_new)
```

```python
import functools
import math

import numpy as np
import jax
import jax.numpy as jnp
from jax import lax
from jax.experimental import pallas as pl
from jax.experimental.pallas import tpu as pltpu

F32 = jnp.float32
BF16 = jnp.bfloat16
HIGHEST = lax.Precision.HIGHEST

EPS = 1e-6
D_MODEL = 2048
DEPTH = 4
N_MOD = 6
GRID_W = 64
SEQ = 256
TOK_BLOCK = 2048
N_CTX_BLOCKS = 2

A_HEADS = 8
A_DK = 128
A_DV = 128
A_WIDTH = 1024
HGRN_C = 128

B_WIDTH = 1024
B_BLOCK = 128
LRU_C = 8.0
LRU_SEG = 256

C_INNER = 1024
C_HEADDIM = 64
C_HEADS = 16
C_GROUPS = 4
C_STATE = 128
SSD_L = 128
CONV_W = 4

N_EXPERTS = 16
N_EXPERT_GROUPS = 4
D_EXPERT = 1024
MOE_TM = 1024
MOE_TF = 256

COL_Q, COL_FF, COL_FB, COL_V, COL_GA = 0, 1024, 2048, 3072, 4096
COL_XB, COL_GB, COL_Z, COL_XBC, COL_DT, COL_MERGE = 5120, 6144, 7168, 8192, 10240, 10272
N_MAIN = 10240

VMEM_LIMIT_BYTES = 56 * 1024 * 1024
NEG_BIG = -1e30


def _cp(*sem):
    return pltpu.CompilerParams(dimension_semantics=sem, vmem_limit_bytes=VMEM_LIMIT_BYTES)


def _silu(x):
    return x * jax.nn.sigmoid(x)


def _dot(a, b):
    return jnp.dot(a, b, preferred_element_type=F32)


def _dot_nt(a, b):
    return lax.dot_general(a, b, (((1,), (1,)), ((), ())), preferred_element_type=F32)


def _dot_tn(a, b):
    return lax.dot_general(a, b, (((0,), (0,)), ((), ())), preferred_element_type=F32)


def _dot_hi(a, b):
    return jnp.dot(a, b, preferred_element_type=F32, precision=HIGHEST)


def _pack_bf16_pairs(h):
    half = h.shape[1] // 2
    bits = lax.bitcast_convert_type(h.astype(BF16).astype(F32), jnp.uint32)
    return bits[:, :half] | (bits[:, half:] >> 16)


def _unpack_bf16_pairs(p):
    hi = lax.bitcast_convert_type(p & jnp.uint32(0xFFFF0000), F32)
    lo = lax.bitcast_convert_type(p << 16, F32)
    return hi.astype(BF16), lo.astype(BF16)


def _cond_row(i, tm):
    return jnp.maximum((i * tm) // TOK_BLOCK - (N_CTX_BLOCKS - 1), 0)


def _mod_kernel(c_ref, w_ref, b_ref, o_ref):
    s = _silu(c_ref[...]).astype(BF16)
    o_ref[...] = _dot(s, w_ref[...].astype(BF16)) + b_ref[...]


def _modulation(cond8, w_mod, b_mod):
    depth, d, n = w_mod.shape
    tn = 1024
    return pl.pallas_call(
        _mod_kernel,
        grid=(depth, n // tn),
        in_specs=[pl.BlockSpec((8, d), lambda l, j: (0, 0)),
                  pl.BlockSpec((None, d, tn), lambda l, j: (l, 0, j)),
                  pl.BlockSpec((None, 1, tn), lambda l, j: (l, 0, j))],
        out_specs=pl.BlockSpec((None, 8, tn), lambda l, j: (l, 0, j)),
        out_shape=jax.ShapeDtypeStruct((depth, 8, n), F32),
        compiler_params=_cp("arbitrary", "arbitrary"),
        name="modulation",
    )(cond8, w_mod, b_mod.reshape(depth, 1, n))


def _normmod_kernel(*refs, modulate, router):
    it = iter(refs)
    x_ref, g_ref = next(it), next(it)
    sc_ref = sh_ref = wr_ref = lg_ref = None
    if modulate:
        sc_ref, sh_ref = next(it), next(it)
    if router:
        wr_ref = next(it)
    h_ref = next(it)
    if router:
        lg_ref = next(it)
    x = x_ref[...]
    h = x * lax.rsqrt(jnp.mean(x * x, axis=-1, keepdims=True) + EPS) * g_ref[...]
    if modulate:
        h = h * (1.0 + sc_ref[...]) + sh_ref[...]
    if h_ref.dtype == jnp.uint32:
        h_ref[...] = _pack_bf16_pairs(h)
    else:
        h_ref[...] = h.astype(h_ref.dtype)
    if router:
        lg_ref[...] = _dot_hi(h, wr_ref[...])


def _normmod(x, g, mod_l=None, chunk=None, w_router=None, out_dtype=BF16, tm=512, row0=0, n_rows=None):
    d = x.shape[1]
    n = x.shape[0] if n_rows is None else n_rows
    r0 = row0 // tm
    modulate = mod_l is not None
    router = w_router is not None
    assert not (modulate and row0)
    in_specs = [pl.BlockSpec((tm, d), lambda i: (i + r0, 0)),
                pl.BlockSpec((1, d), lambda i: (0, 0))]
    args = [x, g.reshape(1, d)]
    if modulate:
        sh_c, sc_c = chunk
        in_specs += [pl.BlockSpec((None, 1, d), lambda i: (_cond_row(i, tm), 0, sc_c)),
                     pl.BlockSpec((None, 1, d), lambda i: (_cond_row(i, tm), 0, sh_c))]
        args += [mod_l, mod_l]
    d_out = d // 2 if out_dtype == jnp.uint32 else d
    out_specs = [pl.BlockSpec((tm, d_out), lambda i: (i, 0))]
    out_shape = [jax.ShapeDtypeStruct((n, d_out), out_dtype)]
    if router:
        ne = w_router.shape[1]
        in_specs.append(pl.BlockSpec((d, ne), lambda i: (0, 0)))
        args.append(w_router)
        out_specs.append(pl.BlockSpec((tm, ne), lambda i: (i, 0)))
        out_shape.append(jax.ShapeDtypeStruct((n, ne), F32))
    res = pl.pallas_call(
        functools.partial(_normmod_kernel, modulate=modulate, router=router),
        grid=(n // tm,),
        in_specs=in_specs, out_specs=out_specs, out_shape=out_shape,
        compiler_params=_cp("arbitrary"),
        name="normmod",
    )(*args)
    return res if router else res[0]


def _mm_kernel(a_ref, w_ref, o_ref):
    o_ref[...] = _dot(a_ref[...], w_ref[...].astype(BF16))


def _mm(a, w, layer, col0, n_out, tm=2048, tn=512):
    m, k = a.shape
    tn = min(tn, n_out)
    off = col0 // tn
    assert col0 % tn == 0 and n_out % tn == 0 and m % tm == 0
    return pl.pallas_call(
        _mm_kernel,
        grid=(m // tm, n_out // tn),
        in_specs=[pl.BlockSpec((tm, k), lambda i, j: (i, 0)),
                  pl.BlockSpec((None, k, tn), lambda i, j: (layer, 0, j + off))],
        out_specs=pl.BlockSpec((tm, tn), lambda i, j: (i, j)),
        out_shape=jax.ShapeDtypeStruct((m, n_out), F32),
        compiler_params=_cp("arbitrary", "arbitrary"),
        name="matmul",
    )(a, w)


def _gates_kernel(a_ref, w_ref, wn_ref, o_ref, *, shift):
    acc = _dot(a_ref[...], w_ref[...].astype(BF16))
    nxt = _dot(a_ref[...], wn_ref[...].astype(BF16))
    full = jnp.concatenate([acc, nxt], axis=1)
    o_ref[...] = jax.nn.sigmoid(full[:, shift:shift + acc.shape[1]]).astype(o_ref.dtype)


def _gates(a, w, layer, col0, n_out, tm=2048, tn=1024, lane=128):
    m, k = a.shape
    shift = col0 % lane
    base = col0 - shift
    assert base % tn == 0 and n_out % tn == 0 and tn % lane == 0
    return pl.pallas_call(
        functools.partial(_gates_kernel, shift=shift),
        grid=(m // tm, n_out // tn),
        in_specs=[pl.BlockSpec((tm, k), lambda i, j: (i, 0)),
                  pl.BlockSpec((None, k, tn), lambda i, j: (layer, 0, j + base // tn)),
                  pl.BlockSpec((None, k, lane), lambda i, j: (layer, 0, (base + (j + 1) * tn) // lane))],
        out_specs=pl.BlockSpec((tm, tn), lambda i, j: (i, j)),
        out_shape=jax.ShapeDtypeStruct((m, n_out), BF16),
        compiler_params=_cp("arbitrary", "arbitrary"),
        name="merge_gates",
    )(a, w, w)


def _mm_res_kernel(a_ref, w_ref, x_ref, ga_ref, o_ref):
    acc = _dot(a_ref[...], w_ref[...].astype(BF16))
    o_ref[...] = x_ref[...] + ga_ref[...] * acc


def _mm_residual(a, w, layer, x, mod_l, ga_chunk, tm=1024, tn=512):
    m, k = a.shape
    n = w.shape[-1]
    gs = n // tn
    return pl.pallas_call(
        _mm_res_kernel,
        grid=(m // tm, n // tn),
        in_specs=[pl.BlockSpec((tm, k), lambda i, j: (i, 0)),
                  pl.BlockSpec((None, k, tn), lambda i, j: (layer, 0, j)),
                  pl.BlockSpec((tm, tn), lambda i, j: (i, j)),
                  pl.BlockSpec((None, 1, tn), lambda i, j: (_cond_row(i, tm), 0, ga_chunk * gs + j))],
        out_specs=pl.BlockSpec((tm, tn), lambda i, j: (i, j)),
        out_shape=jax.ShapeDtypeStruct((m, n), F32),
        compiler_params=_cp("arbitrary", "arbitrary"),
        name="out_proj",
    )(a, w, x, mod_l)


def _merge_kernel(oa_ref, ob_ref, oc_ref, wa_ref, wb_ref, wc_ref, g1_ref, g2_ref, g3_ref, o_ref):
    ya = _dot(oa_ref[...], wa_ref[...].astype(BF16))
    yb = _dot(ob_ref[...], wb_ref[...].astype(BF16))
    yc = _dot(oc_ref[...], wc_ref[...].astype(BF16))
    m = (g1_ref[...].astype(F32) * ya + g2_ref[...].astype(F32) * yb + g3_ref[...].astype(F32) * yc)
    o_ref[...] = m.astype(o_ref.dtype)


def _merge(oa, ob, oc, wa, wb, wc, gates, layer, tm=1024, tn=512):
    m, k = oa.shape
    n = wa.shape[-1]
    gs = n // tn
    a_spec = pl.BlockSpec((tm, k), lambda i, j: (i, 0))
    w_spec = pl.BlockSpec((None, k, tn), lambda i, j: (layer, 0, j))
    return pl.pallas_call(
        _merge_kernel,
        grid=(m // tm, n // tn),
        in_specs=[a_spec, a_spec, a_spec, w_spec, w_spec, w_spec,
                  pl.BlockSpec((tm, tn), lambda i, j: (i, j)),
                  pl.BlockSpec((tm, tn), lambda i, j: (i, j + gs)),
                  pl.BlockSpec((tm, tn), lambda i, j: (i, j + 2 * gs))],
        out_specs=pl.BlockSpec((tm, tn), lambda i, j: (i, j)),
        out_shape=jax.ShapeDtypeStruct((m, n), BF16),
        compiler_params=_cp("arbitrary", "arbitrary"),
        name="merge",
    )(oa, ob, oc, wa, wb, wc, gates, gates, gates)


def _dwconv(x, w, b, row_len):
    t = x.shape[0]
    pos = lax.broadcasted_iota(jnp.int32, x.shape, 0) % row_len
    xm2 = jnp.where(pos >= 2, pltpu.roll(x, 2, axis=0), 0.0)
    xm1 = jnp.where(pos >= 1, pltpu.roll(x, 1, axis=0), 0.0)
    xp1 = jnp.where(pos <= row_len - 2, pltpu.roll(x, t - 1, axis=0), 0.0)
    y = b + xm2 * w[0:1]
    y = y + xm1 * w[1:2]
    y = y + x * w[2:3]
    y = y + xp1 * w[3:4]
    return y


def _hgrn_consts(c):
    nl = int(math.log2(c))
    seg = np.zeros((2, nl + 1, c, c), np.float32)
    msk = np.zeros((2, nl + 1, c, c), np.float32)
    up = np.zeros((2, nl, c, 128), np.float32)
    for d in range(2):
        tt = np.arange(c) if d == 0 else c - 1 - np.arange(c)
        tr, tc = tt[:, None], tt[None, :]
        seg[d, 0] = tc <= tr
        for l in range(nl):
            s = 1 << l
            blk, upper = tt // (2 * s), (tt % (2 * s)) >= s
            mid = (blk * 2 * s + s)[:, None]
            seg_u = (tc >= mid) & (tc <= tr)
            seg_l = (tc >= tr + 1) & (tc <= mid - 1)
            seg[d, l + 1] = np.where(upper[:, None], seg_u, seg_l)
            msk[d, l] = upper[:, None] & (~upper)[None, :] & (blk[:, None] == blk[None, :])
            up[d, l] = upper[:, None]
        msk[d, nl] = np.eye(c)
    return seg.reshape(2, (nl + 1) * c, c), msk, up


def _hgrn_kernel(*refs, seq_len, chained, aliased):
    it = iter(refs)
    q_ref, ff_ref, fb_ref, v_ref, ga_ref, lb_ref, ng_ref = (next(it) for _ in range(7))
    seg_ref, msk_ref, up_ref = next(it), next(it), next(it)
    s0_ref = next(it) if chained else None
    for _ in range(aliased):
        next(it)
    o_ref = next(it)
    sout_ref = None if chained else next(it)
    o_scr, qp_scr, u_scr, dec_scr, snap_scr = (next(it) for _ in range(5))

    c = HGRN_C
    nl = int(math.log2(c))
    tb = q_ref.shape[0]
    n_chunks = tb // c
    n_ch = seq_len // c
    lb = lb_ref[...]
    f_refs = (ff_ref, fb_ref)

    def phase_a(n, carry):
        rows = pl.ds(pl.multiple_of(n * c, c), c)
        q = _silu(q_ref[rows, :]) * (A_DK ** -0.5)
        qb = q.astype(BF16)
        v = v_ref[rows, :].astype(BF16)
        o = jnp.zeros((c, A_DV), F32)
        qps, kps, decs = [], [], []
        for d in range(2):
            f = lb + (1.0 - lb) * jax.nn.sigmoid(f_refs[d][rows, :])
            k = 1.0 - f
            lf = jnp.log(f)
            hi = lf.astype(BF16)
            mid = (lf - hi.astype(F32)).astype(BF16)
            e2 = _dot(seg_ref[d], jnp.concatenate([hi, mid], axis=1))
            e = e2[:, :A_DK] + e2[:, A_DK:]
            b = e[0:c]
            att = msk_ref[d, nl] * _dot_nt(qb, k.astype(BF16))
            for l in range(nl):
                x = (jnp.where(up_ref[d, l] > 0.5, q, k) * jnp.exp(e[(l + 1) * c:(l + 2) * c])).astype(BF16)
                att = att + msk_ref[d, l] * _dot_nt(x, x)
            o = o + _dot(att.astype(BF16), v)
            b_end = b[c - 1:c] if d == 0 else b[0:1]
            qps.append((q * jnp.exp(b)).astype(BF16))
            kps.append((k * jnp.exp(b_end - b)).astype(BF16))
            decs.append(jnp.exp(b_end))
        o_scr[rows, :] = o
        qp_scr[rows, :] = jnp.concatenate(qps, axis=1)
        u_scr[n] = _dot_tn(v, jnp.concatenate(kps, axis=1))
        dec_scr[n] = jnp.concatenate(decs, axis=1)
        return carry

    lax.fori_loop(0, n_chunks, phase_a, 0)

    def init_state(d):
        return s0_ref[d].T if chained else jnp.zeros((A_DV, A_DK), F32)

    def phase_b1(j, carry):
        sf, sb = carry
        nf = j
        nb = n_chunks - 1 - j
        if not chained:
            sf = jnp.where(nf % n_ch == 0, 0.0, sf)
            sb = jnp.where(nb % n_ch == n_ch - 1, 0.0, sb)
        snap_scr[nf, :, 0:A_DK] = sf.astype(BF16)
        snap_scr[nb, :, A_DK:2 * A_DK] = sb.astype(BF16)
        sf = sf * dec_scr[nf][:, 0:A_DK] + u_scr[nf][:, 0:A_DK]
        sb = sb * dec_scr[nb][:, A_DK:2 * A_DK] + u_scr[nb][:, A_DK:2 * A_DK]
        if not chained:
            @pl.when(nf % n_ch == n_ch - 1)
            def _():
                sout_ref[nf // n_ch, 0] = sf.T

            @pl.when(nb % n_ch == 0)
            def _():
                sout_ref[nb // n_ch, 1] = sb.T
        return sf, sb

    lax.fori_loop(0, n_chunks, phase_b1, (init_state(0), init_state(1)))

    ng = ng_ref[...]

    def phase_b2(n, carry):
        rows = pl.ds(pl.multiple_of(n * c, c), c)
        o = o_scr[rows, :] + _dot_nt(qp_scr[rows, :], snap_scr[n])
        o = o * lax.rsqrt(jnp.mean(o * o, axis=-1, keepdims=True) + EPS) * ng
        o_ref[rows, :] = (o * _silu(ga_ref[rows, :])).astype(o_ref.dtype)
        return carry

    lax.fori_loop(0, n_chunks, phase_b2, 0, unroll=4)


def _alias_out(prevs, in_specs, args):
    aliases = {}
    for out_idx, prev in prevs:
        if prev is not None:
            in_specs.append(pl.BlockSpec(memory_space=pl.ANY))
            args.append(prev)
            aliases[len(args) - 1] = out_idx
    return aliases


def _hgrn(proj, lb_l, ng_l, blk0, n_blk, seq_len, state=None, layer=0, prev=None, st_prev=None):
    tb = TOK_BLOCK
    chained = state is not None
    seg, msk, up = _hgrn_consts(HGRN_C)
    cb = lambda col: col // A_DK

    def col_spec(col):
        return pl.BlockSpec((tb, A_DK), lambda b, h: (b + blk0, cb(col) + h))

    def full(a):
        nd = a.ndim
        return pl.BlockSpec(a.shape, lambda b, h: (0,) * nd)

    in_specs = [col_spec(COL_Q), col_spec(COL_FF), col_spec(COL_FB), col_spec(COL_V), col_spec(COL_GA),
                pl.BlockSpec((1, A_DK), lambda b, h: (0, h)),
                pl.BlockSpec((1, A_DV), lambda b, h: (0, 0)),
                full(seg), full(msk), full(up)]
    args = [proj, proj, proj, proj, proj, lb_l.reshape(1, -1), ng_l.reshape(1, -1),
            jnp.asarray(seg, BF16), jnp.asarray(msk), jnp.asarray(up)]
    out_specs = [pl.BlockSpec((tb, A_DV), lambda b, h: (b + blk0, h))]
    out_shape = [jax.ShapeDtypeStruct((proj.shape[0], A_WIDTH), BF16)]
    if chained:
        in_specs.append(pl.BlockSpec((None, None, 2, None, A_DK, A_DV), lambda b, h: (b, layer, 0, h, 0, 0)))
        args.append(state)
    else:
        n_seq = tb // seq_len
        out_specs.append(pl.BlockSpec((n_seq, None, 2, None, A_DK, A_DV), lambda b, h: (b, layer, 0, h, 0, 0)))
        out_shape.append(jax.ShapeDtypeStruct((n_blk * n_seq, DEPTH, 2, A_HEADS, A_DK, A_DV), F32))
    aliases = _alias_out([(0, prev), (1, st_prev)], in_specs, args)
    n_chunks = tb // HGRN_C
    res = pl.pallas_call(
        functools.partial(_hgrn_kernel, seq_len=seq_len, chained=chained, aliased=len(aliases)),
        grid=(n_blk, A_HEADS),
        in_specs=in_specs, out_specs=out_specs, out_shape=out_shape,
        input_output_aliases=aliases,
        scratch_shapes=[pltpu.VMEM((tb, A_DV), F32), pltpu.VMEM((tb, 2 * A_DK), BF16),
                        pltpu.VMEM((n_chunks, A_DV, 2 * A_DK), F32), pltpu.VMEM((n_chunks, 1, 2 * A_DK), F32),
                        pltpu.VMEM((n_chunks, A_DV, 2 * A_DK), BF16)],
        compiler_params=_cp("arbitrary", "arbitrary"),
        name="hgrn2",
    )(*args)
    return res[0], (None if chained else res[1])


def _lru_kernel(*refs, row_len, chained, aliased):
    it = iter(refs)
    x_ref, gb_ref, cw_ref, cb_ref, wa_ref, ba_ref, wx_ref, bx_ref, lam_ref = (next(it) for _ in range(9))
    h0_ref = next(it) if chained else None
    for _ in range(aliased):
        next(it)
    o_ref = next(it)
    hout_ref = None if chained else next(it)
    a_scr, u_scr, h_scr, p_scr = (next(it) for _ in range(4))

    tb = x_ref.shape[0]
    seg = LRU_SEG
    n_seg = tb // seg
    xc = _dwconv(x_ref[...], cw_ref[...], cb_ref[...], row_len)
    xcb = xc.astype(BF16)
    for d in range(2):
        r = jax.nn.sigmoid(_dot(xcb, wa_ref[d].astype(BF16)) + ba_ref[d])
        g = jax.nn.sigmoid(_dot(xcb, wx_ref[d].astype(BF16)) + bx_ref[d])
        log_a = LRU_C * r * jax.nn.log_sigmoid(lam_ref[d])
        a = jnp.exp(log_a)
        a_scr[d] = a
        u_scr[d] = jnp.sqrt(1.0 - a * a) * (g * xc)

    def step(i, carry):
        hf, pf, hb, pb = carry
        tf = i
        tr = seg - 1 - i
        af = a_scr[0, pl.ds(tf, n_seg, stride=seg), :]
        hf = af * hf + u_scr[0, pl.ds(tf, n_seg, stride=seg), :]
        h_scr[0, pl.ds(tf, n_seg, stride=seg), :] = hf
        ab = a_scr[1, pl.ds(tr, n_seg, stride=seg), :]
        hb = ab * hb + u_scr[1, pl.ds(tr, n_seg, stride=seg), :]
        h_scr[1, pl.ds(tr, n_seg, stride=seg), :] = hb
        if chained:
            pf = af * pf
            pb = ab * pb
            p_scr[0, pl.ds(tf, n_seg, stride=seg), :] = pf
            p_scr[1, pl.ds(tr, n_seg, stride=seg), :] = pb
        return hf, pf, hb, pb

    zeros = jnp.zeros((n_seg, B_BLOCK), F32)
    ones = jnp.ones((n_seg, B_BLOCK), F32)
    hf, pf, hb, pb = lax.fori_loop(0, seg, step, (zeros, ones, zeros, ones), unroll=8)

    if chained:
        hin = h0_ref[0:1, :]
        for s in range(n_seg):
            rows = pl.ds(s * seg, seg)
            h_scr[0, rows, :] = h_scr[0, rows, :] + p_scr[0, rows, :] * hin
            hin = hf[s:s + 1] + pf[s:s + 1] * hin
        hin = h0_ref[1:2, :]
        for s in range(n_seg - 1, -1, -1):
            rows = pl.ds(s * seg, seg)
            h_scr[1, rows, :] = h_scr[1, rows, :] + p_scr[1, rows, :] * hin
            hin = hb[s:s + 1] + pb[s:s + 1] * hin
    else:
        hout_ref[0] = hf
        hout_ref[1] = hb
    o_ref[...] = ((h_scr[0] + h_scr[1]) * jax.nn.gelu(gb_ref[...])).astype(o_ref.dtype)


def _lru(proj, p, layer, blk0, n_blk, row_len, state=None, prev=None):
    tb = TOK_BLOCK
    chained = state is not None
    nb = B_WIDTH // B_BLOCK
    cbx, cbg = COL_XB // B_BLOCK, COL_GB // B_BLOCK
    in_specs = [pl.BlockSpec((tb, B_BLOCK), lambda b, n: (b + blk0, cbx + n)),
                pl.BlockSpec((tb, B_BLOCK), lambda b, n: (b + blk0, cbg + n)),
                pl.BlockSpec((None, CONV_W, B_BLOCK), lambda b, n: (layer, 0, n)),
                pl.BlockSpec((None, 1, B_BLOCK), lambda b, n: (layer, 0, n)),
                pl.BlockSpec((None, 2, None, B_BLOCK, B_BLOCK), lambda b, n: (layer, 0, n, 0, 0)),
                pl.BlockSpec((None, 2, 1, B_BLOCK), lambda b, n: (layer, 0, 0, n)),
                pl.BlockSpec((None, 2, None, B_BLOCK, B_BLOCK), lambda b, n: (layer, 0, n, 0, 0)),
                pl.BlockSpec((None, 2, 1, B_BLOCK), lambda b, n: (layer, 0, 0, n)),
                pl.BlockSpec((None, 2, 1, B_BLOCK), lambda b, n: (layer, 0, 0, n))]
    d4 = lambda a: a.reshape(DEPTH, 2, 1, B_WIDTH)
    args = [proj, proj, p["conv_b_w"], p["conv_b_b"].reshape(DEPTH, 1, B_WIDTH),
            p["lru_wa"], d4(p["lru_ba"]), p["lru_wx"], d4(p["lru_bx"]), d4(p["lru_lambda"])]
    out_specs = [pl.BlockSpec((tb, B_BLOCK), lambda b, n: (b + blk0, n))]
    out_shape = [jax.ShapeDtypeStruct((proj.shape[0], B_WIDTH), BF16)]
    if chained:
        in_specs.append(pl.BlockSpec((None, None, 2, B_BLOCK), lambda b, n: (b, layer, 0, n)))
        args.append(state)
    else:
        n_seq = tb // LRU_SEG
        out_specs.append(pl.BlockSpec((2, n_seq, B_BLOCK), lambda b, n: (0, b, n)))
        out_shape.append(jax.ShapeDtypeStruct((2, n_blk * n_seq, B_WIDTH), F32))
    aliases = _alias_out([(0, prev)], in_specs, args)
    res = pl.pallas_call(
        functools.partial(_lru_kernel, row_len=row_len, chained=chained, aliased=len(aliases)),
        grid=(n_blk, nb),
        in_specs=in_specs, out_specs=out_specs, out_shape=out_shape,
        input_output_aliases=aliases,
        scratch_shapes=[pltpu.VMEM((2, tb, B_BLOCK), F32) for _ in range(4)],
        compiler_params=_cp("arbitrary", "arbitrary"),
        name="rglru",
    )(*args)
    return res[0], (None if chained else res[1])


def _ssd_kernel(*refs, seq_len, row_len, chained, aliased):
    it = iter(refs)
    (xs_ref, bm_ref, cm_ref, z_ref, dt_ref, cwx_ref, cwb_ref, cwc_ref, cbx_ref, cbb_ref, cbc_ref,
     dtb_ref, a_ref, dsk_ref, ng_ref, tri_ref, ecat_ref, hmask_ref) = (next(it) for _ in range(18))
    s0_ref = next(it) if chained else None
    for _ in range(aliased):
        next(it)
    o_ref = next(it)
    sout_ref = None if chained else next(it)
    (xs_scr, bm_scr, cm_scr, dt_scr, y_scr, dcy_scr, u_scr, dec_scr, snap_scr,
     st_scr) = (next(it) for _ in range(10))

    lc = SSD_L
    hpg = C_HEADS // C_GROUPS
    gw = hpg * C_HEADDIM
    tb = xs_ref.shape[0]
    n_chunks = tb // lc
    n_ch = seq_len // lc
    xs_scr[...] = _silu(_dwconv(xs_ref[...], cwx_ref[...], cbx_ref[...], row_len))
    bm_scr[...] = _silu(_dwconv(bm_ref[...], cwb_ref[...], cbb_ref[...], row_len)).astype(BF16)
    cm_scr[...] = _silu(_dwconv(cm_ref[...], cwc_ref[...], cbc_ref[...], row_len)).astype(BF16)
    dt_scr[...] = jax.nn.softplus(dt_ref[...] + dtb_ref[...])
    a_row = -jnp.exp(a_ref[...])
    rr = lax.broadcasted_iota(jnp.int32, (lc, lc), 0)
    cc = lax.broadcasted_iota(jnp.int32, (lc, lc), 1)
    causal = (rr >= cc, cc >= rr)
    rep0 = hpg * lc

    def split2(x):
        hi = x.astype(BF16)
        return hi, (x - hi.astype(F32)).astype(BF16)

    def phase_a(n, carry):
        rows = pl.ds(pl.multiple_of(n * lc, lc), lc)
        dtc = dt_scr[rows, :]
        hi, mid = split2(dtc * a_row)
        c2 = _dot(tri_ref[...], jnp.concatenate([hi, mid], axis=1))
        cum2 = c2[:, :128] + c2[:, 128:]
        dhi, dmid = split2(dtc)
        xs = xs_scr[rows, :]
        bm = bm_scr[rows, :]
        cm = cm_scr[rows, :]
        scores = _dot_nt(cm, bm)
        y = dsk_ref[...] * xs
        for d in range(2):
            cum = cum2[d * lc:(d + 1) * lc]
            chi, cmid = split2(cum)
            r4 = _dot(jnp.concatenate([chi, cmid, dhi, dmid], axis=0), ecat_ref[d])
            rep = r4[0:lc] + r4[lc:2 * lc]
            rep64 = rep[:, rep0:]
            dtrep = r4[2 * lc:3 * lc, rep0:] + r4[3 * lc:, rep0:]
            cum_t = cum.T
            ps = []
            for hh in range(hpg):
                ln = d * hpg + hh
                seg = jnp.exp(jnp.where(causal[d], rep[:, lc * hh:lc * (hh + 1)] - cum_t[ln:ln + 1, :], NEG_BIG))
                ps.append((scores * seg).astype(BF16))
            xdt = xs * dtrep
            rhs = jnp.concatenate([(xdt * hmask_ref[hh]).astype(BF16) for hh in range(hpg)], axis=0)
            y = y + _dot(jnp.concatenate(ps, axis=1), rhs)
            end = rep64[lc - 1:lc] if d == 0 else rep64[0:1]
            u_scr[d, n] = _dot_tn(bm, (xdt * jnp.exp(end - rep64)).astype(BF16))
            dec_scr[d, n] = jnp.exp(end)
            dcy_scr[d, rows, :] = jnp.exp(rep64)
        y_scr[rows, :] = y
        return carry

    lax.fori_loop(0, n_chunks, phase_a, 0)

    for d in range(2):
        if chained:
            st_scr[d] = jnp.concatenate([s0_ref[d, hh] for hh in range(hpg)], axis=0).T
        else:
            st_scr[d] = jnp.zeros((C_STATE, gw), F32)

    def phase_b1(j, carry):
        nf = j
        nb = n_chunks - 1 - j
        sf = st_scr[0]
        sb = st_scr[1]
        if not chained:
            sf = jnp.where(nf % n_ch == 0, 0.0, sf)
            sb = jnp.where(nb % n_ch == n_ch - 1, 0.0, sb)
        snap_scr[0, nf] = sf.astype(BF16)
        snap_scr[1, nb] = sb.astype(BF16)
        sf = sf * dec_scr[0, nf] + u_scr[0, nf]
        sb = sb * dec_scr[1, nb] + u_scr[1, nb]
        st_scr[0] = sf
        st_scr[1] = sb
        if not chained:
            @pl.when(nf % n_ch == n_ch - 1)
            def _():
                sft = sf.T
                for hh in range(hpg):
                    sout_ref[nf // n_ch, 0, hh] = sft[hh * C_HEADDIM:(hh + 1) * C_HEADDIM]

            @pl.when(nb % n_ch == 0)
            def _():
                sbt = sb.T
                for hh in range(hpg):
                    sout_ref[nb // n_ch, 1, hh] = sbt[hh * C_HEADDIM:(hh + 1) * C_HEADDIM]
        return carry

    lax.fori_loop(0, n_chunks, phase_b1, 0)

    ng = ng_ref[...]

    def phase_b2(n, carry):
        rows = pl.ds(pl.multiple_of(n * lc, lc), lc)
        cm = cm_scr[rows, :]
        y = y_scr[rows, :] + _dot(cm, snap_scr[0, n]) * dcy_scr[0, rows, :]
        y = y + _dot(cm, snap_scr[1, n]) * dcy_scr[1, rows, :]
        y = y * _silu(z_ref[rows, :])
        y = y * lax.rsqrt(jnp.mean(y * y, axis=-1, keepdims=True) + EPS) * ng
        o_ref[rows, :] = y.astype(o_ref.dtype)
        return carry

    lax.fori_loop(0, n_chunks, phase_b2, 0, unroll=2)


def _ssd(proj, dtp, p, layer, blk0, n_blk, seq_len, row_len, state=None, prev=None, st_prev=None):
    tb = TOK_BLOCK
    chained = state is not None
    hpg = C_HEADS // C_GROUPS
    gw = hpg * C_HEADDIM
    col_x, col_b, col_c = COL_XBC, COL_XBC + C_INNER, COL_XBC + C_INNER + C_GROUPS * C_STATE
    lc = SSD_L
    tri = np.concatenate([np.tril(np.ones((lc, lc), np.float32)), np.triu(np.ones((lc, lc), np.float32))])
    ecat = np.zeros((2, 128, hpg * lc + gw), np.float32)
    hmask = np.zeros((hpg, 1, gw), np.float32)
    for hh in range(hpg):
        hmask[hh, 0, hh * C_HEADDIM:(hh + 1) * C_HEADDIM] = 1.0
        for d in range(2):
            ecat[d, d * hpg + hh, hh * lc:(hh + 1) * lc] = 1.0
            ecat[d, d * hpg + hh, hpg * lc + hh * C_HEADDIM:hpg * lc + (hh + 1) * C_HEADDIM] = 1.0
    in_specs = [pl.BlockSpec((tb, gw), lambda b, g: (b + blk0, col_x // gw + g)),
                pl.BlockSpec((tb, C_STATE), lambda b, g: (b + blk0, col_b // C_STATE + g)),
                pl.BlockSpec((tb, C_STATE), lambda b, g: (b + blk0, col_c // C_STATE + g)),
                pl.BlockSpec((tb, gw), lambda b, g: (b + blk0, COL_Z // gw + g)),
                pl.BlockSpec((tb, 128), lambda b, g: (b + blk0, g)),
                pl.BlockSpec((None, CONV_W, gw), lambda b, g: (layer, 0, g)),
                pl.BlockSpec((None, CONV_W, C_STATE), lambda b, g: (layer, 0, C_INNER // C_STATE + g)),
                pl.BlockSpec((None, CONV_W, C_STATE), lambda b, g: (layer, 0, C_INNER // C_STATE + C_GROUPS + g)),
                pl.BlockSpec((None, 1, gw), lambda b, g: (layer, 0, g)),
                pl.BlockSpec((None, 1, C_STATE), lambda b, g: (layer, 0, C_INNER // C_STATE + g)),
                pl.BlockSpec((None, 1, C_STATE), lambda b, g: (layer, 0, C_INNER // C_STATE + C_GROUPS + g)),
                pl.BlockSpec((None, None, 1, 128), lambda b, g: (layer, g, 0, 0)),
                pl.BlockSpec((None, None, 1, 128), lambda b, g: (layer, g, 0, 0)),
                pl.BlockSpec((None, 1, gw), lambda b, g: (layer, 0, g)),
                pl.BlockSpec((None, 1, gw), lambda b, g: (layer, 0, g)),
                pl.BlockSpec(tri.shape, lambda b, g: (0, 0)),
                pl.BlockSpec(ecat.shape, lambda b, g: (0, 0, 0)),
                pl.BlockSpec(hmask.shape, lambda b, g: (0, 0, 0))]
    cw = p["conv_c_w"]
    cbias = p["conv_c_b"].reshape(DEPTH, 1, -1)
    args = [proj, proj, proj, proj, dtp, cw, cw, cw, cbias, cbias, cbias,
            p["dt_bias_g"], p["a_log_g"], p["ssd_d_rep"], p["ssd_norm_g"].reshape(DEPTH, 1, C_INNER),
            jnp.asarray(tri, BF16), jnp.asarray(ecat, BF16), jnp.asarray(hmask)]
    out_specs = [pl.BlockSpec((tb, gw), lambda b, g: (b + blk0, g))]
    out_shape = [jax.ShapeDtypeStruct((proj.shape[0], C_INNER), BF16)]
    if chained:
        in_specs.append(pl.BlockSpec((None, None, 2, hpg, C_HEADDIM, C_STATE),
                                     lambda b, g: (b, layer, 0, g, 0, 0)))
        args.append(state)
    else:
        n_seq = tb // seq_len
        out_specs.append(pl.BlockSpec((n_seq, None, 2, hpg, C_HEADDIM, C_STATE),
                                      lambda b, g: (b, layer, 0, g, 0, 0)))
        out_shape.append(jax.ShapeDtypeStruct((n_blk * n_seq, DEPTH, 2, C_HEADS, C_HEADDIM, C_STATE), F32))
    aliases = _alias_out([(0, prev), (1, st_prev)], in_specs, args)
    res = pl.pallas_call(
        functools.partial(_ssd_kernel, seq_len=seq_len, row_len=row_len, chained=chained,
                          aliased=len(aliases)),
        grid=(n_blk, C_GROUPS),
        in_specs=in_specs, out_specs=out_specs, out_shape=out_shape,
        input_output_aliases=aliases,
        scratch_shapes=[pltpu.VMEM((tb, gw), F32), pltpu.VMEM((tb, C_STATE), BF16), pltpu.VMEM((tb, C_STATE), BF16),
                        pltpu.VMEM((tb, 128), F32), pltpu.VMEM((tb, gw), F32), pltpu.VMEM((2, tb, gw), F32),
                        pltpu.VMEM((2, tb // lc, C_STATE, gw), F32), pltpu.VMEM((2, tb // lc, 1, gw), F32),
                        pltpu.VMEM((2, tb // lc, C_STATE, gw), BF16), pltpu.VMEM((2, C_STATE, gw), F32)],
        compiler_params=_cp("arbitrary", "arbitrary"),
        name="ssd",
    )(*args)
    return res[0], (None if chained else res[1])


def _route_kernel(lg_ref, bias_ref, e_ref, w_ref):
    lg = lg_ref[...]
    ne = lg.shape[0]
    epg = ne // N_EXPERT_GROUPS
    mx = jnp.max(lg, axis=0, keepdims=True)
    ex = jnp.exp(lg - mx)
    probs = ex / jnp.sum(ex, axis=0, keepdims=True)
    sel = probs + bias_ref[...]
    rows = [sel[e:e + 1] for e in range(ne)]
    top2 = []
    for e in range(ne):
        g0 = (e // epg) * epg
        rank = jnp.zeros_like(rows[e])
        for o in range(g0, g0 + epg):
            if o == e:
                continue
            ahead = (rows[o] > rows[e]) | ((rows[o] == rows[e]) & (o < e))
            rank = rank + jnp.where(ahead, 1.0, 0.0)
        top2.append(rank < 1.5)
    score = []
    for g in range(N_EXPERT_GROUPS):
        sc = jnp.zeros_like(rows[0])
        for e in range(g * epg, (g + 1) * epg):
            sc = sc + jnp.where(top2[e], rows[e], 0.0)
        score.append(sc)
    best = []
    for g in range(N_EXPERT_GROUPS):
        ok = jnp.ones(rows[0].shape, jnp.bool_)
        for o in range(N_EXPERT_GROUPS):
            if o < g:
                ok = ok & (score[g] > score[o])
            elif o > g:
                ok = ok & (score[g] >= score[o])
        best.append(ok)
    first = jnp.full(rows[0].shape, float(ne), F32)
    second = jnp.full(rows[0].shape, -1.0, F32)
    p_first = jnp.zeros_like(rows[0])
    p_second = jnp.zeros_like(rows[0])
    for e in range(ne - 1, -1, -1):
        ch = top2[e] & best[e // epg]
        first = jnp.where(ch, float(e), first)
        p_first = jnp.where(ch, probs[e:e + 1], p_first)
    for e in range(ne):
        ch = top2[e] & best[e // epg]
        second = jnp.where(ch, float(e), second)
        p_second = jnp.where(ch, probs[e:e + 1], p_second)
    tot = p_first + p_second
    e_ref[0:1, :] = first.astype(jnp.int32)
    e_ref[1:2, :] = second.astype(jnp.int32)
    w_ref[0:1, :] = p_first / tot
    w_ref[1:2, :] = p_second / tot


def _route(logits_t, router_bias):
    ne, n = logits_t.shape
    return pl.pallas_call(
        _route_kernel,
        grid=(1,),
        in_specs=[pl.BlockSpec((ne, n), lambda i: (0, 0)), pl.BlockSpec((ne, 1), lambda i: (0, 0))],
        out_specs=[pl.BlockSpec((2, n), lambda i: (0, 0)), pl.BlockSpec((2, n), lambda i: (0, 0))],
        out_shape=[jax.ShapeDtypeStruct((2, n), jnp.int32), jax.ShapeDtypeStruct((2, n), F32)],
        compiler_params=_cp("arbitrary"),
        name="route",
    )(logits_t, router_bias.reshape(ne, 1))


def _dispatch_plan(e2, tm, n_tiles):
    n = e2.shape[1]
    e_flat = e2.reshape(-1)
    onehot = (e_flat[:, None] == jnp.arange(N_EXPERTS, dtype=jnp.int32)[None, :]).astype(jnp.int32)
    counts = jnp.sum(onehot, axis=0)
    rank = jnp.sum(jnp.cumsum(onehot, axis=0) * onehot, axis=1) - 1
    tiles_per = (counts + tm - 1) // tm
    tile_end = jnp.cumsum(tiles_per)
    starts = (tile_end - tiles_per) * tm
    dest = jnp.sum(onehot * starts[None, :], axis=1) + rank
    src = jnp.zeros((n_tiles * tm,), jnp.int32).at[dest].set(jnp.arange(2 * n, dtype=jnp.int32) % n)
    n_used = tile_end[-1]
    t_idx = jnp.minimum(jnp.arange(n_tiles, dtype=jnp.int32), n_used - 1)
    tile_expert = jnp.sum((tile_end[None, :] <= t_idx[:, None]).astype(jnp.int32), axis=1)
    tile_expert = jnp.minimum(tile_expert, N_EXPERTS - 1)
    return dest, src, tile_expert, n_used.reshape(1).astype(jnp.int32)


def _moe_kernel(te_ref, nu_ref, src_ref, h_hbm, wg_ref, wu_ref, wd_ref, o_ref, xbuf, xb_scr, sem, *, tm):
    t, j = pl.program_id(0), pl.program_id(1)
    n_used = nu_ref[0]

    def row_copy(tile, slot, k):
        r = src_ref[tile * tm + k]
        return pltpu.make_async_copy(h_hbm.at[pl.ds(r, 1), :], xbuf.at[slot, pl.ds(k, 1), :], sem.at[slot])

    def issue(tile, slot):
        def body(k, c):
            row_copy(tile, slot, k).start()
            return c
        lax.fori_loop(0, tm, body, 0, unroll=8)

    def drain(tile, slot):
        def body(k, c):
            row_copy(tile, slot, k).wait()
            return c
        lax.fori_loop(0, tm, body, 0, unroll=8)

    @pl.when((j == 0) & (t == 0))
    def _():
        issue(0, 0)

    @pl.when((j == 0) & (t + 1 < n_used))
    def _():
        issue(t + 1, (t + 1) % 2)

    @pl.when((j == 0) & (t < n_used))
    def _():
        drain(t, t % 2)
        half = xbuf.shape[2]
        hi, lo = _unpack_bf16_pairs(xbuf[t % 2])
        xb_scr[:, :half] = hi
        xb_scr[:, half:] = lo

    @pl.when(j == 0)
    def _():
        o_ref[...] = jnp.zeros_like(o_ref)

    @pl.when(t < n_used)
    def _():
        x = xb_scr[...]
        hg = _dot(x, wg_ref[...].astype(BF16))
        hu = _dot(x, wu_ref[...].astype(BF16))
        act = (_silu(hg) * hu).astype(BF16)
        o_ref[...] += _dot(act, wd_ref[...].astype(BF16))


def _moe(h, src, tile_expert, n_used, w_gate, w_up, w_down, layer, tm=MOE_TM, tf=MOE_TF):
    d = 2 * h.shape[1]
    r = src.shape[0]
    n_tiles = r // tm
    nj = D_EXPERT // tf

    def jj(t, j, nu):
        return jnp.where(t < nu[0], j, nj - 1)

    grid_spec = pltpu.PrefetchScalarGridSpec(
        num_scalar_prefetch=3,
        grid=(n_tiles, nj),
        in_specs=[pl.BlockSpec(memory_space=pl.ANY),
                  pl.BlockSpec((None, None, d, tf), lambda t, j, te, nu, sr: (layer, te[t], 0, jj(t, j, nu))),
                  pl.BlockSpec((None, None, d, tf), lambda t, j, te, nu, sr: (layer, te[t], 0, jj(t, j, nu))),
                  pl.BlockSpec((None, None, tf, d), lambda t, j, te, nu, sr: (layer, te[t], jj(t, j, nu), 0))],
        out_specs=pl.BlockSpec((tm, d), lambda t, j, te, nu, sr: (t, 0)),
        scratch_shapes=[pltpu.VMEM((2, tm, d // 2), jnp.uint32), pltpu.VMEM((tm, d), BF16),
                        pltpu.SemaphoreType.DMA((2,))],
    )
    return pl.pallas_call(
        functools.partial(_moe_kernel, tm=tm),
        grid_spec=grid_spec,
        out_shape=jax.ShapeDtypeStruct((r, d), F32),
        compiler_params=_cp("arbitrary", "arbitrary"),
        name="moe_experts",
    )(tile_expert, n_used, src, h, w_gate, w_up, w_down)


def _combine_kernel(*refs, tm, n, fuse_norm):
    it = iter(refs)
    dest_ref, x_ref, y_hbm, w_ref, ga_ref = (next(it) for _ in range(5))
    g_ref, sc_ref, sh_ref = (next(it), next(it), next(it)) if fuse_norm else (None, None, None)
    o_ref = next(it)
    h_ref = next(it) if fuse_norm else None
    ybuf, sem = next(it), next(it)
    i = pl.program_id(0)
    nt = pl.num_programs(0)

    def row_copy(tile, slot, s, k):
        r = dest_ref[s * n + tile * tm + k]
        return pltpu.make_async_copy(y_hbm.at[pl.ds(r, 1), :], ybuf.at[slot, s, pl.ds(k, 1), :], sem.at[slot])

    def issue(tile, slot):
        def body(k, c):
            row_copy(tile, slot, 0, k).start()
            row_copy(tile, slot, 1, k).start()
            return c
        lax.fori_loop(0, tm, body, 0, unroll=8)

    def drain(tile, slot):
        def body(k, c):
            row_copy(tile, slot, 0, k).wait()
            row_copy(tile, slot, 1, k).wait()
            return c
        lax.fori_loop(0, tm, body, 0, unroll=8)

    @pl.when(i == 0)
    def _():
        issue(0, 0)

    @pl.when(i + 1 < nt)
    def _():
        issue(i + 1, (i + 1) % 2)

    slot = i % 2
    drain(i, slot)
    w = w_ref[...]
    y = w[:, 0:1] * ybuf[slot, 0] + w[:, 1:2] * ybuf[slot, 1]
    xn = x_ref[...] + ga_ref[...] * y
    o_ref[...] = xn
    if fuse_norm:
        h = xn * lax.rsqrt(jnp.mean(xn * xn, axis=-1, keepdims=True) + EPS) * g_ref[...]
        h_ref[...] = (h * (1.0 + sc_ref[...]) + sh_ref[...]).astype(h_ref.dtype)


def _combine(x, y, dest, w2, mod_l, ga_chunk, next_norm=None, tm=256):
    n, d = x.shape
    row = pl.BlockSpec((tm, d), lambda i, dst: (i, 0))
    in_specs = [row, pl.BlockSpec(memory_space=pl.ANY), pl.BlockSpec((tm, 2), lambda i, dst: (i, 0)),
                pl.BlockSpec((None, 1, d), lambda i, dst: (_cond_row(i, tm), 0, ga_chunk))]
    args = [dest, x, y, w2, mod_l]
    out_specs = [row]
    out_shape = [jax.ShapeDtypeStruct((n, d), F32)]
    if next_norm is not None:
        g, mod_n, (sh_c, sc_c) = next_norm
        in_specs += [pl.BlockSpec((1, d), lambda i, dst: (0, 0)),
                     pl.BlockSpec((None, 1, d), lambda i, dst: (_cond_row(i, tm), 0, sc_c)),
                     pl.BlockSpec((None, 1, d), lambda i, dst: (_cond_row(i, tm), 0, sh_c))]
        args += [g.reshape(1, d), mod_n, mod_n]
        out_specs.append(row)
        out_shape.append(jax.ShapeDtypeStruct((n, d), BF16))
    grid_spec = pltpu.PrefetchScalarGridSpec(
        num_scalar_prefetch=1,
        grid=(n // tm,),
        in_specs=in_specs,
        out_specs=out_specs,
        scratch_shapes=[pltpu.VMEM((2, 2, tm, d), F32), pltpu.SemaphoreType.DMA((2,))],
    )
    res = pl.pallas_call(
        functools.partial(_combine_kernel, tm=tm, n=n, fuse_norm=next_norm is not None),
        grid_spec=grid_spec,
        out_shape=out_shape,
        compiler_params=_cp("arbitrary"),
        name="combine",
    )(*args)
    return (res[0], res[1]) if next_norm is not None else (res[0], None)


def _layer(x, h, l, mod, p, lbs, w_router, router_bias, states, new_states):
    n = x.shape[0]
    mod_l = mod[l]
    state_hgrn, state_rglru, state_ssd = states
    if h is None:
        h = _normmod(x, p["norm1_g"][l], mod_l, chunk=(0, 1))
    proj = _mm(h, p["w_in"], l, 0, N_MAIN)
    dtp = _mm(h, p["w_dt"], l, 0, 512, tn=512)
    gates = _gates(h, p["w_in"], l, COL_MERGE, 3 * D_MODEL)
    nb = n // TOK_BLOCK
    ns = nb - N_CTX_BLOCKS
    hg_prev, ssd_prev = new_states
    oa, hg_new = _hgrn(proj, lbs[l], p["hgrn_norm_g"][l], 0, N_CTX_BLOCKS, SEQ, layer=l, st_prev=hg_prev)
    oa, _ = _hgrn(proj, lbs[l], p["hgrn_norm_g"][l], N_CTX_BLOCKS, ns, TOK_BLOCK, state_hgrn, l, prev=oa)
    ob, lru_new = _lru(proj, p, l, 0, N_CTX_BLOCKS, SEQ)
    ob, _ = _lru(proj, p, l, N_CTX_BLOCKS, ns, GRID_W, state_rglru, prev=ob)
    oc, ssd_new = _ssd(proj, dtp, p, l, 0, N_CTX_BLOCKS, SEQ, SEQ, st_prev=ssd_prev)
    oc, _ = _ssd(proj, dtp, p, l, N_CTX_BLOCKS, ns, TOK_BLOCK, GRID_W, state_ssd, prev=oc)
    merged = _merge(oa, ob, oc, p["w_branch_a"], p["w_branch_b"], p["w_branch_c"], gates, l)
    x = _mm_residual(merged, p["w_out"], l, x, mod_l, 2)
    h2, logits = _normmod(x, p["norm2_g"][l], mod_l, chunk=(3, 4), w_router=w_router, out_dtype=jnp.uint32)
    e2, w2 = _route(logits.T, router_bias)
    n_tiles = (2 * n) // MOE_TM + N_EXPERTS
    dest, src, tile_expert, n_used = _dispatch_plan(e2, MOE_TM, n_tiles)
    y = _moe(h2, src, tile_expert, n_used, p["w_e_gate"], p["w_e_up"], p["w_e_down"], l)
    next_norm = (p["norm1_g"][l + 1], mod[l + 1], (0, 1)) if l + 1 < DEPTH else None
    x, h_next = _combine(x, y, dest, w2.T, mod_l, 5, next_norm)
    return x, h_next, hg_new, lru_new.transpose(1, 0, 2), ssd_new


def _hgrn_lower_bounds(lb_raw):
    pr = jax.nn.softmax(lb_raw.astype(F32), axis=0)
    cum = jnp.cumsum(pr, axis=0)
    return cum - cum[0]


def kernel(x_prompt, x_sample, state_hgrn, state_rglru, state_ssd, c, c_ctx, w_mod, b_mod, norm1_g, norm2_g, w_in, hgrn_lb, hgrn_norm_g, conv_b_w, conv_b_b, lru_wa, lru_ba, lru_wx, lru_bx, lru_lambda, conv_c_w, conv_c_b, ssd_a_log, ssd_dt_bias, ssd_d, ssd_norm_g, w_branch_a, w_branch_b, w_branch_c, w_out, w_router, router_bias, w_e_gate, w_e_up, w_e_down, final_g):
    bsz, seq, d = x_prompt.shape
    dbsz, dseq, _ = x_sample.shape
    assert seq == SEQ and dseq == TOK_BLOCK and (bsz * seq) == N_CTX_BLOCKS * TOK_BLOCK and d == D_MODEL
    x = jnp.concatenate([x_prompt.reshape(-1, d), x_sample.reshape(-1, d)], axis=0)

    cond8 = jnp.zeros((8, d), F32).at[0].set(c_ctx).at[1:1 + dbsz].set(c)
    mod = _modulation(cond8, w_mod, b_mod).reshape(DEPTH, 8, 1, N_MOD * d)

    hpg = C_HEADS // C_GROUPS
    w_dt_raw = w_in[:, :, COL_DT:COL_DT + 2 * C_HEADS].reshape(DEPTH, d, 2, C_GROUPS, hpg)
    w_dt = jnp.zeros((DEPTH, d, C_GROUPS, 128), F32).at[:, :, :, :2 * hpg].set(
        w_dt_raw.transpose(0, 1, 3, 2, 4).reshape(DEPTH, d, C_GROUPS, 2 * hpg)).reshape(DEPTH, d, C_GROUPS * 128)

    def group_lanes(a):
        g = a.reshape(DEPTH, 2, C_GROUPS, hpg).transpose(0, 2, 1, 3).reshape(DEPTH, C_GROUPS, 1, 2 * hpg)
        return jnp.zeros((DEPTH, C_GROUPS, 1, 128), F32).at[..., :2 * hpg].set(g)

    p = dict(norm1_g=norm1_g, norm2_g=norm2_g, w_in=w_in, w_dt=w_dt,
             hgrn_norm_g=hgrn_norm_g,
             conv_b_w=conv_b_w, conv_b_b=conv_b_b, lru_wa=lru_wa, lru_ba=lru_ba, lru_wx=lru_wx,
             lru_bx=lru_bx, lru_lambda=lru_lambda, conv_c_w=conv_c_w, conv_c_b=conv_c_b,
             dt_bias_g=group_lanes(ssd_dt_bias), a_log_g=group_lanes(ssd_a_log),
             ssd_d_rep=jnp.repeat(ssd_d, C_HEADDIM, axis=-1).reshape(DEPTH, 1, C_INNER),
             ssd_norm_g=ssd_norm_g, w_branch_a=w_branch_a, w_branch_b=w_branch_b, w_branch_c=w_branch_c,
             w_out=w_out, w_e_gate=w_e_gate, w_e_up=w_e_up, w_e_down=w_e_down)
    lbs = _hgrn_lower_bounds(hgrn_lb)

    h = hg_new = ssd_new = None
    lru_list = []
    for l in range(DEPTH):
        x, h, hg_new, lru_new, ssd_new = _layer(x, h, l, mod, p, lbs, w_router, router_bias,
                                                (state_hgrn, state_rglru, state_ssd), (hg_new, ssd_new))
        lru_list.append(lru_new)
    n_ctx = bsz * seq
    y_prompt = _normmod(x, final_g, out_dtype=F32, row0=0, n_rows=n_ctx).reshape(bsz, seq, d)
    y_sample = _normmod(x, final_g, out_dtype=F32, row0=n_ctx, n_rows=dbsz * dseq).reshape(dbsz, dseq, d)
    return (y_prompt, y_sample, hg_new, jnp.stack(lru_list, axis=1), ssd_new)
```

```python
import functools
import math

import numpy as np
import jax
import jax.numpy as jnp
from jax import lax
from jax.experimental import pallas as pl
from jax.experimental.pallas import tpu as pltpu

F32 = jnp.float32
BF16 = jnp.bfloat16
HIGHEST = lax.Precision.HIGHEST

EPS = 1e-6
D_MODEL = 2048
DEPTH = 4
N_MOD = 6
GRID_W = 64
SEQ = 256
TOK_BLOCK = 2048
N_CTX_BLOCKS = 2

A_HEADS = 8
A_DK = 128
A_DV = 128
A_WIDTH = 1024
HGRN_C = 128

B_WIDTH = 1024
B_BLOCK = 128
LRU_C = 8.0
LRU_SEG = 256

C_INNER = 1024
C_HEADDIM = 64
C_HEADS = 16
C_GROUPS = 4
C_STATE = 128
SSD_L = 128
CONV_W = 4

N_EXPERTS = 16
N_EXPERT_GROUPS = 4
D_EXPERT = 1024
MOE_TM = 1024
MOE_TF = 256

COL_Q, COL_FF, COL_FB, COL_V, COL_GA = 0, 1024, 2048, 3072, 4096
COL_XB, COL_GB, COL_Z, COL_XBC, COL_DT, COL_MERGE = 5120, 6144, 7168, 8192, 10240, 10272
N_MAIN = 10240

VMEM_LIMIT_BYTES = 56 * 1024 * 1024
NEG_BIG = -1e30


def _cp(*sem):
    return pltpu.CompilerParams(dimension_semantics=sem, vmem_limit_bytes=VMEM_LIMIT_BYTES)


def _silu(x):
    return x * jax.nn.sigmoid(x)


def _dot(a, b):
    return jnp.dot(a, b, preferred_element_type=F32)


def _dot_nt(a, b):
    return lax.dot_general(a, b, (((1,), (1,)), ((), ())), preferred_element_type=F32)


def _dot_tn(a, b):
    return lax.dot_general(a, b, (((0,), (0,)), ((), ())), preferred_element_type=F32)


def _dot_hi(a, b):
    return jnp.dot(a, b, preferred_element_type=F32, precision=HIGHEST)


def _cond_row(i, tm):
    return jnp.maximum((i * tm) // TOK_BLOCK - (N_CTX_BLOCKS - 1), 0)


def _mod_kernel(c_ref, w_ref, b_ref, o_ref):
    s = _silu(c_ref[...]).astype(BF16)
    o_ref[...] = _dot(s, w_ref[...].astype(BF16)) + b_ref[...]


def _modulation(cond8, w_mod, b_mod):
    depth, d, n = w_mod.shape
    tn = 1024
    return pl.pallas_call(
        _mod_kernel,
        grid=(depth, n // tn),
        in_specs=[pl.BlockSpec((8, d), lambda l, j: (0, 0)),
                  pl.BlockSpec((None, d, tn), lambda l, j: (l, 0, j)),
                  pl.BlockSpec((None, 1, tn), lambda l, j: (l, 0, j))],
        out_specs=pl.BlockSpec((None, 8, tn), lambda l, j: (l, 0, j)),
        out_shape=jax.ShapeDtypeStruct((depth, 8, n), F32),
        compiler_params=_cp("arbitrary", "arbitrary"),
        name="modulation",
    )(cond8, w_mod, b_mod.reshape(depth, 1, n))


def _normmod_kernel(*refs, modulate, router):
    it = iter(refs)
    x_ref, g_ref = next(it), next(it)
    sc_ref = sh_ref = wr_ref = lg_ref = None
    if modulate:
        sc_ref, sh_ref = next(it), next(it)
    if router:
        wr_ref = next(it)
    h_ref = next(it)
    if router:
        lg_ref = next(it)
    x = x_ref[...]
    h = x * lax.rsqrt(jnp.mean(x * x, axis=-1, keepdims=True) + EPS) * g_ref[...]
    if modulate:
        h = h * (1.0 + sc_ref[...]) + sh_ref[...]
    h_ref[...] = h.astype(h_ref.dtype)
    if router:
        lg_ref[...] = _dot_hi(h, wr_ref[...])


def _normmod(x, g, mod_l=None, chunk=None, w_router=None, out_dtype=BF16, tm=512, row0=0, n_rows=None):
    d = x.shape[1]
    n = x.shape[0] if n_rows is None else n_rows
    r0 = row0 // tm
    modulate = mod_l is not None
    router = w_router is not None
    assert not (modulate and row0)
    in_specs = [pl.BlockSpec((tm, d), lambda i: (i + r0, 0)),
                pl.BlockSpec((1, d), lambda i: (0, 0))]
    args = [x, g.reshape(1, d)]
    if modulate:
        sh_c, sc_c = chunk
        in_specs += [pl.BlockSpec((None, 1, d), lambda i: (_cond_row(i, tm), 0, sc_c)),
                     pl.BlockSpec((None, 1, d), lambda i: (_cond_row(i, tm), 0, sh_c))]
        args += [mod_l, mod_l]
    out_specs = [pl.BlockSpec((tm, d), lambda i: (i, 0))]
    out_shape = [jax.ShapeDtypeStruct((n, d), out_dtype)]
    if router:
        ne = w_router.shape[1]
        in_specs.append(pl.BlockSpec((d, ne), lambda i: (0, 0)))
        args.append(w_router)
        out_specs.append(pl.BlockSpec((tm, ne), lambda i: (i, 0)))
        out_shape.append(jax.ShapeDtypeStruct((n, ne), F32))
    res = pl.pallas_call(
        functools.partial(_normmod_kernel, modulate=modulate, router=router),
        grid=(n // tm,),
        in_specs=in_specs, out_specs=out_specs, out_shape=out_shape,
        compiler_params=_cp("arbitrary"),
        name="normmod",
    )(*args)
    return res if router else res[0]


def _mm_kernel(a_ref, w_ref, o_ref):
    o_ref[...] = _dot(a_ref[...], w_ref[...].astype(BF16))


def _mm(a, w, layer, col0, n_out, tm=2048, tn=512):
    m, k = a.shape
    tn = min(tn, n_out)
    off = col0 // tn
    assert col0 % tn == 0 and n_out % tn == 0 and m % tm == 0
    return pl.pallas_call(
        _mm_kernel,
        grid=(m // tm, n_out // tn),
        in_specs=[pl.BlockSpec((tm, k), lambda i, j: (i, 0)),
                  pl.BlockSpec((None, k, tn), lambda i, j: (layer, 0, j + off))],
        out_specs=pl.BlockSpec((tm, tn), lambda i, j: (i, j)),
        out_shape=jax.ShapeDtypeStruct((m, n_out), F32),
        compiler_params=_cp("arbitrary", "arbitrary"),
        name="matmul",
    )(a, w)


def _gates_kernel(a_ref, w_ref, o_ref):
    o_ref[...] = jax.nn.sigmoid(_dot(a_ref[...], w_ref[...].astype(BF16))).astype(o_ref.dtype)


def _gates(a, w, layer, tm=2048, tn=512):
    m, k = a.shape
    n_out = w.shape[-1]
    return pl.pallas_call(
        _gates_kernel,
        grid=(m // tm, n_out // tn),
        in_specs=[pl.BlockSpec((tm, k), lambda i, j: (i, 0)),
                  pl.BlockSpec((None, k, tn), lambda i, j: (layer, 0, j))],
        out_specs=pl.BlockSpec((tm, tn), lambda i, j: (i, j)),
        out_shape=jax.ShapeDtypeStruct((m, n_out), BF16),
        compiler_params=_cp("arbitrary", "arbitrary"),
        name="merge_gates",
    )(a, w)


def _mm_res_kernel(a_ref, w_ref, x_ref, ga_ref, o_ref):
    acc = _dot(a_ref[...], w_ref[...].astype(BF16))
    o_ref[...] = x_ref[...] + ga_ref[...] * acc


def _mm_residual(a, w, layer, x, mod_l, ga_chunk, tm=1024, tn=512):
    m, k = a.shape
    n = w.shape[-1]
    gs = n // tn
    return pl.pallas_call(
        _mm_res_kernel,
        grid=(m // tm, n // tn),
        in_specs=[pl.BlockSpec((tm, k), lambda i, j: (i, 0)),
                  pl.BlockSpec((None, k, tn), lambda i, j: (layer, 0, j)),
                  pl.BlockSpec((tm, tn), lambda i, j: (i, j)),
                  pl.BlockSpec((None, 1, tn), lambda i, j: (_cond_row(i, tm), 0, ga_chunk * gs + j))],
        out_specs=pl.BlockSpec((tm, tn), lambda i, j: (i, j)),
        out_shape=jax.ShapeDtypeStruct((m, n), F32),
        compiler_params=_cp("arbitrary", "arbitrary"),
        name="out_proj",
    )(a, w, x, mod_l)


def _merge_kernel(oa_ref, ob_ref, oc_ref, wa_ref, wb_ref, wc_ref, g1_ref, g2_ref, g3_ref, o_ref):
    ya = _dot(oa_ref[...], wa_ref[...].astype(BF16))
    yb = _dot(ob_ref[...], wb_ref[...].astype(BF16))
    yc = _dot(oc_ref[...], wc_ref[...].astype(BF16))
    m = (g1_ref[...].astype(F32) * ya + g2_ref[...].astype(F32) * yb + g3_ref[...].astype(F32) * yc)
    o_ref[...] = m.astype(o_ref.dtype)


def _merge(oa, ob, oc, wa, wb, wc, gates, layer, tm=1024, tn=512):
    m, k = oa.shape
    n = wa.shape[-1]
    gs = n // tn
    a_spec = pl.BlockSpec((tm, k), lambda i, j: (i, 0))
    w_spec = pl.BlockSpec((None, k, tn), lambda i, j: (layer, 0, j))
    return pl.pallas_call(
        _merge_kernel,
        grid=(m // tm, n // tn),
        in_specs=[a_spec, a_spec, a_spec, w_spec, w_spec, w_spec,
                  pl.BlockSpec((tm, tn), lambda i, j: (i, j)),
                  pl.BlockSpec((tm, tn), lambda i, j: (i, j + gs)),
                  pl.BlockSpec((tm, tn), lambda i, j: (i, j + 2 * gs))],
        out_specs=pl.BlockSpec((tm, tn), lambda i, j: (i, j)),
        out_shape=jax.ShapeDtypeStruct((m, n), BF16),
        compiler_params=_cp("arbitrary", "arbitrary"),
        name="merge",
    )(oa, ob, oc, wa, wb, wc, gates, gates, gates)


def _dwconv(x, w, b, row_len):
    t = x.shape[0]
    pos = lax.broadcasted_iota(jnp.int32, x.shape, 0) % row_len
    xm2 = jnp.where(pos >= 2, pltpu.roll(x, 2, axis=0), 0.0)
    xm1 = jnp.where(pos >= 1, pltpu.roll(x, 1, axis=0), 0.0)
    xp1 = jnp.where(pos <= row_len - 2, pltpu.roll(x, t - 1, axis=0), 0.0)
    y = b + xm2 * w[0:1]
    y = y + xm1 * w[1:2]
    y = y + x * w[2:3]
    y = y + xp1 * w[3:4]
    return y


def _hgrn_consts(c):
    nl = int(math.log2(c))
    seg = np.zeros((2, nl + 1, c, c), np.float32)
    msk = np.zeros((2, nl + 1, c, c), np.float32)
    up = np.zeros((2, nl, c, 128), np.float32)
    for d in range(2):
        tt = np.arange(c) if d == 0 else c - 1 - np.arange(c)
        tr, tc = tt[:, None], tt[None, :]
        seg[d, 0] = tc <= tr
        for l in range(nl):
            s = 1 << l
            blk, upper = tt // (2 * s), (tt % (2 * s)) >= s
            mid = (blk * 2 * s + s)[:, None]
            seg_u = (tc >= mid) & (tc <= tr)
            seg_l = (tc >= tr + 1) & (tc <= mid - 1)
            seg[d, l + 1] = np.where(upper[:, None], seg_u, seg_l)
            msk[d, l] = upper[:, None] & (~upper)[None, :] & (blk[:, None] == blk[None, :])
            up[d, l] = upper[:, None]
        msk[d, nl] = np.eye(c)
    return seg.reshape(2, (nl + 1) * c, c), msk, up


def _hgrn_kernel(*refs, seq_len, chained, aliased):
    it = iter(refs)
    q_ref, ff_ref, fb_ref, v_ref, ga_ref, lb_ref, ng_ref = (next(it) for _ in range(7))
    seg_ref, msk_ref, up_ref = next(it), next(it), next(it)
    s0_ref = next(it) if chained else None
    for _ in range(aliased):
        next(it)
    o_ref = next(it)
    sout_ref = None if chained else next(it)
    o_scr, qp_scr, u_scr, dec_scr, snap_scr = (next(it) for _ in range(5))

    c = HGRN_C
    nl = int(math.log2(c))
    tb = q_ref.shape[0]
    n_chunks = tb // c
    n_ch = seq_len // c
    lb = lb_ref[...]
    f_refs = (ff_ref, fb_ref)

    def phase_a(n, carry):
        rows = pl.ds(pl.multiple_of(n * c, c), c)
        q = _silu(q_ref[rows, :]) * (A_DK ** -0.5)
        qb = q.astype(BF16)
        v = v_ref[rows, :].astype(BF16)
        o = jnp.zeros((c, A_DV), F32)
        qps, kps, decs = [], [], []
        for d in range(2):
            f = lb + (1.0 - lb) * jax.nn.sigmoid(f_refs[d][rows, :])
            k = 1.0 - f
            lf = jnp.log(f)
            hi = lf.astype(BF16)
            mid = (lf - hi.astype(F32)).astype(BF16)
            e2 = _dot(seg_ref[d], jnp.concatenate([hi, mid], axis=1))
            e = e2[:, :A_DK] + e2[:, A_DK:]
            b = e[0:c]
            att = msk_ref[d, nl] * _dot_nt(qb, k.astype(BF16))
            for l in range(nl):
                x = (jnp.where(up_ref[d, l] > 0.5, q, k) * jnp.exp(e[(l + 1) * c:(l + 2) * c])).astype(BF16)
                att = att + msk_ref[d, l] * _dot_nt(x, x)
            o = o + _dot(att.astype(BF16), v)
            b_end = b[c - 1:c] if d == 0 else b[0:1]
            qps.append((q * jnp.exp(b)).astype(BF16))
            kps.append((k * jnp.exp(b_end - b)).astype(BF16))
            decs.append(jnp.exp(b_end))
        o_scr[rows, :] = o
        qp_scr[rows, :] = jnp.concatenate(qps, axis=1)
        u_scr[n] = _dot_tn(v, jnp.concatenate(kps, axis=1))
        dec_scr[n] = jnp.concatenate(decs, axis=1)
        return carry

    lax.fori_loop(0, n_chunks, phase_a, 0)

    def init_state(d):
        return s0_ref[d].T if chained else jnp.zeros((A_DV, A_DK), F32)

    def phase_b1(j, carry):
        sf, sb = carry
        nf = j
        nb = n_chunks - 1 - j
        if not chained:
            sf = jnp.where(nf % n_ch == 0, 0.0, sf)
            sb = jnp.where(nb % n_ch == n_ch - 1, 0.0, sb)
        snap_scr[nf, :, 0:A_DK] = sf.astype(BF16)
        snap_scr[nb, :, A_DK:2 * A_DK] = sb.astype(BF16)
        sf = sf * dec_scr[nf][:, 0:A_DK] + u_scr[nf][:, 0:A_DK]
        sb = sb * dec_scr[nb][:, A_DK:2 * A_DK] + u_scr[nb][:, A_DK:2 * A_DK]
        if not chained:
            @pl.when(nf % n_ch == n_ch - 1)
            def _():
                sout_ref[nf // n_ch, 0] = sf.T

            @pl.when(nb % n_ch == 0)
            def _():
                sout_ref[nb // n_ch, 1] = sb.T
        return sf, sb

    lax.fori_loop(0, n_chunks, phase_b1, (init_state(0), init_state(1)))

    ng = ng_ref[...]

    def phase_b2(n, carry):
        rows = pl.ds(pl.multiple_of(n * c, c), c)
        o = o_scr[rows, :] + _dot_nt(qp_scr[rows, :], snap_scr[n])
        o = o * lax.rsqrt(jnp.mean(o * o, axis=-1, keepdims=True) + EPS) * ng
        o_ref[rows, :] = (o * _silu(ga_ref[rows, :])).astype(o_ref.dtype)
        return carry

    lax.fori_loop(0, n_chunks, phase_b2, 0, unroll=4)


def _alias_out(prevs, in_specs, args):
    aliases = {}
    for out_idx, prev in prevs:
        if prev is not None:
            in_specs.append(pl.BlockSpec(memory_space=pl.ANY))
            args.append(prev)
            aliases[len(args) - 1] = out_idx
    return aliases


def _hgrn(proj, lb_l, ng_l, blk0, n_blk, seq_len, state=None, layer=0, prev=None, st_prev=None):
    tb = TOK_BLOCK
    chained = state is not None
    seg, msk, up = _hgrn_consts(HGRN_C)
    cb = lambda col: col // A_DK

    def col_spec(col):
        return pl.BlockSpec((tb, A_DK), lambda b, h: (b + blk0, cb(col) + h))

    def full(a):
        nd = a.ndim
        return pl.BlockSpec(a.shape, lambda b, h: (0,) * nd)

    in_specs = [col_spec(COL_Q), col_spec(COL_FF), col_spec(COL_FB), col_spec(COL_V), col_spec(COL_GA),
                pl.BlockSpec((1, A_DK), lambda b, h: (0, h)),
                pl.BlockSpec((1, A_DV), lambda b, h: (0, 0)),
                full(seg), full(msk), full(up)]
    args = [proj, proj, proj, proj, proj, lb_l.reshape(1, -1), ng_l.reshape(1, -1),
            jnp.asarray(seg, BF16), jnp.asarray(msk), jnp.asarray(up)]
    out_specs = [pl.BlockSpec((tb, A_DV), lambda b, h: (b + blk0, h))]
    out_shape = [jax.ShapeDtypeStruct((proj.shape[0], A_WIDTH), BF16)]
    if chained:
        in_specs.append(pl.BlockSpec((None, None, 2, None, A_DK, A_DV), lambda b, h: (b, layer, 0, h, 0, 0)))
        args.append(state)
    else:
        n_seq = tb // seq_len
        out_specs.append(pl.BlockSpec((n_seq, None, 2, None, A_DK, A_DV), lambda b, h: (b, layer, 0, h, 0, 0)))
        out_shape.append(jax.ShapeDtypeStruct((n_blk * n_seq, DEPTH, 2, A_HEADS, A_DK, A_DV), F32))
    aliases = _alias_out([(0, prev), (1, st_prev)], in_specs, args)
    n_chunks = tb // HGRN_C
    res = pl.pallas_call(
        functools.partial(_hgrn_kernel, seq_len=seq_len, chained=chained, aliased=len(aliases)),
        grid=(n_blk, A_HEADS),
        in_specs=in_specs, out_specs=out_specs, out_shape=out_shape,
        input_output_aliases=aliases,
        scratch_shapes=[pltpu.VMEM((tb, A_DV), F32), pltpu.VMEM((tb, 2 * A_DK), BF16),
                        pltpu.VMEM((n_chunks, A_DV, 2 * A_DK), F32), pltpu.VMEM((n_chunks, 1, 2 * A_DK), F32),
                        pltpu.VMEM((n_chunks, A_DV, 2 * A_DK), BF16)],
        compiler_params=_cp("arbitrary", "arbitrary"),
        name="hgrn2",
    )(*args)
    return res[0], (None if chained else res[1])


def _lru_kernel(*refs, row_len, chained, aliased):
    it = iter(refs)
    x_ref, gb_ref, cw_ref, cb_ref, wa_ref, ba_ref, wx_ref, bx_ref, lam_ref = (next(it) for _ in range(9))
    h0_ref = next(it) if chained else None
    for _ in range(aliased):
        next(it)
    o_ref = next(it)
    hout_ref = None if chained else next(it)
    a_scr, u_scr, h_scr, p_scr = (next(it) for _ in range(4))

    tb = x_ref.shape[0]
    seg = LRU_SEG
    n_seg = tb // seg
    xc = _dwconv(x_ref[...], cw_ref[...], cb_ref[...], row_len)
    xcb = xc.astype(BF16)
    for d in range(2):
        r = jax.nn.sigmoid(_dot(xcb, wa_ref[d].astype(BF16)) + ba_ref[d])
        g = jax.nn.sigmoid(_dot(xcb, wx_ref[d].astype(BF16)) + bx_ref[d])
        log_a = LRU_C * r * jax.nn.log_sigmoid(lam_ref[d])
        a = jnp.exp(log_a)
        a_scr[d] = a
        u_scr[d] = jnp.sqrt(1.0 - a * a) * (g * xc)

    def step(i, carry):
        hf, pf, hb, pb = carry
        tf = i
        tr = seg - 1 - i
        af = a_scr[0, pl.ds(tf, n_seg, stride=seg), :]
        hf = af * hf + u_scr[0, pl.ds(tf, n_seg, stride=seg), :]
        h_scr[0, pl.ds(tf, n_seg, stride=seg), :] = hf
        ab = a_scr[1, pl.ds(tr, n_seg, stride=seg), :]
        hb = ab * hb + u_scr[1, pl.ds(tr, n_seg, stride=seg), :]
        h_scr[1, pl.ds(tr, n_seg, stride=seg), :] = hb
        if chained:
            pf = af * pf
            pb = ab * pb
            p_scr[0, pl.ds(tf, n_seg, stride=seg), :] = pf
            p_scr[1, pl.ds(tr, n_seg, stride=seg), :] = pb
        return hf, pf, hb, pb

    zeros = jnp.zeros((n_seg, B_BLOCK), F32)
    ones = jnp.ones((n_seg, B_BLOCK), F32)
    hf, pf, hb, pb = lax.fori_loop(0, seg, step, (zeros, ones, zeros, ones), unroll=8)

    if chained:
        hin = h0_ref[0:1, :]
        for s in range(n_seg):
            rows = pl.ds(s * seg, seg)
            h_scr[0, rows, :] = h_scr[0, rows, :] + p_scr[0, rows, :] * hin
            hin = hf[s:s + 1] + pf[s:s + 1] * hin
        hin = h0_ref[1:2, :]
        for s in range(n_seg - 1, -1, -1):
            rows = pl.ds(s * seg, seg)
            h_scr[1, rows, :] = h_scr[1, rows, :] + p_scr[1, rows, :] * hin
            hin = hb[s:s + 1] + pb[s:s + 1] * hin
    else:
        hout_ref[0] = hf
        hout_ref[1] = hb
    o_ref[...] = ((h_scr[0] + h_scr[1]) * jax.nn.gelu(gb_ref[...])).astype(o_ref.dtype)


def _lru(proj, p, layer, blk0, n_blk, row_len, state=None, prev=None):
    tb = TOK_BLOCK
    chained = state is not None
    nb = B_WIDTH // B_BLOCK
    cbx, cbg = COL_XB // B_BLOCK, COL_GB // B_BLOCK
    in_specs = [pl.BlockSpec((tb, B_BLOCK), lambda b, n: (b + blk0, cbx + n)),
                pl.BlockSpec((tb, B_BLOCK), lambda b, n: (b + blk0, cbg + n)),
                pl.BlockSpec((None, CONV_W, B_BLOCK), lambda b, n: (layer, 0, n)),
                pl.BlockSpec((None, 1, B_BLOCK), lambda b, n: (layer, 0, n)),
                pl.BlockSpec((None, 2, None, B_BLOCK, B_BLOCK), lambda b, n: (layer, 0, n, 0, 0)),
                pl.BlockSpec((None, 2, 1, B_BLOCK), lambda b, n: (layer, 0, 0, n)),
                pl.BlockSpec((None, 2, None, B_BLOCK, B_BLOCK), lambda b, n: (layer, 0, n, 0, 0)),
                pl.BlockSpec((None, 2, 1, B_BLOCK), lambda b, n: (layer, 0, 0, n)),
                pl.BlockSpec((None, 2, 1, B_BLOCK), lambda b, n: (layer, 0, 0, n))]
    d4 = lambda a: a.reshape(DEPTH, 2, 1, B_WIDTH)
    args = [proj, proj, p["conv_b_w"], p["conv_b_b"].reshape(DEPTH, 1, B_WIDTH),
            p["lru_wa"], d4(p["lru_ba"]), p["lru_wx"], d4(p["lru_bx"]), d4(p["lru_lambda"])]
    out_specs = [pl.BlockSpec((tb, B_BLOCK), lambda b, n: (b + blk0, n))]
    out_shape = [jax.ShapeDtypeStruct((proj.shape[0], B_WIDTH), BF16)]
    if chained:
        in_specs.append(pl.BlockSpec((None, None, 2, B_BLOCK), lambda b, n: (b, layer, 0, n)))
        args.append(state)
    else:
        n_seq = tb // LRU_SEG
        out_specs.append(pl.BlockSpec((2, n_seq, B_BLOCK), lambda b, n: (0, b, n)))
        out_shape.append(jax.ShapeDtypeStruct((2, n_blk * n_seq, B_WIDTH), F32))
    aliases = _alias_out([(0, prev)], in_specs, args)
    res = pl.pallas_call(
        functools.partial(_lru_kernel, row_len=row_len, chained=chained, aliased=len(aliases)),
        grid=(n_blk, nb),
        in_specs=in_specs, out_specs=out_specs, out_shape=out_shape,
        input_output_aliases=aliases,
        scratch_shapes=[pltpu.VMEM((2, tb, B_BLOCK), F32) for _ in range(4)],
        compiler_params=_cp("arbitrary", "arbitrary"),
        name="rglru",
    )(*args)
    return res[0], (None if chained else res[1])


def _ssd_kernel(*refs, seq_len, row_len, chained, aliased):
    it = iter(refs)
    (xs_ref, bm_ref, cm_ref, z_ref, dt_ref, cwx_ref, cwb_ref, cwc_ref, cbx_ref, cbb_ref, cbc_ref,
     dtb_ref, a_ref, dsk_ref, ng_ref, tri_ref, ecat_ref, hmask_ref) = (next(it) for _ in range(18))
    s0_ref = next(it) if chained else None
    for _ in range(aliased):
        next(it)
    o_ref = next(it)
    sout_ref = None if chained else next(it)
    (xs_scr, bm_scr, cm_scr, dt_scr, y_scr, dcy_scr, u_scr, dec_scr, snap_scr,
     st_scr) = (next(it) for _ in range(10))

    lc = SSD_L
    hpg = C_HEADS // C_GROUPS
    gw = hpg * C_HEADDIM
    tb = xs_ref.shape[0]
    n_chunks = tb // lc
    n_ch = seq_len // lc
    xs_scr[...] = _silu(_dwconv(xs_ref[...], cwx_ref[...], cbx_ref[...], row_len))
    bm_scr[...] = _silu(_dwconv(bm_ref[...], cwb_ref[...], cbb_ref[...], row_len)).astype(BF16)
    cm_scr[...] = _silu(_dwconv(cm_ref[...], cwc_ref[...], cbc_ref[...], row_len)).astype(BF16)
    dt_scr[...] = jax.nn.softplus(dt_ref[...] + dtb_ref[...])
    a_row = -jnp.exp(a_ref[...])
    rr = lax.broadcasted_iota(jnp.int32, (lc, lc), 0)
    cc = lax.broadcasted_iota(jnp.int32, (lc, lc), 1)
    causal = (rr >= cc, cc >= rr)
    rep0 = hpg * lc

    def split2(x):
        hi = x.astype(BF16)
        return hi, (x - hi.astype(F32)).astype(BF16)

    def phase_a(n, carry):
        rows = pl.ds(pl.multiple_of(n * lc, lc), lc)
        dtc = dt_scr[rows, :]
        hi, mid = split2(dtc * a_row)
        c2 = _dot(tri_ref[...], jnp.concatenate([hi, mid], axis=1))
        cum2 = c2[:, :128] + c2[:, 128:]
        dhi, dmid = split2(dtc)
        xs = xs_scr[rows, :]
        bm = bm_scr[rows, :]
        cm = cm_scr[rows, :]
        scores = _dot_nt(cm, bm)
        y = dsk_ref[...] * xs
        for d in range(2):
            cum = cum2[d * lc:(d + 1) * lc]
            chi, cmid = split2(cum)
            r4 = _dot(jnp.concatenate([chi, cmid, dhi, dmid], axis=0), ecat_ref[d])
            rep = r4[0:lc] + r4[lc:2 * lc]
            rep64 = rep[:, rep0:]
            dtrep = r4[2 * lc:3 * lc, rep0:] + r4[3 * lc:, rep0:]
            cum_t = cum.T
            ps = []
            for hh in range(hpg):
                ln = d * hpg + hh
                seg = jnp.exp(jnp.where(causal[d], rep[:, lc * hh:lc * (hh + 1)] - cum_t[ln:ln + 1, :], NEG_BIG))
                ps.append((scores * seg).astype(BF16))
            xdt = xs * dtrep
            rhs = jnp.concatenate([(xdt * hmask_ref[hh]).astype(BF16) for hh in range(hpg)], axis=0)
            y = y + _dot(jnp.concatenate(ps, axis=1), rhs)
            end = rep64[lc - 1:lc] if d == 0 else rep64[0:1]
            u_scr[d, n] = _dot_tn(bm, (xdt * jnp.exp(end - rep64)).astype(BF16))
            dec_scr[d, n] = jnp.exp(end)
            dcy_scr[d, rows, :] = jnp.exp(rep64)
        y_scr[rows, :] = y
        return carry

    lax.fori_loop(0, n_chunks, phase_a, 0)

    for d in range(2):
        if chained:
            st_scr[d] = jnp.concatenate([s0_ref[d, hh] for hh in range(hpg)], axis=0).T
        else:
            st_scr[d] = jnp.zeros((C_STATE, gw), F32)

    def phase_b1(j, carry):
        nf = j
        nb = n_chunks - 1 - j
        sf = st_scr[0]
        sb = st_scr[1]
        if not chained:
            sf = jnp.where(nf % n_ch == 0, 0.0, sf)
            sb = jnp.where(nb % n_ch == n_ch - 1, 0.0, sb)
        snap_scr[0, nf] = sf.astype(BF16)
        snap_scr[1, nb] = sb.astype(BF16)
        sf = sf * dec_scr[0, nf] + u_scr[0, nf]
        sb = sb * dec_scr[1, nb] + u_scr[1, nb]
        st_scr[0] = sf
        st_scr[1] = sb
        if not chained:
            @pl.when(nf % n_ch == n_ch - 1)
            def _():
                sft = sf.T
                for hh in range(hpg):
                    sout_ref[nf // n_ch, 0, hh] = sft[hh * C_HEADDIM:(hh + 1) * C_HEADDIM]

            @pl.when(nb % n_ch == 0)
            def _():
                sbt = sb.T
                for hh in range(hpg):
                    sout_ref[nb // n_ch, 1, hh] = sbt[hh * C_HEADDIM:(hh + 1) * C_HEADDIM]
        return carry

    lax.fori_loop(0, n_chunks, phase_b1, 0)

    ng = ng_ref[...]

    def phase_b2(n, carry):
        rows = pl.ds(pl.multiple_of(n * lc, lc), lc)
        cm = cm_scr[rows, :]
        y = y_scr[rows, :] + _dot(cm, snap_scr[0, n]) * dcy_scr[0, rows, :]
        y = y + _dot(cm, snap_scr[1, n]) * dcy_scr[1, rows, :]
        y = y * _silu(z_ref[rows, :])
        y = y * lax.rsqrt(jnp.mean(y * y, axis=-1, keepdims=True) + EPS) * ng
        o_ref[rows, :] = y.astype(o_ref.dtype)
        return carry

    lax.fori_loop(0, n_chunks, phase_b2, 0, unroll=2)


def _ssd(proj, dtp, p, layer, blk0, n_blk, seq_len, row_len, state=None, prev=None, st_prev=None):
    tb = TOK_BLOCK
    chained = state is not None
    hpg = C_HEADS // C_GROUPS
    gw = hpg * C_HEADDIM
    col_x, col_b, col_c = COL_XBC, COL_XBC + C_INNER, COL_XBC + C_INNER + C_GROUPS * C_STATE
    lc = SSD_L
    tri = np.concatenate([np.tril(np.ones((lc, lc), np.float32)), np.triu(np.ones((lc, lc), np.float32))])
    ecat = np.zeros((2, 128, hpg * lc + gw), np.float32)
    hmask = np.zeros((hpg, 1, gw), np.float32)
    for hh in range(hpg):
        hmask[hh, 0, hh * C_HEADDIM:(hh + 1) * C_HEADDIM] = 1.0
        for d in range(2):
            ecat[d, d * hpg + hh, hh * lc:(hh + 1) * lc] = 1.0
            ecat[d, d * hpg + hh, hpg * lc + hh * C_HEADDIM:hpg * lc + (hh + 1) * C_HEADDIM] = 1.0
    in_specs = [pl.BlockSpec((tb, gw), lambda b, g: (b + blk0, col_x // gw + g)),
                pl.BlockSpec((tb, C_STATE), lambda b, g: (b + blk0, col_b // C_STATE + g)),
                pl.BlockSpec((tb, C_STATE), lambda b, g: (b + blk0, col_c // C_STATE + g)),
                pl.BlockSpec((tb, gw), lambda b, g: (b + blk0, COL_Z // gw + g)),
                pl.BlockSpec((tb, 128), lambda b, g: (b + blk0, g)),
                pl.BlockSpec((None, CONV_W, gw), lambda b, g: (layer, 0, g)),
                pl.BlockSpec((None, CONV_W, C_STATE), lambda b, g: (layer, 0, C_INNER // C_STATE + g)),
                pl.BlockSpec((None, CONV_W, C_STATE), lambda b, g: (layer, 0, C_INNER // C_STATE + C_GROUPS + g)),
                pl.BlockSpec((None, 1, gw), lambda b, g: (layer, 0, g)),
                pl.BlockSpec((None, 1, C_STATE), lambda b, g: (layer, 0, C_INNER // C_STATE + g)),
                pl.BlockSpec((None, 1, C_STATE), lambda b, g: (layer, 0, C_INNER // C_STATE + C_GROUPS + g)),
                pl.BlockSpec((None, None, 1, 128), lambda b, g: (layer, g, 0, 0)),
                pl.BlockSpec((None, None, 1, 128), lambda b, g: (layer, g, 0, 0)),
                pl.BlockSpec((None, 1, gw), lambda b, g: (layer, 0, g)),
                pl.BlockSpec((None, 1, gw), lambda b, g: (layer, 0, g)),
                pl.BlockSpec(tri.shape, lambda b, g: (0, 0)),
                pl.BlockSpec(ecat.shape, lambda b, g: (0, 0, 0)),
                pl.BlockSpec(hmask.shape, lambda b, g: (0, 0, 0))]
    cw = p["conv_c_w"]
    cbias = p["conv_c_b"].reshape(DEPTH, 1, -1)
    args = [proj, proj, proj, proj, dtp, cw, cw, cw, cbias, cbias, cbias,
            p["dt_bias_g"], p["a_log_g"], p["ssd_d_rep"], p["ssd_norm_g"].reshape(DEPTH, 1, C_INNER),
            jnp.asarray(tri, BF16), jnp.asarray(ecat, BF16), jnp.asarray(hmask)]
    out_specs = [pl.BlockSpec((tb, gw), lambda b, g: (b + blk0, g))]
    out_shape = [jax.ShapeDtypeStruct((proj.shape[0], C_INNER), BF16)]
    if chained:
        in_specs.append(pl.BlockSpec((None, None, 2, hpg, C_HEADDIM, C_STATE),
                                     lambda b, g: (b, layer, 0, g, 0, 0)))
        args.append(state)
    else:
        n_seq = tb // seq_len
        out_specs.append(pl.BlockSpec((n_seq, None, 2, hpg, C_HEADDIM, C_STATE),
                                      lambda b, g: (b, layer, 0, g, 0, 0)))
        out_shape.append(jax.ShapeDtypeStruct((n_blk * n_seq, DEPTH, 2, C_HEADS, C_HEADDIM, C_STATE), F32))
    aliases = _alias_out([(0, prev), (1, st_prev)], in_specs, args)
    res = pl.pallas_call(
        functools.partial(_ssd_kernel, seq_len=seq_len, row_len=row_len, chained=chained,
                          aliased=len(aliases)),
        grid=(n_blk, C_GROUPS),
        in_specs=in_specs, out_specs=out_specs, out_shape=out_shape,
        input_output_aliases=aliases,
        scratch_shapes=[pltpu.VMEM((tb, gw), F32), pltpu.VMEM((tb, C_STATE), BF16), pltpu.VMEM((tb, C_STATE), BF16),
                        pltpu.VMEM((tb, 128), F32), pltpu.VMEM((tb, gw), F32), pltpu.VMEM((2, tb, gw), F32),
                        pltpu.VMEM((2, tb // lc, C_STATE, gw), F32), pltpu.VMEM((2, tb // lc, 1, gw), F32),
                        pltpu.VMEM((2, tb // lc, C_STATE, gw), BF16), pltpu.VMEM((2, C_STATE, gw), F32)],
        compiler_params=_cp("arbitrary", "arbitrary"),
        name="ssd",
    )(*args)
    return res[0], (None if chained else res[1])


def _route_kernel(lg_ref, bias_ref, e_ref, w_ref):
    lg = lg_ref[...]
    ne = lg.shape[0]
    epg = ne // N_EXPERT_GROUPS
    mx = jnp.max(lg, axis=0, keepdims=True)
    ex = jnp.exp(lg - mx)
    probs = ex / jnp.sum(ex, axis=0, keepdims=True)
    sel = probs + bias_ref[...]
    rows = [sel[e:e + 1] for e in range(ne)]
    top2 = []
    for e in range(ne):
        g0 = (e // epg) * epg
        rank = jnp.zeros_like(rows[e])
        for o in range(g0, g0 + epg):
            if o == e:
                continue
            ahead = (rows[o] > rows[e]) | ((rows[o] == rows[e]) & (o < e))
            rank = rank + jnp.where(ahead, 1.0, 0.0)
        top2.append(rank < 1.5)
    score = []
    for g in range(N_EXPERT_GROUPS):
        sc = jnp.zeros_like(rows[0])
        for e in range(g * epg, (g + 1) * epg):
            sc = sc + jnp.where(top2[e], rows[e], 0.0)
        score.append(sc)
    best = []
    for g in range(N_EXPERT_GROUPS):
        ok = jnp.ones(rows[0].shape, jnp.bool_)
        for o in range(N_EXPERT_GROUPS):
            if o < g:
                ok = ok & (score[g] > score[o])
            elif o > g:
                ok = ok & (score[g] >= score[o])
        best.append(ok)
    first = jnp.full(rows[0].shape, float(ne), F32)
    second = jnp.full(rows[0].shape, -1.0, F32)
    p_first = jnp.zeros_like(rows[0])
    p_second = jnp.zeros_like(rows[0])
    for e in range(ne - 1, -1, -1):
        ch = top2[e] & best[e // epg]
        first = jnp.where(ch, float(e), first)
        p_first = jnp.where(ch, probs[e:e + 1], p_first)
    for e in range(ne):
        ch = top2[e] & best[e // epg]
        second = jnp.where(ch, float(e), second)
        p_second = jnp.where(ch, probs[e:e + 1], p_second)
    tot = p_first + p_second
    e_ref[0:1, :] = first.astype(jnp.int32)
    e_ref[1:2, :] = second.astype(jnp.int32)
    w_ref[0:1, :] = p_first / tot
    w_ref[1:2, :] = p_second / tot


def _route(logits_t, router_bias):
    ne, n = logits_t.shape
    return pl.pallas_call(
        _route_kernel,
        grid=(1,),
        in_specs=[pl.BlockSpec((ne, n), lambda i: (0, 0)), pl.BlockSpec((ne, 1), lambda i: (0, 0))],
        out_specs=[pl.BlockSpec((2, n), lambda i: (0, 0)), pl.BlockSpec((2, n), lambda i: (0, 0))],
        out_shape=[jax.ShapeDtypeStruct((2, n), jnp.int32), jax.ShapeDtypeStruct((2, n), F32)],
        compiler_params=_cp("arbitrary"),
        name="route",
    )(logits_t, router_bias.reshape(ne, 1))


def _dispatch_plan(e2, tm, n_tiles):
    n = e2.shape[1]
    e_flat = e2.reshape(-1)
    onehot = (e_flat[:, None] == jnp.arange(N_EXPERTS, dtype=jnp.int32)[None, :]).astype(jnp.int32)
    counts = jnp.sum(onehot, axis=0)
    rank = jnp.sum(jnp.cumsum(onehot, axis=0) * onehot, axis=1) - 1
    tiles_per = (counts + tm - 1) // tm
    tile_end = jnp.cumsum(tiles_per)
    starts = (tile_end - tiles_per) * tm
    dest = jnp.sum(onehot * starts[None, :], axis=1) + rank
    src = jnp.zeros((n_tiles * tm,), jnp.int32).at[dest].set(jnp.arange(2 * n, dtype=jnp.int32) % n)
    n_used = tile_end[-1]
    t_idx = jnp.minimum(jnp.arange(n_tiles, dtype=jnp.int32), n_used - 1)
    tile_expert = jnp.sum((tile_end[None, :] <= t_idx[:, None]).astype(jnp.int32), axis=1)
    tile_expert = jnp.minimum(tile_expert, N_EXPERTS - 1)
    te_hot = (tile_expert[:, None] == jnp.arange(N_EXPERTS, dtype=jnp.int32)[None, :]).astype(jnp.int32)
    first_tile = jnp.sum(te_hot * (tile_end - tiles_per)[None, :], axis=1)
    rows = jnp.sum(te_hot * counts[None, :], axis=1) - (jnp.arange(n_tiles, dtype=jnp.int32) - first_tile) * tm
    tile_groups = (jnp.clip(rows, 0, tm) + 7) // 8
    return dest, src, tile_expert, n_used.reshape(1).astype(jnp.int32), tile_groups.astype(jnp.int32)


def _moe_kernel(te_ref, nu_ref, src_ref, tg_ref, h_hbm, wg_ref, wu_ref, wd_ref, o_ref, xbuf, xb_scr, sem, *, tm):
    t, j = pl.program_id(0), pl.program_id(1)
    n_used = nu_ref[0]

    def row_copy(tile, slot, k):
        r = src_ref[tile * tm + k]
        return pltpu.make_async_copy(h_hbm.at[pl.ds(r, 1), :], xbuf.at[slot, pl.ds(k, 1), :], sem.at[slot])

    def issue(tile, slot):
        def body(g, c):
            for u in range(8):
                row_copy(tile, slot, g * 8 + u).start()
            return c
        lax.fori_loop(0, tg_ref[tile], body, 0)

    def drain(tile, slot):
        def body(g, c):
            for u in range(8):
                row_copy(tile, slot, g * 8 + u).wait()
            return c
        lax.fori_loop(0, tg_ref[tile], body, 0)

    @pl.when((j == 0) & (t == 0))
    def _():
        xbuf[...] = jnp.zeros_like(xbuf)
        issue(0, 0)

    @pl.when((j == 0) & (t + 1 < n_used))
    def _():
        issue(t + 1, (t + 1) % 2)

    @pl.when((j == 0) & (t < n_used))
    def _():
        drain(t, t % 2)
        xb_scr[...] = xbuf[t % 2].astype(BF16)

    @pl.when(j == 0)
    def _():
        o_ref[...] = jnp.zeros_like(o_ref)

    @pl.when(t < n_used)
    def _():
        x = xb_scr[...]
        hg = _dot(x, wg_ref[...].astype(BF16))
        hu = _dot(x, wu_ref[...].astype(BF16))
        act = (_silu(hg) * hu).astype(BF16)
        o_ref[...] += _dot(act, wd_ref[...].astype(BF16))


def _moe(h, src, tile_expert, n_used, tile_groups, w_gate, w_up, w_down, layer, tm=MOE_TM, tf=MOE_TF):
    d = h.shape[1]
    r = src.shape[0]
    n_tiles = r // tm
    nj = D_EXPERT // tf

    def jj(t, j, nu):
        return jnp.where(t < nu[0], j, nj - 1)

    grid_spec = pltpu.PrefetchScalarGridSpec(
        num_scalar_prefetch=4,
        grid=(n_tiles, nj),
        in_specs=[pl.BlockSpec(memory_space=pl.ANY),
                  pl.BlockSpec((None, None, d, tf), lambda t, j, te, nu, sr, tg: (layer, te[t], 0, jj(t, j, nu))),
                  pl.BlockSpec((None, None, d, tf), lambda t, j, te, nu, sr, tg: (layer, te[t], 0, jj(t, j, nu))),
                  pl.BlockSpec((None, None, tf, d), lambda t, j, te, nu, sr, tg: (layer, te[t], jj(t, j, nu), 0))],
        out_specs=pl.BlockSpec((tm, d), lambda t, j, te, nu, sr, tg: (t, 0)),
        scratch_shapes=[pltpu.VMEM((2, tm, d), F32), pltpu.VMEM((tm, d), BF16), pltpu.SemaphoreType.DMA((2,))],
    )
    return pl.pallas_call(
        functools.partial(_moe_kernel, tm=tm),
        grid_spec=grid_spec,
        out_shape=jax.ShapeDtypeStruct((r, d), F32),
        compiler_params=_cp("arbitrary", "arbitrary"),
        name="moe_experts",
    )(tile_expert, n_used, src, tile_groups, h, w_gate, w_up, w_down)


def _combine_kernel(*refs, tm, n, fuse_norm):
    it = iter(refs)
    dest_ref, x_ref, y_hbm, w_ref, ga_ref = (next(it) for _ in range(5))
    g_ref, sc_ref, sh_ref = (next(it), next(it), next(it)) if fuse_norm else (None, None, None)
    o_ref = next(it)
    h_ref = next(it) if fuse_norm else None
    ybuf, sem = next(it), next(it)
    i = pl.program_id(0)
    nt = pl.num_programs(0)

    def row_copy(tile, slot, s, k):
        r = dest_ref[s * n + tile * tm + k]
        return pltpu.make_async_copy(y_hbm.at[pl.ds(r, 1), :], ybuf.at[slot, s, pl.ds(k, 1), :], sem.at[slot])

    def issue(tile, slot):
        def body(k, c):
            row_copy(tile, slot, 0, k).start()
            row_copy(tile, slot, 1, k).start()
            return c
        lax.fori_loop(0, tm, body, 0, unroll=8)

    def drain(tile, slot):
        def body(k, c):
            row_copy(tile, slot, 0, k).wait()
            row_copy(tile, slot, 1, k).wait()
            return c
        lax.fori_loop(0, tm, body, 0, unroll=8)

    @pl.when(i == 0)
    def _():
        issue(0, 0)

    @pl.when(i + 1 < nt)
    def _():
        issue(i + 1, (i + 1) % 2)

    slot = i % 2
    drain(i, slot)
    w = w_ref[...]
    y = w[:, 0:1] * ybuf[slot, 0] + w[:, 1:2] * ybuf[slot, 1]
    xn = x_ref[...] + ga_ref[...] * y
    o_ref[...] = xn
    if fuse_norm:
        h = xn * lax.rsqrt(jnp.mean(xn * xn, axis=-1, keepdims=True) + EPS) * g_ref[...]
        h_ref[...] = (h * (1.0 + sc_ref[...]) + sh_ref[...]).astype(h_ref.dtype)


def _combine(x, y, dest, w2, mod_l, ga_chunk, next_norm=None, tm=256):
    n, d = x.shape
    row = pl.BlockSpec((tm, d), lambda i, dst: (i, 0))
    in_specs = [row, pl.BlockSpec(memory_space=pl.ANY), pl.BlockSpec((tm, 2), lambda i, dst: (i, 0)),
                pl.BlockSpec((None, 1, d), lambda i, dst: (_cond_row(i, tm), 0, ga_chunk))]
    args = [dest, x, y, w2, mod_l]
    out_specs = [row]
    out_shape = [jax.ShapeDtypeStruct((n, d), F32)]
    if next_norm is not None:
        g, mod_n, (sh_c, sc_c) = next_norm
        in_specs += [pl.BlockSpec((1, d), lambda i, dst: (0, 0)),
                     pl.BlockSpec((None, 1, d), lambda i, dst: (_cond_row(i, tm), 0, sc_c)),
                     pl.BlockSpec((None, 1, d), lambda i, dst: (_cond_row(i, tm), 0, sh_c))]
        args += [g.reshape(1, d), mod_n, mod_n]
        out_specs.append(row)
        out_shape.append(jax.ShapeDtypeStruct((n, d), BF16))
    grid_spec = pltpu.PrefetchScalarGridSpec(
        num_scalar_prefetch=1,
        grid=(n // tm,),
        in_specs=in_specs,
        out_specs=out_specs,
        scratch_shapes=[pltpu.VMEM((2, 2, tm, d), F32), pltpu.SemaphoreType.DMA((2,))],
    )
    res = pl.pallas_call(
        functools.partial(_combine_kernel, tm=tm, n=n, fuse_norm=next_norm is not None),
        grid_spec=grid_spec,
        out_shape=out_shape,
        compiler_params=_cp("arbitrary"),
        name="combine",
    )(*args)
    return (res[0], res[1]) if next_norm is not None else (res[0], None)


def _layer(x, h, l, mod, p, lbs, w_router, router_bias, states, new_states):
    n = x.shape[0]
    mod_l = mod[l]
    state_hgrn, state_rglru, state_ssd = states
    if h is None:
        h = _normmod(x, p["norm1_g"][l], mod_l, chunk=(0, 1))
    proj = _mm(h, p["w_in"], l, 0, N_MAIN)
    dtp = _mm(h, p["w_dt"], l, 0, 512, tn=512)
    gates = _gates(h, p["w_merge"], l)
    nb = n // TOK_BLOCK
    ns = nb - N_CTX_BLOCKS
    hg_prev, ssd_prev = new_states
    oa, hg_new = _hgrn(proj, lbs[l], p["hgrn_norm_g"][l], 0, N_CTX_BLOCKS, SEQ, layer=l, st_prev=hg_prev)
    oa, _ = _hgrn(proj, lbs[l], p["hgrn_norm_g"][l], N_CTX_BLOCKS, ns, TOK_BLOCK, state_hgrn, l, prev=oa)
    ob, lru_new = _lru(proj, p, l, 0, N_CTX_BLOCKS, SEQ)
    ob, _ = _lru(proj, p, l, N_CTX_BLOCKS, ns, GRID_W, state_rglru, prev=ob)
    oc, ssd_new = _ssd(proj, dtp, p, l, 0, N_CTX_BLOCKS, SEQ, SEQ, st_prev=ssd_prev)
    oc, _ = _ssd(proj, dtp, p, l, N_CTX_BLOCKS, ns, TOK_BLOCK, GRID_W, state_ssd, prev=oc)
    merged = _merge(oa, ob, oc, p["w_branch_a"], p["w_branch_b"], p["w_branch_c"], gates, l)
    x = _mm_residual(merged, p["w_out"], l, x, mod_l, 2)
    h2, logits = _normmod(x, p["norm2_g"][l], mod_l, chunk=(3, 4), w_router=w_router, out_dtype=F32)
    e2, w2 = _route(logits.T, router_bias)
    n_tiles = (2 * n) // MOE_TM + N_EXPERTS
    dest, src, tile_expert, n_used, tile_groups = _dispatch_plan(e2, MOE_TM, n_tiles)
    y = _moe(h2, src, tile_expert, n_used, tile_groups, p["w_e_gate"], p["w_e_up"], p["w_e_down"], l)
    next_norm = (p["norm1_g"][l + 1], mod[l + 1], (0, 1)) if l + 1 < DEPTH else None
    x, h_next = _combine(x, y, dest, w2.T, mod_l, 5, next_norm)
    return x, h_next, hg_new, lru_new.transpose(1, 0, 2), ssd_new


def _hgrn_lower_bounds(lb_raw):
    pr = jax.nn.softmax(lb_raw.astype(F32), axis=0)
    cum = jnp.cumsum(pr, axis=0)
    return cum - cum[0]


def kernel(x_prompt, x_sample, state_hgrn, state_rglru, state_ssd, c, c_ctx, w_mod, b_mod, norm1_g, norm2_g, w_in, hgrn_lb, hgrn_norm_g, conv_b_w, conv_b_b, lru_wa, lru_ba, lru_wx, lru_bx, lru_lambda, conv_c_w, conv_c_b, ssd_a_log, ssd_dt_bias, ssd_d, ssd_norm_g, w_branch_a, w_branch_b, w_branch_c, w_out, w_router, router_bias, w_e_gate, w_e_up, w_e_down, final_g):
    bsz, seq, d = x_prompt.shape
    dbsz, dseq, _ = x_sample.shape
    assert seq == SEQ and dseq == TOK_BLOCK and (bsz * seq) == N_CTX_BLOCKS * TOK_BLOCK and d == D_MODEL
    x = jnp.concatenate([x_prompt.reshape(-1, d), x_sample.reshape(-1, d)], axis=0)

    cond8 = jnp.zeros((8, d), F32).at[0].set(c_ctx).at[1:1 + dbsz].set(c)
    mod = _modulation(cond8, w_mod, b_mod).reshape(DEPTH, 8, 1, N_MOD * d)

    hpg = C_HEADS // C_GROUPS
    w_dt_raw = w_in[:, :, COL_DT:COL_DT + 2 * C_HEADS].reshape(DEPTH, d, 2, C_GROUPS, hpg)
    w_dt = jnp.zeros((DEPTH, d, C_GROUPS, 128), F32).at[:, :, :, :2 * hpg].set(
        w_dt_raw.transpose(0, 1, 3, 2, 4).reshape(DEPTH, d, C_GROUPS, 2 * hpg)).reshape(DEPTH, d, C_GROUPS * 128)

    def group_lanes(a):
        g = a.reshape(DEPTH, 2, C_GROUPS, hpg).transpose(0, 2, 1, 3).reshape(DEPTH, C_GROUPS, 1, 2 * hpg)
        return jnp.zeros((DEPTH, C_GROUPS, 1, 128), F32).at[..., :2 * hpg].set(g)

    p = dict(norm1_g=norm1_g, norm2_g=norm2_g, w_in=w_in, w_dt=w_dt,
             w_merge=w_in[:, :, COL_MERGE:], hgrn_norm_g=hgrn_norm_g,
             conv_b_w=conv_b_w, conv_b_b=conv_b_b, lru_wa=lru_wa, lru_ba=lru_ba, lru_wx=lru_wx,
             lru_bx=lru_bx, lru_lambda=lru_lambda, conv_c_w=conv_c_w, conv_c_b=conv_c_b,
             dt_bias_g=group_lanes(ssd_dt_bias), a_log_g=group_lanes(ssd_a_log),
             ssd_d_rep=jnp.repeat(ssd_d, C_HEADDIM, axis=-1).reshape(DEPTH, 1, C_INNER),
             ssd_norm_g=ssd_norm_g, w_branch_a=w_branch_a, w_branch_b=w_branch_b, w_branch_c=w_branch_c,
             w_out=w_out, w_e_gate=w_e_gate, w_e_up=w_e_up, w_e_down=w_e_down)
    lbs = _hgrn_lower_bounds(hgrn_lb)

    h = hg_new = ssd_new = None
    lru_list = []
    for l in range(DEPTH):
        x, h, hg_new, lru_new, ssd_new = _layer(x, h, l, mod, p, lbs, w_router, router_bias,
                                                (state_hgrn, state_rglru, state_ssd), (hg_new, ssd_new))
        lru_list.append(lru_new)
    n_ctx = bsz * seq
    y_prompt = _normmod(x, final_g, out_dtype=F32, row0=0, n_rows=n_ctx).reshape(bsz, seq, d)
    y_sample = _normmod(x, final_g, out_dtype=F32, row0=n_ctx, n_rows=dbsz * dseq).reshape(dbsz, dseq, d)
    return (y_prompt, y_sample, hg_new, jnp.stack(lru_list, axis=1), ssd_new)
```

```python
import functools
import math

import numpy as np
import jax
import jax.numpy as jnp
from jax import lax
from jax.experimental import pallas as pl
from jax.experimental.pallas import tpu as pltpu

F32 = jnp.float32
BF16 = jnp.bfloat16
HIGHEST = lax.Precision.HIGHEST

EPS = 1e-6
D_MODEL = 2048
DEPTH = 4
N_MOD = 6
GRID_W = 64
SEQ = 256
TOK_BLOCK = 2048
N_CTX_BLOCKS = 2

A_HEADS = 8
A_DK = 128
A_DV = 128
A_WIDTH = 1024
HGRN_C = 128

B_WIDTH = 1024
B_BLOCK = 128
LRU_C = 8.0
LRU_SEG = 256

C_INNER = 1024
C_HEADDIM = 64
C_HEADS = 16
C_GROUPS = 4
C_STATE = 128
SSD_L = 128
CONV_W = 4

N_EXPERTS = 16
N_EXPERT_GROUPS = 4
D_EXPERT = 1024
MOE_TM = 1024
MOE_TF = 256

COL_Q, COL_FF, COL_FB, COL_V, COL_GA = 0, 1024, 2048, 3072, 4096
COL_XB, COL_GB, COL_Z, COL_XBC, COL_DT, COL_MERGE = 5120, 6144, 7168, 8192, 10240, 10272
N_MAIN = 10240

VMEM_LIMIT_BYTES = 56 * 1024 * 1024
NEG_BIG = -1e30


def _cp(*sem):
    return pltpu.CompilerParams(dimension_semantics=sem, vmem_limit_bytes=VMEM_LIMIT_BYTES)


def _silu(x):
    return x * jax.nn.sigmoid(x)


def _dot(a, b):
    return jnp.dot(a, b, preferred_element_type=F32)


def _dot_nt(a, b):
    return lax.dot_general(a, b, (((1,), (1,)), ((), ())), preferred_element_type=F32)


def _dot_tn(a, b):
    return lax.dot_general(a, b, (((0,), (0,)), ((), ())), preferred_element_type=F32)


def _dot_hi(a, b):
    return jnp.dot(a, b, preferred_element_type=F32, precision=HIGHEST)


def _cond_row(i, tm):
    return jnp.maximum((i * tm) // TOK_BLOCK - (N_CTX_BLOCKS - 1), 0)


def _mod_kernel(c_ref, w_ref, b_ref, o_ref):
    s = _silu(c_ref[...]).astype(BF16)
    o_ref[...] = _dot(s, w_ref[...].astype(BF16)) + b_ref[...]


def _modulation(cond8, w_mod, b_mod):
    depth, d, n = w_mod.shape
    tn = 1024
    return pl.pallas_call(
        _mod_kernel,
        grid=(depth, n // tn),
        in_specs=[pl.BlockSpec((8, d), lambda l, j: (0, 0)),
                  pl.BlockSpec((None, d, tn), lambda l, j: (l, 0, j)),
                  pl.BlockSpec((None, 1, tn), lambda l, j: (l, 0, j))],
        out_specs=pl.BlockSpec((None, 8, tn), lambda l, j: (l, 0, j)),
        out_shape=jax.ShapeDtypeStruct((depth, 8, n), F32),
        compiler_params=_cp("arbitrary", "arbitrary"),
        name="modulation",
    )(cond8, w_mod, b_mod.reshape(depth, 1, n))


def _normmod_kernel(*refs, modulate, router):
    it = iter(refs)
    x_ref, g_ref = next(it), next(it)
    sc_ref = sh_ref = wr_ref = lg_ref = None
    if modulate:
        sc_ref, sh_ref = next(it), next(it)
    if router:
        wr_ref = next(it)
    h_ref = next(it)
    if router:
        lg_ref = next(it)
    x = x_ref[...]
    h = x * lax.rsqrt(jnp.mean(x * x, axis=-1, keepdims=True) + EPS) * g_ref[...]
    if modulate:
        h = h * (1.0 + sc_ref[...]) + sh_ref[...]
    h_ref[...] = h.astype(h_ref.dtype)
    if router:
        lg_ref[...] = _dot_hi(h, wr_ref[...])


def _normmod(x, g, mod_l=None, chunk=None, w_router=None, out_dtype=BF16, tm=512, row0=0, n_rows=None):
    d = x.shape[1]
    n = x.shape[0] if n_rows is None else n_rows
    r0 = row0 // tm
    modulate = mod_l is not None
    router = w_router is not None
    assert not (modulate and row0)
    in_specs = [pl.BlockSpec((tm, d), lambda i: (i + r0, 0)),
                pl.BlockSpec((1, d), lambda i: (0, 0))]
    args = [x, g.reshape(1, d)]
    if modulate:
        sh_c, sc_c = chunk
        in_specs += [pl.BlockSpec((None, 1, d), lambda i: (_cond_row(i, tm), 0, sc_c)),
                     pl.BlockSpec((None, 1, d), lambda i: (_cond_row(i, tm), 0, sh_c))]
        args += [mod_l, mod_l]
    out_specs = [pl.BlockSpec((tm, d), lambda i: (i, 0))]
    out_shape = [jax.ShapeDtypeStruct((n, d), out_dtype)]
    if router:
        ne = w_router.shape[1]
        in_specs.append(pl.BlockSpec((d, ne), lambda i: (0, 0)))
        args.append(w_router)
        out_specs.append(pl.BlockSpec((tm, ne), lambda i: (i, 0)))
        out_shape.append(jax.ShapeDtypeStruct((n, ne), F32))
    res = pl.pallas_call(
        functools.partial(_normmod_kernel, modulate=modulate, router=router),
        grid=(n // tm,),
        in_specs=in_specs, out_specs=out_specs, out_shape=out_shape,
        compiler_params=_cp("arbitrary"),
        name="normmod",
    )(*args)
    return res if router else res[0]


def _mm_kernel(a_ref, w_ref, o_ref, *, sigmoid):
    acc = _dot_nt(a_ref[...], w_ref[0].astype(BF16))
    if sigmoid:
        acc = jax.nn.sigmoid(acc)
    o_ref[...] = acc.astype(o_ref.dtype)


def _mm_t(a, wt, layer, row0, n_out, sigmoid=False, out_dtype=F32, tm=2048, tn=512):
    m, k = a.shape
    tn = min(tn, n_out)
    assert row0 % 8 == 0 and n_out % tn == 0 and m % tm == 0
    return pl.pallas_call(
        functools.partial(_mm_kernel, sigmoid=sigmoid),
        grid=(m // tm, n_out // tn),
        in_specs=[pl.BlockSpec((tm, k), lambda i, j: (i, 0)),
                  pl.BlockSpec((pl.Element(1), pl.Element(tn), pl.Element(k)),
                               lambda i, j: (layer, pl.multiple_of(row0 + j * tn, 8), 0))],
        out_specs=pl.BlockSpec((tm, tn), lambda i, j: (i, j)),
        out_shape=jax.ShapeDtypeStruct((m, n_out), out_dtype),
        compiler_params=_cp("arbitrary", "arbitrary"),
        name="matmul",
    )(a, wt)


def _mm_res_kernel(a_ref, w_ref, x_ref, ga_ref, o_ref):
    acc = _dot(a_ref[...], w_ref[...].astype(BF16))
    o_ref[...] = x_ref[...] + ga_ref[...] * acc


def _mm_residual(a, w, layer, x, mod_l, ga_chunk, tm=1024, tn=512):
    m, k = a.shape
    n = w.shape[-1]
    gs = n // tn
    return pl.pallas_call(
        _mm_res_kernel,
        grid=(m // tm, n // tn),
        in_specs=[pl.BlockSpec((tm, k), lambda i, j: (i, 0)),
                  pl.BlockSpec((None, k, tn), lambda i, j: (layer, 0, j)),
                  pl.BlockSpec((tm, tn), lambda i, j: (i, j)),
                  pl.BlockSpec((None, 1, tn), lambda i, j: (_cond_row(i, tm), 0, ga_chunk * gs + j))],
        out_specs=pl.BlockSpec((tm, tn), lambda i, j: (i, j)),
        out_shape=jax.ShapeDtypeStruct((m, n), F32),
        compiler_params=_cp("arbitrary", "arbitrary"),
        name="out_proj",
    )(a, w, x, mod_l)


def _merge_kernel(oa_ref, ob_ref, oc_ref, wa_ref, wb_ref, wc_ref, g1_ref, g2_ref, g3_ref, o_ref):
    ya = _dot(oa_ref[...], wa_ref[...].astype(BF16))
    yb = _dot(ob_ref[...], wb_ref[...].astype(BF16))
    yc = _dot(oc_ref[...], wc_ref[...].astype(BF16))
    m = (g1_ref[...].astype(F32) * ya + g2_ref[...].astype(F32) * yb + g3_ref[...].astype(F32) * yc)
    o_ref[...] = m.astype(o_ref.dtype)


def _merge(oa, ob, oc, wa, wb, wc, gates, layer, tm=1024, tn=512):
    m, k = oa.shape
    n = wa.shape[-1]
    gs = n // tn
    a_spec = pl.BlockSpec((tm, k), lambda i, j: (i, 0))
    w_spec = pl.BlockSpec((None, k, tn), lambda i, j: (layer, 0, j))
    return pl.pallas_call(
        _merge_kernel,
        grid=(m // tm, n // tn),
        in_specs=[a_spec, a_spec, a_spec, w_spec, w_spec, w_spec,
                  pl.BlockSpec((tm, tn), lambda i, j: (i, j)),
                  pl.BlockSpec((tm, tn), lambda i, j: (i, j + gs)),
                  pl.BlockSpec((tm, tn), lambda i, j: (i, j + 2 * gs))],
        out_specs=pl.BlockSpec((tm, tn), lambda i, j: (i, j)),
        out_shape=jax.ShapeDtypeStruct((m, n), BF16),
        compiler_params=_cp("arbitrary", "arbitrary"),
        name="merge",
    )(oa, ob, oc, wa, wb, wc, gates, gates, gates)


def _dwconv(x, w, b, row_len):
    t = x.shape[0]
    pos = lax.broadcasted_iota(jnp.int32, x.shape, 0) % row_len
    xm2 = jnp.where(pos >= 2, pltpu.roll(x, 2, axis=0), 0.0)
    xm1 = jnp.where(pos >= 1, pltpu.roll(x, 1, axis=0), 0.0)
    xp1 = jnp.where(pos <= row_len - 2, pltpu.roll(x, t - 1, axis=0), 0.0)
    y = b + xm2 * w[0:1]
    y = y + xm1 * w[1:2]
    y = y + x * w[2:3]
    y = y + xp1 * w[3:4]
    return y


def _hgrn_consts(c):
    nl = int(math.log2(c))
    seg = np.zeros((2, nl + 1, c, c), np.float32)
    msk = np.zeros((2, nl + 1, c, c), np.float32)
    up = np.zeros((2, nl, c, 128), np.float32)
    for d in range(2):
        tt = np.arange(c) if d == 0 else c - 1 - np.arange(c)
        tr, tc = tt[:, None], tt[None, :]
        seg[d, 0] = tc <= tr
        for l in range(nl):
            s = 1 << l
            blk, upper = tt // (2 * s), (tt % (2 * s)) >= s
            mid = (blk * 2 * s + s)[:, None]
            seg_u = (tc >= mid) & (tc <= tr)
            seg_l = (tc >= tr + 1) & (tc <= mid - 1)
            seg[d, l + 1] = np.where(upper[:, None], seg_u, seg_l)
            msk[d, l] = upper[:, None] & (~upper)[None, :] & (blk[:, None] == blk[None, :])
            up[d, l] = upper[:, None]
        msk[d, nl] = np.eye(c)
    return seg.reshape(2, (nl + 1) * c, c), msk, up


def _hgrn_kernel(*refs, seq_len, chained, aliased):
    it = iter(refs)
    q_ref, ff_ref, fb_ref, v_ref, ga_ref, lb_ref, ng_ref = (next(it) for _ in range(7))
    seg_ref, msk_ref, up_ref = next(it), next(it), next(it)
    s0_ref = next(it) if chained else None
    for _ in range(aliased):
        next(it)
    o_ref = next(it)
    sout_ref = None if chained else next(it)
    o_scr, qp_scr, u_scr, dec_scr, snap_scr = (next(it) for _ in range(5))

    c = HGRN_C
    nl = int(math.log2(c))
    tb = q_ref.shape[0]
    n_chunks = tb // c
    n_ch = seq_len // c
    lb = lb_ref[...]
    f_refs = (ff_ref, fb_ref)

    def phase_a(n, carry):
        rows = pl.ds(pl.multiple_of(n * c, c), c)
        q = _silu(q_ref[rows, :]) * (A_DK ** -0.5)
        qb = q.astype(BF16)
        v = v_ref[rows, :].astype(BF16)
        o = jnp.zeros((c, A_DV), F32)
        qps, kps, decs = [], [], []
        for d in range(2):
            f = lb + (1.0 - lb) * jax.nn.sigmoid(f_refs[d][rows, :])
            k = 1.0 - f
            lf = jnp.log(f)
            hi = lf.astype(BF16)
            mid = (lf - hi.astype(F32)).astype(BF16)
            e2 = _dot(seg_ref[d], jnp.concatenate([hi, mid], axis=1))
            e = e2[:, :A_DK] + e2[:, A_DK:]
            b = e[0:c]
            att = msk_ref[d, nl] * _dot_nt(qb, k.astype(BF16))
            for l in range(nl):
                x = (jnp.where(up_ref[d, l] > 0.5, q, k) * jnp.exp(e[(l + 1) * c:(l + 2) * c])).astype(BF16)
                att = att + msk_ref[d, l] * _dot_nt(x, x)
            o = o + _dot(att.astype(BF16), v)
            b_end = b[c - 1:c] if d == 0 else b[0:1]
            qps.append((q * jnp.exp(b)).astype(BF16))
            kps.append((k * jnp.exp(b_end - b)).astype(BF16))
            decs.append(jnp.exp(b_end))
        o_scr[rows, :] = o
        qp_scr[rows, :] = jnp.concatenate(qps, axis=1)
        u_scr[n] = _dot_tn(v, jnp.concatenate(kps, axis=1))
        dec_scr[n] = jnp.concatenate(decs, axis=1)
        return carry

    lax.fori_loop(0, n_chunks, phase_a, 0)

    def init_state(d):
        return s0_ref[d].T if chained else jnp.zeros((A_DV, A_DK), F32)

    def phase_b1(j, carry):
        sf, sb = carry
        nf = j
        nb = n_chunks - 1 - j
        if not chained:
            sf = jnp.where(nf % n_ch == 0, 0.0, sf)
            sb = jnp.where(nb % n_ch == n_ch - 1, 0.0, sb)
        snap_scr[nf, :, 0:A_DK] = sf.astype(BF16)
        snap_scr[nb, :, A_DK:2 * A_DK] = sb.astype(BF16)
        sf = sf * dec_scr[nf][:, 0:A_DK] + u_scr[nf][:, 0:A_DK]
        sb = sb * dec_scr[nb][:, A_DK:2 * A_DK] + u_scr[nb][:, A_DK:2 * A_DK]
        if not chained:
            @pl.when(nf % n_ch == n_ch - 1)
            def _():
                sout_ref[nf // n_ch, 0] = sf.T

            @pl.when(nb % n_ch == 0)
            def _():
                sout_ref[nb // n_ch, 1] = sb.T
        return sf, sb

    lax.fori_loop(0, n_chunks, phase_b1, (init_state(0), init_state(1)))

    ng = ng_ref[...]

    def phase_b2(n, carry):
        rows = pl.ds(pl.multiple_of(n * c, c), c)
        o = o_scr[rows, :] + _dot_nt(qp_scr[rows, :], snap_scr[n])
        o = o * lax.rsqrt(jnp.mean(o * o, axis=-1, keepdims=True) + EPS) * ng
        o_ref[rows, :] = (o * _silu(ga_ref[rows, :])).astype(o_ref.dtype)
        return carry

    lax.fori_loop(0, n_chunks, phase_b2, 0, unroll=4)


def _alias_out(prevs, in_specs, args):
    aliases = {}
    for out_idx, prev in prevs:
        if prev is not None:
            in_specs.append(pl.BlockSpec(memory_space=pl.ANY))
            args.append(prev)
            aliases[len(args) - 1] = out_idx
    return aliases


def _hgrn(proj, lb_l, ng_l, blk0, n_blk, seq_len, state=None, layer=0, prev=None, st_prev=None):
    tb = TOK_BLOCK
    chained = state is not None
    seg, msk, up = _hgrn_consts(HGRN_C)
    cb = lambda col: col // A_DK

    def col_spec(col):
        return pl.BlockSpec((tb, A_DK), lambda b, h: (b + blk0, cb(col) + h))

    def full(a):
        nd = a.ndim
        return pl.BlockSpec(a.shape, lambda b, h: (0,) * nd)

    in_specs = [col_spec(COL_Q), col_spec(COL_FF), col_spec(COL_FB), col_spec(COL_V), col_spec(COL_GA),
                pl.BlockSpec((1, A_DK), lambda b, h: (0, h)),
                pl.BlockSpec((1, A_DV), lambda b, h: (0, 0)),
                full(seg), full(msk), full(up)]
    args = [proj, proj, proj, proj, proj, lb_l.reshape(1, -1), ng_l.reshape(1, -1),
            jnp.asarray(seg, BF16), jnp.asarray(msk), jnp.asarray(up)]
    out_specs = [pl.BlockSpec((tb, A_DV), lambda b, h: (b + blk0, h))]
    out_shape = [jax.ShapeDtypeStruct((proj.shape[0], A_WIDTH), BF16)]
    if chained:
        in_specs.append(pl.BlockSpec((None, None, 2, None, A_DK, A_DV), lambda b, h: (b, layer, 0, h, 0, 0)))
        args.append(state)
    else:
        n_seq = tb // seq_len
        out_specs.append(pl.BlockSpec((n_seq, None, 2, None, A_DK, A_DV), lambda b, h: (b, layer, 0, h, 0, 0)))
        out_shape.append(jax.ShapeDtypeStruct((n_blk * n_seq, DEPTH, 2, A_HEADS, A_DK, A_DV), F32))
    aliases = _alias_out([(0, prev), (1, st_prev)], in_specs, args)
    n_chunks = tb // HGRN_C
    res = pl.pallas_call(
        functools.partial(_hgrn_kernel, seq_len=seq_len, chained=chained, aliased=len(aliases)),
        grid=(n_blk, A_HEADS),
        in_specs=in_specs, out_specs=out_specs, out_shape=out_shape,
        input_output_aliases=aliases,
        scratch_shapes=[pltpu.VMEM((tb, A_DV), F32), pltpu.VMEM((tb, 2 * A_DK), BF16),
                        pltpu.VMEM((n_chunks, A_DV, 2 * A_DK), F32), pltpu.VMEM((n_chunks, 1, 2 * A_DK), F32),
                        pltpu.VMEM((n_chunks, A_DV, 2 * A_DK), BF16)],
        compiler_params=_cp("arbitrary", "arbitrary"),
        name="hgrn2",
    )(*args)
    return res[0], (None if chained else res[1])


def _lru_kernel(*refs, row_len, chained, aliased):
    it = iter(refs)
    x_ref, gb_ref, cw_ref, cb_ref, wa_ref, ba_ref, wx_ref, bx_ref, lam_ref = (next(it) for _ in range(9))
    h0_ref = next(it) if chained else None
    for _ in range(aliased):
        next(it)
    o_ref = next(it)
    hout_ref = None if chained else next(it)
    a_scr, u_scr, h_scr, p_scr = (next(it) for _ in range(4))

    tb = x_ref.shape[0]
    seg = LRU_SEG
    n_seg = tb // seg
    xc = _dwconv(x_ref[...], cw_ref[...], cb_ref[...], row_len)
    xcb = xc.astype(BF16)
    for d in range(2):
        r = jax.nn.sigmoid(_dot(xcb, wa_ref[d].astype(BF16)) + ba_ref[d])
        g = jax.nn.sigmoid(_dot(xcb, wx_ref[d].astype(BF16)) + bx_ref[d])
        log_a = LRU_C * r * jax.nn.log_sigmoid(lam_ref[d])
        a = jnp.exp(log_a)
        a_scr[d] = a
        u_scr[d] = jnp.sqrt(1.0 - a * a) * (g * xc)

    def step(i, carry):
        hf, pf, hb, pb = carry
        tf = i
        tr = seg - 1 - i
        af = a_scr[0, pl.ds(tf, n_seg, stride=seg), :]
        hf = af * hf + u_scr[0, pl.ds(tf, n_seg, stride=seg), :]
        h_scr[0, pl.ds(tf, n_seg, stride=seg), :] = hf
        ab = a_scr[1, pl.ds(tr, n_seg, stride=seg), :]
        hb = ab * hb + u_scr[1, pl.ds(tr, n_seg, stride=seg), :]
        h_scr[1, pl.ds(tr, n_seg, stride=seg), :] = hb
        if chained:
            pf = af * pf
            pb = ab * pb
            p_scr[0, pl.ds(tf, n_seg, stride=seg), :] = pf
            p_scr[1, pl.ds(tr, n_seg, stride=seg), :] = pb
        return hf, pf, hb, pb

    zeros = jnp.zeros((n_seg, B_BLOCK), F32)
    ones = jnp.ones((n_seg, B_BLOCK), F32)
    hf, pf, hb, pb = lax.fori_loop(0, seg, step, (zeros, ones, zeros, ones), unroll=8)

    if chained:
        hin = h0_ref[0:1, :]
        for s in range(n_seg):
            rows = pl.ds(s * seg, seg)
            h_scr[0, rows, :] = h_scr[0, rows, :] + p_scr[0, rows, :] * hin
            hin = hf[s:s + 1] + pf[s:s + 1] * hin
        hin = h0_ref[1:2, :]
        for s in range(n_seg - 1, -1, -1):
            rows = pl.ds(s * seg, seg)
            h_scr[1, rows, :] = h_scr[1, rows, :] + p_scr[1, rows, :] * hin
            hin = hb[s:s + 1] + pb[s:s + 1] * hin
    else:
        hout_ref[0] = hf
        hout_ref[1] = hb
    o_ref[...] = ((h_scr[0] + h_scr[1]) * jax.nn.gelu(gb_ref[...])).astype(o_ref.dtype)


def _lru(proj, p, layer, blk0, n_blk, row_len, state=None, prev=None):
    tb = TOK_BLOCK
    chained = state is not None
    nb = B_WIDTH // B_BLOCK
    cbx, cbg = COL_XB // B_BLOCK, COL_GB // B_BLOCK
    in_specs = [pl.BlockSpec((tb, B_BLOCK), lambda b, n: (b + blk0, cbx + n)),
                pl.BlockSpec((tb, B_BLOCK), lambda b, n: (b + blk0, cbg + n)),
                pl.BlockSpec((None, CONV_W, B_BLOCK), lambda b, n: (layer, 0, n)),
                pl.BlockSpec((None, 1, B_BLOCK), lambda b, n: (layer, 0, n)),
                pl.BlockSpec((None, 2, None, B_BLOCK, B_BLOCK), lambda b, n: (layer, 0, n, 0, 0)),
                pl.BlockSpec((None, 2, 1, B_BLOCK), lambda b, n: (layer, 0, 0, n)),
                pl.BlockSpec((None, 2, None, B_BLOCK, B_BLOCK), lambda b, n: (layer, 0, n, 0, 0)),
                pl.BlockSpec((None, 2, 1, B_BLOCK), lambda b, n: (layer, 0, 0, n)),
                pl.BlockSpec((None, 2, 1, B_BLOCK), lambda b, n: (layer, 0, 0, n))]
    d4 = lambda a: a.reshape(DEPTH, 2, 1, B_WIDTH)
    args = [proj, proj, p["conv_b_w"], p["conv_b_b"].reshape(DEPTH, 1, B_WIDTH),
            p["lru_wa"], d4(p["lru_ba"]), p["lru_wx"], d4(p["lru_bx"]), d4(p["lru_lambda"])]
    out_specs = [pl.BlockSpec((tb, B_BLOCK), lambda b, n: (b + blk0, n))]
    out_shape = [jax.ShapeDtypeStruct((proj.shape[0], B_WIDTH), BF16)]
    if chained:
        in_specs.append(pl.BlockSpec((None, None, 2, B_BLOCK), lambda b, n: (b, layer, 0, n)))
        args.append(state)
    else:
        n_seq = tb // LRU_SEG
        out_specs.append(pl.BlockSpec((2, n_seq, B_BLOCK), lambda b, n: (0, b, n)))
        out_shape.append(jax.ShapeDtypeStruct((2, n_blk * n_seq, B_WIDTH), F32))
    aliases = _alias_out([(0, prev)], in_specs, args)
    res = pl.pallas_call(
        functools.partial(_lru_kernel, row_len=row_len, chained=chained, aliased=len(aliases)),
        grid=(n_blk, nb),
        in_specs=in_specs, out_specs=out_specs, out_shape=out_shape,
        input_output_aliases=aliases,
        scratch_shapes=[pltpu.VMEM((2, tb, B_BLOCK), F32) for _ in range(4)],
        compiler_params=_cp("arbitrary", "arbitrary"),
        name="rglru",
    )(*args)
    return res[0], (None if chained else res[1])


def _ssd_kernel(*refs, seq_len, row_len, chained, aliased):
    it = iter(refs)
    (xs_ref, bm_ref, cm_ref, z_ref, dt_ref, cwx_ref, cwb_ref, cwc_ref, cbx_ref, cbb_ref, cbc_ref,
     dtb_ref, a_ref, dsk_ref, ng_ref, tri_ref, ecat_ref, hmask_ref) = (next(it) for _ in range(18))
    s0_ref = next(it) if chained else None
    for _ in range(aliased):
        next(it)
    o_ref = next(it)
    sout_ref = None if chained else next(it)
    (xs_scr, bm_scr, cm_scr, dt_scr, y_scr, dcy_scr, u_scr, dec_scr, snap_scr,
     st_scr) = (next(it) for _ in range(10))

    lc = SSD_L
    hpg = C_HEADS // C_GROUPS
    gw = hpg * C_HEADDIM
    tb = xs_ref.shape[0]
    n_chunks = tb // lc
    n_ch = seq_len // lc
    xs_scr[...] = _silu(_dwconv(xs_ref[...], cwx_ref[...], cbx_ref[...], row_len))
    bm_scr[...] = _silu(_dwconv(bm_ref[...], cwb_ref[...], cbb_ref[...], row_len)).astype(BF16)
    cm_scr[...] = _silu(_dwconv(cm_ref[...], cwc_ref[...], cbc_ref[...], row_len)).astype(BF16)
    dt_scr[...] = jax.nn.softplus(dt_ref[...] + dtb_ref[...])
    a_row = -jnp.exp(a_ref[...])
    rr = lax.broadcasted_iota(jnp.int32, (lc, lc), 0)
    cc = lax.broadcasted_iota(jnp.int32, (lc, lc), 1)
    causal = (rr >= cc, cc >= rr)
    rep0 = hpg * lc

    def split2(x):
        hi = x.astype(BF16)
        return hi, (x - hi.astype(F32)).astype(BF16)

    def phase_a(n, carry):
        rows = pl.ds(pl.multiple_of(n * lc, lc), lc)
        dtc = dt_scr[rows, :]
        hi, mid = split2(dtc * a_row)
        c2 = _dot(tri_ref[...], jnp.concatenate([hi, mid], axis=1))
        cum2 = c2[:, :128] + c2[:, 128:]
        dhi, dmid = split2(dtc)
        xs = xs_scr[rows, :]
        bm = bm_scr[rows, :]
        cm = cm_scr[rows, :]
        scores = _dot_nt(cm, bm)
        y = dsk_ref[...] * xs
        for d in range(2):
            cum = cum2[d * lc:(d + 1) * lc]
            chi, cmid = split2(cum)
            r4 = _dot(jnp.concatenate([chi, cmid, dhi, dmid], axis=0), ecat_ref[d])
            rep = r4[0:lc] + r4[lc:2 * lc]
            rep64 = rep[:, rep0:]
            dtrep = r4[2 * lc:3 * lc, rep0:] + r4[3 * lc:, rep0:]
            cum_t = cum.T
            ps = []
            for hh in range(hpg):
                ln = d * hpg + hh
                seg = jnp.exp(jnp.where(causal[d], rep[:, lc * hh:lc * (hh + 1)] - cum_t[ln:ln + 1, :], NEG_BIG))
                ps.append((scores * seg).astype(BF16))
            xdt = xs * dtrep
            rhs = jnp.concatenate([(xdt * hmask_ref[hh]).astype(BF16) for hh in range(hpg)], axis=0)
            y = y + _dot(jnp.concatenate(ps, axis=1), rhs)
            end = rep64[lc - 1:lc] if d == 0 else rep64[0:1]
            u_scr[d, n] = _dot_tn(bm, (xdt * jnp.exp(end - rep64)).astype(BF16))
            dec_scr[d, n] = jnp.exp(end)
            dcy_scr[d, rows, :] = jnp.exp(rep64)
        y_scr[rows, :] = y
        return carry

    lax.fori_loop(0, n_chunks, phase_a, 0)

    for d in range(2):
        if chained:
            st_scr[d] = jnp.concatenate([s0_ref[d, hh] for hh in range(hpg)], axis=0).T
        else:
            st_scr[d] = jnp.zeros((C_STATE, gw), F32)

    def phase_b1(j, carry):
        nf = j
        nb = n_chunks - 1 - j
        sf = st_scr[0]
        sb = st_scr[1]
        if not chained:
            sf = jnp.where(nf % n_ch == 0, 0.0, sf)
            sb = jnp.where(nb % n_ch == n_ch - 1, 0.0, sb)
        snap_scr[0, nf] = sf.astype(BF16)
        snap_scr[1, nb] = sb.astype(BF16)
        sf = sf * dec_scr[0, nf] + u_scr[0, nf]
        sb = sb * dec_scr[1, nb] + u_scr[1, nb]
        st_scr[0] = sf
        st_scr[1] = sb
        if not chained:
            @pl.when(nf % n_ch == n_ch - 1)
            def _():
                sft = sf.T
                for hh in range(hpg):
                    sout_ref[nf // n_ch, 0, hh] = sft[hh * C_HEADDIM:(hh + 1) * C_HEADDIM]

            @pl.when(nb % n_ch == 0)
            def _():
                sbt = sb.T
                for hh in range(hpg):
                    sout_ref[nb // n_ch, 1, hh] = sbt[hh * C_HEADDIM:(hh + 1) * C_HEADDIM]
        return carry

    lax.fori_loop(0, n_chunks, phase_b1, 0)

    ng = ng_ref[...]

    def phase_b2(n, carry):
        rows = pl.ds(pl.multiple_of(n * lc, lc), lc)
        cm = cm_scr[rows, :]
        y = y_scr[rows, :] + _dot(cm, snap_scr[0, n]) * dcy_scr[0, rows, :]
        y = y + _dot(cm, snap_scr[1, n]) * dcy_scr[1, rows, :]
        y = y * _silu(z_ref[rows, :])
        y = y * lax.rsqrt(jnp.mean(y * y, axis=-1, keepdims=True) + EPS) * ng
        o_ref[rows, :] = y.astype(o_ref.dtype)
        return carry

    lax.fori_loop(0, n_chunks, phase_b2, 0, unroll=2)


def _ssd(proj, dtp, p, layer, blk0, n_blk, seq_len, row_len, state=None, prev=None, st_prev=None):
    tb = TOK_BLOCK
    chained = state is not None
    hpg = C_HEADS // C_GROUPS
    gw = hpg * C_HEADDIM
    col_x, col_b, col_c = COL_XBC, COL_XBC + C_INNER, COL_XBC + C_INNER + C_GROUPS * C_STATE
    lc = SSD_L
    tri = np.concatenate([np.tril(np.ones((lc, lc), np.float32)), np.triu(np.ones((lc, lc), np.float32))])
    ecat = np.zeros((2, 128, hpg * lc + gw), np.float32)
    hmask = np.zeros((hpg, 1, gw), np.float32)
    for hh in range(hpg):
        hmask[hh, 0, hh * C_HEADDIM:(hh + 1) * C_HEADDIM] = 1.0
        for d in range(2):
            ecat[d, d * hpg + hh, hh * lc:(hh + 1) * lc] = 1.0
            ecat[d, d * hpg + hh, hpg * lc + hh * C_HEADDIM:hpg * lc + (hh + 1) * C_HEADDIM] = 1.0
    in_specs = [pl.BlockSpec((tb, gw), lambda b, g: (b + blk0, col_x // gw + g)),
                pl.BlockSpec((tb, C_STATE), lambda b, g: (b + blk0, col_b // C_STATE + g)),
                pl.BlockSpec((tb, C_STATE), lambda b, g: (b + blk0, col_c // C_STATE + g)),
                pl.BlockSpec((tb, gw), lambda b, g: (b + blk0, COL_Z // gw + g)),
                pl.BlockSpec((tb, 128), lambda b, g: (b + blk0, g)),
                pl.BlockSpec((None, CONV_W, gw), lambda b, g: (layer, 0, g)),
                pl.BlockSpec((None, CONV_W, C_STATE), lambda b, g: (layer, 0, C_INNER // C_STATE + g)),
                pl.BlockSpec((None, CONV_W, C_STATE), lambda b, g: (layer, 0, C_INNER // C_STATE + C_GROUPS + g)),
                pl.BlockSpec((None, 1, gw), lambda b, g: (layer, 0, g)),
                pl.BlockSpec((None, 1, C_STATE), lambda b, g: (layer, 0, C_INNER // C_STATE + g)),
                pl.BlockSpec((None, 1, C_STATE), lambda b, g: (layer, 0, C_INNER // C_STATE + C_GROUPS + g)),
                pl.BlockSpec((None, None, 1, 128), lambda b, g: (layer, g, 0, 0)),
                pl.BlockSpec((None, None, 1, 128), lambda b, g: (layer, g, 0, 0)),
                pl.BlockSpec((None, 1, gw), lambda b, g: (layer, 0, g)),
                pl.BlockSpec((None, 1, gw), lambda b, g: (layer, 0, g)),
                pl.BlockSpec(tri.shape, lambda b, g: (0, 0)),
                pl.BlockSpec(ecat.shape, lambda b, g: (0, 0, 0)),
                pl.BlockSpec(hmask.shape, lambda b, g: (0, 0, 0))]
    cw = p["conv_c_w"]
    cbias = p["conv_c_b"].reshape(DEPTH, 1, -1)
    args = [proj, proj, proj, proj, dtp, cw, cw, cw, cbias, cbias, cbias,
            p["dt_bias_g"], p["a_log_g"], p["ssd_d_rep"], p["ssd_norm_g"].reshape(DEPTH, 1, C_INNER),
            jnp.asarray(tri, BF16), jnp.asarray(ecat, BF16), jnp.asarray(hmask)]
    out_specs = [pl.BlockSpec((tb, gw), lambda b, g: (b + blk0, g))]
    out_shape = [jax.ShapeDtypeStruct((proj.shape[0], C_INNER), BF16)]
    if chained:
        in_specs.append(pl.BlockSpec((None, None, 2, hpg, C_HEADDIM, C_STATE),
                                     lambda b, g: (b, layer, 0, g, 0, 0)))
        args.append(state)
    else:
        n_seq = tb // seq_len
        out_specs.append(pl.BlockSpec((n_seq, None, 2, hpg, C_HEADDIM, C_STATE),
                                      lambda b, g: (b, layer, 0, g, 0, 0)))
        out_shape.append(jax.ShapeDtypeStruct((n_blk * n_seq, DEPTH, 2, C_HEADS, C_HEADDIM, C_STATE), F32))
    aliases = _alias_out([(0, prev), (1, st_prev)], in_specs, args)
    res = pl.pallas_call(
        functools.partial(_ssd_kernel, seq_len=seq_len, row_len=row_len, chained=chained,
                          aliased=len(aliases)),
        grid=(n_blk, C_GROUPS),
        in_specs=in_specs, out_specs=out_specs, out_shape=out_shape,
        input_output_aliases=aliases,
        scratch_shapes=[pltpu.VMEM((tb, gw), F32), pltpu.VMEM((tb, C_STATE), BF16), pltpu.VMEM((tb, C_STATE), BF16),
                        pltpu.VMEM((tb, 128), F32), pltpu.VMEM((tb, gw), F32), pltpu.VMEM((2, tb, gw), F32),
                        pltpu.VMEM((2, tb // lc, C_STATE, gw), F32), pltpu.VMEM((2, tb // lc, 1, gw), F32),
                        pltpu.VMEM((2, tb // lc, C_STATE, gw), BF16), pltpu.VMEM((2, C_STATE, gw), F32)],
        compiler_params=_cp("arbitrary", "arbitrary"),
        name="ssd",
    )(*args)
    return res[0], (None if chained else res[1])


def _route_kernel(lg_ref, bias_ref, e_ref, w_ref):
    lg = lg_ref[...]
    ne = lg.shape[0]
    epg = ne // N_EXPERT_GROUPS
    mx = jnp.max(lg, axis=0, keepdims=True)
    ex = jnp.exp(lg - mx)
    probs = ex / jnp.sum(ex, axis=0, keepdims=True)
    sel = probs + bias_ref[...]
    rows = [sel[e:e + 1] for e in range(ne)]
    top2 = []
    for e in range(ne):
        g0 = (e // epg) * epg
        rank = jnp.zeros_like(rows[e])
        for o in range(g0, g0 + epg):
            if o == e:
                continue
            ahead = (rows[o] > rows[e]) | ((rows[o] == rows[e]) & (o < e))
            rank = rank + jnp.where(ahead, 1.0, 0.0)
        top2.append(rank < 1.5)
    score = []
    for g in range(N_EXPERT_GROUPS):
        sc = jnp.zeros_like(rows[0])
        for e in range(g * epg, (g + 1) * epg):
            sc = sc + jnp.where(top2[e], rows[e], 0.0)
        score.append(sc)
    best = []
    for g in range(N_EXPERT_GROUPS):
        ok = jnp.ones(rows[0].shape, jnp.bool_)
        for o in range(N_EXPERT_GROUPS):
            if o < g:
                ok = ok & (score[g] > score[o])
            elif o > g:
                ok = ok & (score[g] >= score[o])
        best.append(ok)
    first = jnp.full(rows[0].shape, float(ne), F32)
    second = jnp.full(rows[0].shape, -1.0, F32)
    p_first = jnp.zeros_like(rows[0])
    p_second = jnp.zeros_like(rows[0])
    for e in range(ne - 1, -1, -1):
        ch = top2[e] & best[e // epg]
        first = jnp.where(ch, float(e), first)
        p_first = jnp.where(ch, probs[e:e + 1], p_first)
    for e in range(ne):
        ch = top2[e] & best[e // epg]
        second = jnp.where(ch, float(e), second)
        p_second = jnp.where(ch, probs[e:e + 1], p_second)
    tot = p_first + p_second
    e_ref[0:1, :] = first.astype(jnp.int32)
    e_ref[1:2, :] = second.astype(jnp.int32)
    w_ref[0:1, :] = p_first / tot
    w_ref[1:2, :] = p_second / tot


def _route(logits_t, router_bias):
    ne, n = logits_t.shape
    return pl.pallas_call(
        _route_kernel,
        grid=(1,),
        in_specs=[pl.BlockSpec((ne, n), lambda i: (0, 0)), pl.BlockSpec((ne, 1), lambda i: (0, 0))],
        out_specs=[pl.BlockSpec((2, n), lambda i: (0, 0)), pl.BlockSpec((2, n), lambda i: (0, 0))],
        out_shape=[jax.ShapeDtypeStruct((2, n), jnp.int32), jax.ShapeDtypeStruct((2, n), F32)],
        compiler_params=_cp("arbitrary"),
        name="route",
    )(logits_t, router_bias.reshape(ne, 1))


def _dispatch_plan(e2, tm, n_tiles):
    n = e2.shape[1]
    e_flat = e2.reshape(-1)
    onehot = (e_flat[:, None] == jnp.arange(N_EXPERTS, dtype=jnp.int32)[None, :]).astype(jnp.int32)
    counts = jnp.sum(onehot, axis=0)
    rank = jnp.sum(jnp.cumsum(onehot, axis=0) * onehot, axis=1) - 1
    tiles_per = (counts + tm - 1) // tm
    tile_end = jnp.cumsum(tiles_per)
    starts = (tile_end - tiles_per) * tm
    dest = jnp.sum(onehot * starts[None, :], axis=1) + rank
    src = jnp.zeros((n_tiles * tm,), jnp.int32).at[dest].set(jnp.arange(2 * n, dtype=jnp.int32) % n)
    n_used = tile_end[-1]
    t_idx = jnp.minimum(jnp.arange(n_tiles, dtype=jnp.int32), n_used - 1)
    tile_expert = jnp.sum((tile_end[None, :] <= t_idx[:, None]).astype(jnp.int32), axis=1)
    tile_expert = jnp.minimum(tile_expert, N_EXPERTS - 1)
    te_hot = (tile_expert[:, None] == jnp.arange(N_EXPERTS, dtype=jnp.int32)[None, :]).astype(jnp.int32)
    first_tile = jnp.sum(te_hot * (tile_end - tiles_per)[None, :], axis=1)
    rows = jnp.sum(te_hot * counts[None, :], axis=1) - (jnp.arange(n_tiles, dtype=jnp.int32) - first_tile) * tm
    tile_groups = (jnp.clip(rows, 0, tm) + 7) // 8
    return dest, src, tile_expert, n_used.reshape(1).astype(jnp.int32), tile_groups.astype(jnp.int32)


def _moe_kernel(te_ref, nu_ref, src_ref, tg_ref, h_hbm, wg_ref, wu_ref, wd_ref, o_ref, xbuf, xb_scr, sem, *, tm):
    t, j = pl.program_id(0), pl.program_id(1)
    n_used = nu_ref[0]

    def row_copy(tile, slot, k):
        r = src_ref[tile * tm + k]
        return pltpu.make_async_copy(h_hbm.at[pl.ds(r, 1), :], xbuf.at[slot, pl.ds(k, 1), :], sem.at[slot])

    def issue(tile, slot):
        def body(g, c):
            for u in range(8):
                row_copy(tile, slot, g * 8 + u).start()
            return c
        lax.fori_loop(0, tg_ref[tile], body, 0)

    def drain(tile, slot):
        def body(g, c):
            for u in range(8):
                row_copy(tile, slot, g * 8 + u).wait()
            return c
        lax.fori_loop(0, tg_ref[tile], body, 0)

    @pl.when((j == 0) & (t == 0))
    def _():
        xbuf[...] = jnp.zeros_like(xbuf)
        issue(0, 0)

    @pl.when((j == 0) & (t + 1 < n_used))
    def _():
        issue(t + 1, (t + 1) % 2)

    @pl.when((j == 0) & (t < n_used))
    def _():
        drain(t, t % 2)
        xb_scr[...] = xbuf[t % 2].astype(BF16)

    @pl.when(j == 0)
    def _():
        o_ref[...] = jnp.zeros_like(o_ref)

    @pl.when(t < n_used)
    def _():
        x = xb_scr[...]
        hg = _dot(x, wg_ref[...].astype(BF16))
        hu = _dot(x, wu_ref[...].astype(BF16))
        act = (_silu(hg) * hu).astype(BF16)
        o_ref[...] += _dot(act, wd_ref[...].astype(BF16))


def _moe(h, src, tile_expert, n_used, tile_groups, w_gate, w_up, w_down, layer, tm=MOE_TM, tf=MOE_TF):
    d = h.shape[1]
    r = src.shape[0]
    n_tiles = r // tm
    nj = D_EXPERT // tf

    def jj(t, j, nu):
        return jnp.where(t < nu[0], j, nj - 1)

    grid_spec = pltpu.PrefetchScalarGridSpec(
        num_scalar_prefetch=4,
        grid=(n_tiles, nj),
        in_specs=[pl.BlockSpec(memory_space=pl.ANY),
                  pl.BlockSpec((None, None, d, tf), lambda t, j, te, nu, sr, tg: (layer, te[t], 0, jj(t, j, nu))),
                  pl.BlockSpec((None, None, d, tf), lambda t, j, te, nu, sr, tg: (layer, te[t], 0, jj(t, j, nu))),
                  pl.BlockSpec((None, None, tf, d), lambda t, j, te, nu, sr, tg: (layer, te[t], jj(t, j, nu), 0))],
        out_specs=pl.BlockSpec((tm, d), lambda t, j, te, nu, sr, tg: (t, 0)),
        scratch_shapes=[pltpu.VMEM((2, tm, d), F32), pltpu.VMEM((tm, d), BF16), pltpu.SemaphoreType.DMA((2,))],
    )
    return pl.pallas_call(
        functools.partial(_moe_kernel, tm=tm),
        grid_spec=grid_spec,
        out_shape=jax.ShapeDtypeStruct((r, d), F32),
        compiler_params=_cp("arbitrary", "arbitrary"),
        name="moe_experts",
    )(tile_expert, n_used, src, tile_groups, h, w_gate, w_up, w_down)


def _combine_kernel(*refs, tm, n, fuse_norm):
    it = iter(refs)
    dest_ref, x_ref, y_hbm, w_ref, ga_ref = (next(it) for _ in range(5))
    g_ref, sc_ref, sh_ref = (next(it), next(it), next(it)) if fuse_norm else (None, None, None)
    o_ref = next(it)
    h_ref = next(it) if fuse_norm else None
    ybuf, sem = next(it), next(it)
    i = pl.program_id(0)
    nt = pl.num_programs(0)

    def row_copy(tile, slot, s, k):
        r = dest_ref[s * n + tile * tm + k]
        return pltpu.make_async_copy(y_hbm.at[pl.ds(r, 1), :], ybuf.at[slot, s, pl.ds(k, 1), :], sem.at[slot])

    def issue(tile, slot):
        def body(k, c):
            row_copy(tile, slot, 0, k).start()
            row_copy(tile, slot, 1, k).start()
            return c
        lax.fori_loop(0, tm, body, 0, unroll=8)

    def drain(tile, slot):
        def body(k, c):
            row_copy(tile, slot, 0, k).wait()
            row_copy(tile, slot, 1, k).wait()
            return c
        lax.fori_loop(0, tm, body, 0, unroll=8)

    @pl.when(i == 0)
    def _():
        issue(0, 0)

    @pl.when(i + 1 < nt)
    def _():
        issue(i + 1, (i + 1) % 2)

    slot = i % 2
    drain(i, slot)
    w = w_ref[...]
    y = w[:, 0:1] * ybuf[slot, 0] + w[:, 1:2] * ybuf[slot, 1]
    xn = x_ref[...] + ga_ref[...] * y
    o_ref[...] = xn
    if fuse_norm:
        h = xn * lax.rsqrt(jnp.mean(xn * xn, axis=-1, keepdims=True) + EPS) * g_ref[...]
        h_ref[...] = (h * (1.0 + sc_ref[...]) + sh_ref[...]).astype(h_ref.dtype)


def _combine(x, y, dest, w2, mod_l, ga_chunk, next_norm=None, tm=256):
    n, d = x.shape
    row = pl.BlockSpec((tm, d), lambda i, dst: (i, 0))
    in_specs = [row, pl.BlockSpec(memory_space=pl.ANY), pl.BlockSpec((tm, 2), lambda i, dst: (i, 0)),
                pl.BlockSpec((None, 1, d), lambda i, dst: (_cond_row(i, tm), 0, ga_chunk))]
    args = [dest, x, y, w2, mod_l]
    out_specs = [row]
    out_shape = [jax.ShapeDtypeStruct((n, d), F32)]
    if next_norm is not None:
        g, mod_n, (sh_c, sc_c) = next_norm
        in_specs += [pl.BlockSpec((1, d), lambda i, dst: (0, 0)),
                     pl.BlockSpec((None, 1, d), lambda i, dst: (_cond_row(i, tm), 0, sc_c)),
                     pl.BlockSpec((None, 1, d), lambda i, dst: (_cond_row(i, tm), 0, sh_c))]
        args += [g.reshape(1, d), mod_n, mod_n]
        out_specs.append(row)
        out_shape.append(jax.ShapeDtypeStruct((n, d), BF16))
    grid_spec = pltpu.PrefetchScalarGridSpec(
        num_scalar_prefetch=1,
        grid=(n // tm,),
        in_specs=in_specs,
        out_specs=out_specs,
        scratch_shapes=[pltpu.VMEM((2, 2, tm, d), F32), pltpu.SemaphoreType.DMA((2,))],
    )
    res = pl.pallas_call(
        functools.partial(_combine_kernel, tm=tm, n=n, fuse_norm=next_norm is not None),
        grid_spec=grid_spec,
        out_shape=out_shape,
        compiler_params=_cp("arbitrary"),
        name="combine",
    )(*args)
    return (res[0], res[1]) if next_norm is not None else (res[0], None)


def _layer(x, h, l, mod, p, lbs, w_router, router_bias, states, new_states):
    n = x.shape[0]
    mod_l = mod[l]
    state_hgrn, state_rglru, state_ssd = states
    if h is None:
        h = _normmod(x, p["norm1_g"][l], mod_l, chunk=(0, 1))
    proj = _mm_t(h, p["w_in_t"], l, 0, N_MAIN)
    dtp = _mm_t(h, p["w_dt_t"], l, 0, C_GROUPS * 128)
    gates = _mm_t(h, p["w_in_t"], l, COL_MERGE, 3 * D_MODEL, sigmoid=True, out_dtype=BF16)
    nb = n // TOK_BLOCK
    ns = nb - N_CTX_BLOCKS
    hg_prev, ssd_prev = new_states
    oa, hg_new = _hgrn(proj, lbs[l], p["hgrn_norm_g"][l], 0, N_CTX_BLOCKS, SEQ, layer=l, st_prev=hg_prev)
    oa, _ = _hgrn(proj, lbs[l], p["hgrn_norm_g"][l], N_CTX_BLOCKS, ns, TOK_BLOCK, state_hgrn, l, prev=oa)
    ob, lru_new = _lru(proj, p, l, 0, N_CTX_BLOCKS, SEQ)
    ob, _ = _lru(proj, p, l, N_CTX_BLOCKS, ns, GRID_W, state_rglru, prev=ob)
    oc, ssd_new = _ssd(proj, dtp, p, l, 0, N_CTX_BLOCKS, SEQ, SEQ, st_prev=ssd_prev)
    oc, _ = _ssd(proj, dtp, p, l, N_CTX_BLOCKS, ns, TOK_BLOCK, GRID_W, state_ssd, prev=oc)
    merged = _merge(oa, ob, oc, p["w_branch_a"], p["w_branch_b"], p["w_branch_c"], gates, l)
    x = _mm_residual(merged, p["w_out"], l, x, mod_l, 2)
    h2, logits = _normmod(x, p["norm2_g"][l], mod_l, chunk=(3, 4), w_router=w_router, out_dtype=F32)
    e2, w2 = _route(logits.T, router_bias)
    n_tiles = (2 * n) // MOE_TM + N_EXPERTS
    dest, src, tile_expert, n_used, tile_groups = _dispatch_plan(e2, MOE_TM, n_tiles)
    y = _moe(h2, src, tile_expert, n_used, tile_groups, p["w_e_gate"], p["w_e_up"], p["w_e_down"], l)
    next_norm = (p["norm1_g"][l + 1], mod[l + 1], (0, 1)) if l + 1 < DEPTH else None
    x, h_next = _combine(x, y, dest, w2.T, mod_l, 5, next_norm)
    return x, h_next, hg_new, lru_new.transpose(1, 0, 2), ssd_new


def _hgrn_lower_bounds(lb_raw):
    pr = jax.nn.softmax(lb_raw.astype(F32), axis=0)
    cum = jnp.cumsum(pr, axis=0)
    return cum - cum[0]


def kernel(x_prompt, x_sample, state_hgrn, state_rglru, state_ssd, c, c_ctx, w_mod, b_mod, norm1_g, norm2_g, w_in, hgrn_lb, hgrn_norm_g, conv_b_w, conv_b_b, lru_wa, lru_ba, lru_wx, lru_bx, lru_lambda, conv_c_w, conv_c_b, ssd_a_log, ssd_dt_bias, ssd_d, ssd_norm_g, w_branch_a, w_branch_b, w_branch_c, w_out, w_router, router_bias, w_e_gate, w_e_up, w_e_down, final_g):
    bsz, seq, d = x_prompt.shape
    dbsz, dseq, _ = x_sample.shape
    assert seq == SEQ and dseq == TOK_BLOCK and (bsz * seq) == N_CTX_BLOCKS * TOK_BLOCK and d == D_MODEL
    x = jnp.concatenate([x_prompt.reshape(-1, d), x_sample.reshape(-1, d)], axis=0)

    cond8 = jnp.zeros((8, d), F32).at[0].set(c_ctx).at[1:1 + dbsz].set(c)
    mod = _modulation(cond8, w_mod, b_mod).reshape(DEPTH, 8, 1, N_MOD * d)

    hpg = C_HEADS // C_GROUPS
    w_in_t = jnp.swapaxes(w_in, 1, 2)
    w_dt_raw = w_in_t[:, COL_DT:COL_DT + 2 * C_HEADS].reshape(DEPTH, 2, C_GROUPS, hpg, d)
    w_dt_t = jnp.zeros((DEPTH, C_GROUPS, 128, d), F32).at[:, :, :2 * hpg].set(
        w_dt_raw.transpose(0, 2, 1, 3, 4).reshape(DEPTH, C_GROUPS, 2 * hpg, d)).reshape(DEPTH, C_GROUPS * 128, d)

    def group_lanes(a):
        g = a.reshape(DEPTH, 2, C_GROUPS, hpg).transpose(0, 2, 1, 3).reshape(DEPTH, C_GROUPS, 1, 2 * hpg)
        return jnp.zeros((DEPTH, C_GROUPS, 1, 128), F32).at[..., :2 * hpg].set(g)

    p = dict(norm1_g=norm1_g, norm2_g=norm2_g, w_in_t=w_in_t, w_dt_t=w_dt_t, hgrn_norm_g=hgrn_norm_g,
             conv_b_w=conv_b_w, conv_b_b=conv_b_b, lru_wa=lru_wa, lru_ba=lru_ba, lru_wx=lru_wx,
             lru_bx=lru_bx, lru_lambda=lru_lambda, conv_c_w=conv_c_w, conv_c_b=conv_c_b,
             dt_bias_g=group_lanes(ssd_dt_bias), a_log_g=group_lanes(ssd_a_log),
             ssd_d_rep=jnp.repeat(ssd_d, C_HEADDIM, axis=-1).reshape(DEPTH, 1, C_INNER),
             ssd_norm_g=ssd_norm_g, w_branch_a=w_branch_a, w_branch_b=w_branch_b, w_branch_c=w_branch_c,
             w_out=w_out, w_e_gate=w_e_gate, w_e_up=w_e_up, w_e_down=w_e_down)
    lbs = _hgrn_lower_bounds(hgrn_lb)

    h = hg_new = ssd_new = None
    lru_list = []
    for l in range(DEPTH):
        x, h, hg_new, lru_new, ssd_new = _layer(x, h, l, mod, p, lbs, w_router, router_bias,
                                                (state_hgrn, state_rglru, state_ssd), (hg_new, ssd_new))
        lru_list.append(lru_new)
    n_ctx = bsz * seq
    y_prompt = _normmod(x, final_g, out_dtype=F32, row0=0, n_rows=n_ctx).reshape(bsz, seq, d)
    y_sample = _normmod(x, final_g, out_dtype=F32, row0=n_ctx, n_rows=dbsz * dseq).reshape(dbsz, dseq, d)
    return (y_prompt, y_sample, hg_new, jnp.stack(lru_list, axis=1), ssd_new)
```

```python
import functools
import math

import numpy as np
import jax
import jax.numpy as jnp
from jax import lax
from jax.experimental import pallas as pl
from jax.experimental.pallas import tpu as pltpu

F32 = jnp.float32
BF16 = jnp.bfloat16
HIGHEST = lax.Precision.HIGHEST

EPS = 1e-6
D_MODEL = 2048
DEPTH = 4
N_MOD = 6
GRID_W = 64
SEQ = 256
TOK_BLOCK = 2048
N_CTX_BLOCKS = 2

A_HEADS = 8
A_DK = 128
A_DV = 128
A_WIDTH = 1024
HGRN_C = 128

B_WIDTH = 1024
B_BLOCK = 128
LRU_C = 8.0
LRU_SEG = 256

C_INNER = 1024
C_HEADDIM = 64
C_HEADS = 16
C_GROUPS = 4
C_STATE = 128
SSD_L = 128
CONV_W = 4

N_EXPERTS = 16
N_EXPERT_GROUPS = 4
D_EXPERT = 1024
MOE_TM = 1024
MOE_TF = 256

COL_Q, COL_FF, COL_FB, COL_V, COL_GA = 0, 1024, 2048, 3072, 4096
COL_XB, COL_GB, COL_Z, COL_XBC, COL_DT, COL_MERGE = 5120, 6144, 7168, 8192, 10240, 10272
N_MAIN = 10240

VMEM_LIMIT_BYTES = 56 * 1024 * 1024
NEG_BIG = -1e30


def _cp(*sem):
    return pltpu.CompilerParams(dimension_semantics=sem, vmem_limit_bytes=VMEM_LIMIT_BYTES)


def _silu(x):
    return x * jax.nn.sigmoid(x)


def _dot(a, b):
    return jnp.dot(a, b, preferred_element_type=F32)


def _dot_nt(a, b):
    return lax.dot_general(a, b, (((1,), (1,)), ((), ())), preferred_element_type=F32)


def _dot_tn(a, b):
    return lax.dot_general(a, b, (((0,), (0,)), ((), ())), preferred_element_type=F32)


def _dot_hi(a, b):
    return jnp.dot(a, b, preferred_element_type=F32, precision=HIGHEST)


def _cond_row(i, tm):
    return jnp.maximum((i * tm) // TOK_BLOCK - (N_CTX_BLOCKS - 1), 0)


def _mod_kernel(c_ref, w_ref, b_ref, o_ref):
    s = _silu(c_ref[...]).astype(BF16)
    o_ref[...] = _dot(s, w_ref[...].astype(BF16)) + b_ref[...]


def _modulation(cond8, w_mod, b_mod):
    depth, d, n = w_mod.shape
    tn = 1024
    return pl.pallas_call(
        _mod_kernel,
        grid=(depth, n // tn),
        in_specs=[pl.BlockSpec((8, d), lambda l, j: (0, 0)),
                  pl.BlockSpec((None, d, tn), lambda l, j: (l, 0, j)),
                  pl.BlockSpec((None, 1, tn), lambda l, j: (l, 0, j))],
        out_specs=pl.BlockSpec((None, 8, tn), lambda l, j: (l, 0, j)),
        out_shape=jax.ShapeDtypeStruct((depth, 8, n), F32),
        compiler_params=_cp("arbitrary", "arbitrary"),
        name="modulation",
    )(cond8, w_mod, b_mod.reshape(depth, 1, n))


def _normmod_kernel(*refs, modulate, router):
    it = iter(refs)
    x_ref, g_ref = next(it), next(it)
    sc_ref = sh_ref = wr_ref = lg_ref = None
    if modulate:
        sc_ref, sh_ref = next(it), next(it)
    if router:
        wr_ref = next(it)
    h_ref = next(it)
    if router:
        lg_ref = next(it)
    x = x_ref[...]
    h = x * lax.rsqrt(jnp.mean(x * x, axis=-1, keepdims=True) + EPS) * g_ref[...]
    if modulate:
        h = h * (1.0 + sc_ref[...]) + sh_ref[...]
    h_ref[...] = h.astype(h_ref.dtype)
    if router:
        lg_ref[...] = _dot_hi(h, wr_ref[...])


def _normmod(x, g, mod_l=None, chunk=None, w_router=None, out_dtype=BF16, tm=512, row0=0, n_rows=None):
    d = x.shape[1]
    n = x.shape[0] if n_rows is None else n_rows
    r0 = row0 // tm
    modulate = mod_l is not None
    router = w_router is not None
    assert not (modulate and row0)
    in_specs = [pl.BlockSpec((tm, d), lambda i: (i + r0, 0)),
                pl.BlockSpec((1, d), lambda i: (0, 0))]
    args = [x, g.reshape(1, d)]
    if modulate:
        sh_c, sc_c = chunk
        in_specs += [pl.BlockSpec((None, 1, d), lambda i: (_cond_row(i, tm), 0, sc_c)),
                     pl.BlockSpec((None, 1, d), lambda i: (_cond_row(i, tm), 0, sh_c))]
        args += [mod_l, mod_l]
    out_specs = [pl.BlockSpec((tm, d), lambda i: (i, 0))]
    out_shape = [jax.ShapeDtypeStruct((n, d), out_dtype)]
    if router:
        ne = w_router.shape[1]
        in_specs.append(pl.BlockSpec((d, ne), lambda i: (0, 0)))
        args.append(w_router)
        out_specs.append(pl.BlockSpec((tm, ne), lambda i: (i, 0)))
        out_shape.append(jax.ShapeDtypeStruct((n, ne), F32))
    res = pl.pallas_call(
        functools.partial(_normmod_kernel, modulate=modulate, router=router),
        grid=(n // tm,),
        in_specs=in_specs, out_specs=out_specs, out_shape=out_shape,
        compiler_params=_cp("arbitrary"),
        name="normmod",
    )(*args)
    return res if router else res[0]


def _mm_kernel(a_ref, w_ref, o_ref, *, sigmoid):
    acc = _dot_nt(a_ref[...], w_ref[0].astype(BF16))
    if sigmoid:
        acc = jax.nn.sigmoid(acc)
    o_ref[...] = acc.astype(o_ref.dtype)


def _mm_t(a, wt, layer, row0, n_out, sigmoid=False, out_dtype=F32, tm=2048, tn=512):
    m, k = a.shape
    tn = min(tn, n_out)
    assert row0 % 8 == 0 and n_out % tn == 0 and m % tm == 0
    return pl.pallas_call(
        functools.partial(_mm_kernel, sigmoid=sigmoid),
        grid=(m // tm, n_out // tn),
        in_specs=[pl.BlockSpec((tm, k), lambda i, j: (i, 0)),
                  pl.BlockSpec((pl.Element(1), pl.Element(tn), pl.Element(k)),
                               lambda i, j: (layer, pl.multiple_of(row0 + j * tn, 8), 0))],
        out_specs=pl.BlockSpec((tm, tn), lambda i, j: (i, j)),
        out_shape=jax.ShapeDtypeStruct((m, n_out), out_dtype),
        compiler_params=_cp("arbitrary", "arbitrary"),
        name="matmul",
    )(a, wt)


def _mm_res_kernel(a_ref, w_ref, x_ref, ga_ref, o_ref):
    acc = _dot(a_ref[...], w_ref[...].astype(BF16))
    o_ref[...] = x_ref[...] + ga_ref[...] * acc


def _mm_residual(a, w, layer, x, mod_l, ga_chunk, tm=1024, tn=512):
    m, k = a.shape
    n = w.shape[-1]
    gs = n // tn
    return pl.pallas_call(
        _mm_res_kernel,
        grid=(m // tm, n // tn),
        in_specs=[pl.BlockSpec((tm, k), lambda i, j: (i, 0)),
                  pl.BlockSpec((None, k, tn), lambda i, j: (layer, 0, j)),
                  pl.BlockSpec((tm, tn), lambda i, j: (i, j)),
                  pl.BlockSpec((None, 1, tn), lambda i, j: (_cond_row(i, tm), 0, ga_chunk * gs + j))],
        out_specs=pl.BlockSpec((tm, tn), lambda i, j: (i, j)),
        out_shape=jax.ShapeDtypeStruct((m, n), F32),
        compiler_params=_cp("arbitrary", "arbitrary"),
        name="out_proj",
    )(a, w, x, mod_l)


def _merge_kernel(oa_ref, ob_ref, oc_ref, wa_ref, wb_ref, wc_ref, g1_ref, g2_ref, g3_ref, o_ref):
    ya = _dot(oa_ref[...], wa_ref[...].astype(BF16))
    yb = _dot(ob_ref[...], wb_ref[...].astype(BF16))
    yc = _dot(oc_ref[...], wc_ref[...].astype(BF16))
    m = (g1_ref[...].astype(F32) * ya + g2_ref[...].astype(F32) * yb + g3_ref[...].astype(F32) * yc)
    o_ref[...] = m.astype(o_ref.dtype)


def _merge(oa, ob, oc, wa, wb, wc, gates, layer, tm=1024, tn=512):
    m, k = oa.shape
    n = wa.shape[-1]
    gs = n // tn
    a_spec = pl.BlockSpec((tm, k), lambda i, j: (i, 0))
    w_spec = pl.BlockSpec((None, k, tn), lambda i, j: (layer, 0, j))
    return pl.pallas_call(
        _merge_kernel,
        grid=(m // tm, n // tn),
        in_specs=[a_spec, a_spec, a_spec, w_spec, w_spec, w_spec,
                  pl.BlockSpec((tm, tn), lambda i, j: (i, j)),
                  pl.BlockSpec((tm, tn), lambda i, j: (i, j + gs)),
                  pl.BlockSpec((tm, tn), lambda i, j: (i, j + 2 * gs))],
        out_specs=pl.BlockSpec((tm, tn), lambda i, j: (i, j)),
        out_shape=jax.ShapeDtypeStruct((m, n), BF16),
        compiler_params=_cp("arbitrary", "arbitrary"),
        name="merge",
    )(oa, ob, oc, wa, wb, wc, gates, gates, gates)


def _dwconv(x, w, b, row_len):
    t = x.shape[0]
    pos = lax.broadcasted_iota(jnp.int32, x.shape, 0) % row_len
    xm2 = jnp.where(pos >= 2, pltpu.roll(x, 2, axis=0), 0.0)
    xm1 = jnp.where(pos >= 1, pltpu.roll(x, 1, axis=0), 0.0)
    xp1 = jnp.where(pos <= row_len - 2, pltpu.roll(x, t - 1, axis=0), 0.0)
    y = b + xm2 * w[0:1]
    y = y + xm1 * w[1:2]
    y = y + x * w[2:3]
    y = y + xp1 * w[3:4]
    return y


def _hgrn_consts(c):
    nl = int(math.log2(c))
    seg = np.zeros((2, nl + 1, c, c), np.float32)
    msk = np.zeros((2, nl + 1, c, c), np.float32)
    up = np.zeros((2, nl, c, 128), np.float32)
    for d in range(2):
        tt = np.arange(c) if d == 0 else c - 1 - np.arange(c)
        tr, tc = tt[:, None], tt[None, :]
        seg[d, 0] = tc <= tr
        for l in range(nl):
            s = 1 << l
            blk, upper = tt // (2 * s), (tt % (2 * s)) >= s
            mid = (blk * 2 * s + s)[:, None]
            seg_u = (tc >= mid) & (tc <= tr)
            seg_l = (tc >= tr + 1) & (tc <= mid - 1)
            seg[d, l + 1] = np.where(upper[:, None], seg_u, seg_l)
            msk[d, l] = upper[:, None] & (~upper)[None, :] & (blk[:, None] == blk[None, :])
            up[d, l] = upper[:, None]
        msk[d, nl] = np.eye(c)
    return seg.reshape(2, (nl + 1) * c, c), msk, up


def _hgrn_kernel(*refs, seq_len, chained, aliased):
    it = iter(refs)
    q_ref, ff_ref, fb_ref, v_ref, ga_ref, lb_ref, ng_ref = (next(it) for _ in range(7))
    seg_ref, msk_ref, up_ref = next(it), next(it), next(it)
    s0_ref = next(it) if chained else None
    for _ in range(aliased):
        next(it)
    o_ref = next(it)
    sout_ref = None if chained else next(it)
    o_scr, qp_scr, u_scr, dec_scr, snap_scr = (next(it) for _ in range(5))

    c = HGRN_C
    nl = int(math.log2(c))
    tb = q_ref.shape[0]
    n_chunks = tb // c
    n_ch = seq_len // c
    lb = lb_ref[...]
    f_refs = (ff_ref, fb_ref)

    def phase_a(n, carry):
        rows = pl.ds(pl.multiple_of(n * c, c), c)
        q = _silu(q_ref[rows, :]) * (A_DK ** -0.5)
        qb = q.astype(BF16)
        v = v_ref[rows, :].astype(BF16)
        o = jnp.zeros((c, A_DV), F32)
        qps, kps, decs = [], [], []
        for d in range(2):
            f = lb + (1.0 - lb) * jax.nn.sigmoid(f_refs[d][rows, :])
            k = 1.0 - f
            lf = jnp.log(f)
            hi = lf.astype(BF16)
            mid = (lf - hi.astype(F32)).astype(BF16)
            e2 = _dot(seg_ref[d], jnp.concatenate([hi, mid], axis=1))
            e = e2[:, :A_DK] + e2[:, A_DK:]
            b = e[0:c]
            att = msk_ref[d, nl] * _dot_nt(qb, k.astype(BF16))
            for l in range(nl):
                x = (jnp.where(up_ref[d, l] > 0.5, q, k) * jnp.exp(e[(l + 1) * c:(l + 2) * c])).astype(BF16)
                att = att + msk_ref[d, l] * _dot_nt(x, x)
            o = o + _dot(att.astype(BF16), v)
            b_end = b[c - 1:c] if d == 0 else b[0:1]
            qps.append((q * jnp.exp(b)).astype(BF16))
            kps.append((k * jnp.exp(b_end - b)).astype(BF16))
            decs.append(jnp.exp(b_end))
        o_scr[rows, :] = o
        qp_scr[rows, :] = jnp.concatenate(qps, axis=1)
        u_scr[n] = _dot_tn(v, jnp.concatenate(kps, axis=1))
        dec_scr[n] = jnp.concatenate(decs, axis=1)
        return carry

    lax.fori_loop(0, n_chunks, phase_a, 0)

    def init_state(d):
        return s0_ref[d].T if chained else jnp.zeros((A_DV, A_DK), F32)

    def phase_b1(j, carry):
        sf, sb = carry
        nf = j
        nb = n_chunks - 1 - j
        if not chained:
            sf = jnp.where(nf % n_ch == 0, 0.0, sf)
            sb = jnp.where(nb % n_ch == n_ch - 1, 0.0, sb)
        snap_scr[nf, :, 0:A_DK] = sf.astype(BF16)
        snap_scr[nb, :, A_DK:2 * A_DK] = sb.astype(BF16)
        sf = sf * dec_scr[nf][:, 0:A_DK] + u_scr[nf][:, 0:A_DK]
        sb = sb * dec_scr[nb][:, A_DK:2 * A_DK] + u_scr[nb][:, A_DK:2 * A_DK]
        if not chained:
            @pl.when(nf % n_ch == n_ch - 1)
            def _():
                sout_ref[nf // n_ch, 0] = sf.T

            @pl.when(nb % n_ch == 0)
            def _():
                sout_ref[nb // n_ch, 1] = sb.T
        return sf, sb

    lax.fori_loop(0, n_chunks, phase_b1, (init_state(0), init_state(1)))

    ng = ng_ref[...]

    def phase_b2(n, carry):
        rows = pl.ds(pl.multiple_of(n * c, c), c)
        o = o_scr[rows, :] + _dot_nt(qp_scr[rows, :], snap_scr[n])
        o = o * lax.rsqrt(jnp.mean(o * o, axis=-1, keepdims=True) + EPS) * ng
        o_ref[rows, :] = (o * _silu(ga_ref[rows, :])).astype(o_ref.dtype)
        return carry

    lax.fori_loop(0, n_chunks, phase_b2, 0, unroll=4)


def _alias_out(prevs, in_specs, args):
    aliases = {}
    for out_idx, prev in prevs:
        if prev is not None:
            in_specs.append(pl.BlockSpec(memory_space=pl.ANY))
            args.append(prev)
            aliases[len(args) - 1] = out_idx
    return aliases


def _hgrn(proj, lb_l, ng_l, blk0, n_blk, seq_len, state=None, layer=0, prev=None, st_prev=None):
    tb = TOK_BLOCK
    chained = state is not None
    seg, msk, up = _hgrn_consts(HGRN_C)
    cb = lambda col: col // A_DK

    def col_spec(col):
        return pl.BlockSpec((tb, A_DK), lambda b, h: (b + blk0, cb(col) + h))

    def full(a):
        nd = a.ndim
        return pl.BlockSpec(a.shape, lambda b, h: (0,) * nd)

    in_specs = [col_spec(COL_Q), col_spec(COL_FF), col_spec(COL_FB), col_spec(COL_V), col_spec(COL_GA),
                pl.BlockSpec((1, A_DK), lambda b, h: (0, h)),
                pl.BlockSpec((1, A_DV), lambda b, h: (0, 0)),
                full(seg), full(msk), full(up)]
    args = [proj, proj, proj, proj, proj, lb_l.reshape(1, -1), ng_l.reshape(1, -1),
            jnp.asarray(seg, BF16), jnp.asarray(msk), jnp.asarray(up)]
    out_specs = [pl.BlockSpec((tb, A_DV), lambda b, h: (b + blk0, h))]
    out_shape = [jax.ShapeDtypeStruct((proj.shape[0], A_WIDTH), BF16)]
    if chained:
        in_specs.append(pl.BlockSpec((None, None, 2, None, A_DK, A_DV), lambda b, h: (b, layer, 0, h, 0, 0)))
        args.append(state)
    else:
        n_seq = tb // seq_len
        out_specs.append(pl.BlockSpec((n_seq, None, 2, None, A_DK, A_DV), lambda b, h: (b, layer, 0, h, 0, 0)))
        out_shape.append(jax.ShapeDtypeStruct((n_blk * n_seq, DEPTH, 2, A_HEADS, A_DK, A_DV), F32))
    aliases = _alias_out([(0, prev), (1, st_prev)], in_specs, args)
    n_chunks = tb // HGRN_C
    res = pl.pallas_call(
        functools.partial(_hgrn_kernel, seq_len=seq_len, chained=chained, aliased=len(aliases)),
        grid=(n_blk, A_HEADS),
        in_specs=in_specs, out_specs=out_specs, out_shape=out_shape,
        input_output_aliases=aliases,
        scratch_shapes=[pltpu.VMEM((tb, A_DV), F32), pltpu.VMEM((tb, 2 * A_DK), BF16),
                        pltpu.VMEM((n_chunks, A_DV, 2 * A_DK), F32), pltpu.VMEM((n_chunks, 1, 2 * A_DK), F32),
                        pltpu.VMEM((n_chunks, A_DV, 2 * A_DK), BF16)],
        compiler_params=_cp("arbitrary", "arbitrary"),
        name="hgrn2",
    )(*args)
    return res[0], (None if chained else res[1])


def _lru_kernel(*refs, row_len, chained, aliased):
    it = iter(refs)
    x_ref, gb_ref, cw_ref, cb_ref, wa_ref, ba_ref, wx_ref, bx_ref, lam_ref = (next(it) for _ in range(9))
    h0_ref = next(it) if chained else None
    for _ in range(aliased):
        next(it)
    o_ref = next(it)
    hout_ref = None if chained else next(it)
    a_scr, u_scr, h_scr, p_scr = (next(it) for _ in range(4))

    tb = x_ref.shape[0]
    seg = LRU_SEG
    n_seg = tb // seg
    xc = _dwconv(x_ref[...], cw_ref[...], cb_ref[...], row_len)
    xcb = xc.astype(BF16)
    for d in range(2):
        r = jax.nn.sigmoid(_dot(xcb, wa_ref[d].astype(BF16)) + ba_ref[d])
        g = jax.nn.sigmoid(_dot(xcb, wx_ref[d].astype(BF16)) + bx_ref[d])
        log_a = LRU_C * r * jax.nn.log_sigmoid(lam_ref[d])
        a = jnp.exp(log_a)
        a_scr[d] = a
        u_scr[d] = jnp.sqrt(1.0 - a * a) * (g * xc)

    def step(i, carry):
        hf, pf, hb, pb = carry
        tf = i
        tr = seg - 1 - i
        af = a_scr[0, pl.ds(tf, n_seg, stride=seg), :]
        hf = af * hf + u_scr[0, pl.ds(tf, n_seg, stride=seg), :]
        h_scr[0, pl.ds(tf, n_seg, stride=seg), :] = hf
        ab = a_scr[1, pl.ds(tr, n_seg, stride=seg), :]
        hb = ab * hb + u_scr[1, pl.ds(tr, n_seg, stride=seg), :]
        h_scr[1, pl.ds(tr, n_seg, stride=seg), :] = hb
        if chained:
            pf = af * pf
            pb = ab * pb
            p_scr[0, pl.ds(tf, n_seg, stride=seg), :] = pf
            p_scr[1, pl.ds(tr, n_seg, stride=seg), :] = pb
        return hf, pf, hb, pb

    zeros = jnp.zeros((n_seg, B_BLOCK), F32)
    ones = jnp.ones((n_seg, B_BLOCK), F32)
    hf, pf, hb, pb = lax.fori_loop(0, seg, step, (zeros, ones, zeros, ones), unroll=8)

    if chained:
        hin = h0_ref[0:1, :]
        for s in range(n_seg):
            rows = pl.ds(s * seg, seg)
            h_scr[0, rows, :] = h_scr[0, rows, :] + p_scr[0, rows, :] * hin
            hin = hf[s:s + 1] + pf[s:s + 1] * hin
        hin = h0_ref[1:2, :]
        for s in range(n_seg - 1, -1, -1):
            rows = pl.ds(s * seg, seg)
            h_scr[1, rows, :] = h_scr[1, rows, :] + p_scr[1, rows, :] * hin
            hin = hb[s:s + 1] + pb[s:s + 1] * hin
    else:
        hout_ref[0] = hf
        hout_ref[1] = hb
    o_ref[...] = ((h_scr[0] + h_scr[1]) * jax.nn.gelu(gb_ref[...])).astype(o_ref.dtype)


def _lru(proj, p, layer, blk0, n_blk, row_len, state=None, prev=None):
    tb = TOK_BLOCK
    chained = state is not None
    nb = B_WIDTH // B_BLOCK
    cbx, cbg = COL_XB // B_BLOCK, COL_GB // B_BLOCK
    in_specs = [pl.BlockSpec((tb, B_BLOCK), lambda b, n: (b + blk0, cbx + n)),
                pl.BlockSpec((tb, B_BLOCK), lambda b, n: (b + blk0, cbg + n)),
                pl.BlockSpec((None, CONV_W, B_BLOCK), lambda b, n: (layer, 0, n)),
                pl.BlockSpec((None, 1, B_BLOCK), lambda b, n: (layer, 0, n)),
                pl.BlockSpec((None, 2, None, B_BLOCK, B_BLOCK), lambda b, n: (layer, 0, n, 0, 0)),
                pl.BlockSpec((None, 2, 1, B_BLOCK), lambda b, n: (layer, 0, 0, n)),
                pl.BlockSpec((None, 2, None, B_BLOCK, B_BLOCK), lambda b, n: (layer, 0, n, 0, 0)),
                pl.BlockSpec((None, 2, 1, B_BLOCK), lambda b, n: (layer, 0, 0, n)),
                pl.BlockSpec((None, 2, 1, B_BLOCK), lambda b, n: (layer, 0, 0, n))]
    d4 = lambda a: a.reshape(DEPTH, 2, 1, B_WIDTH)
    args = [proj, proj, p["conv_b_w"], p["conv_b_b"].reshape(DEPTH, 1, B_WIDTH),
            p["lru_wa"], d4(p["lru_ba"]), p["lru_wx"], d4(p["lru_bx"]), d4(p["lru_lambda"])]
    out_specs = [pl.BlockSpec((tb, B_BLOCK), lambda b, n: (b + blk0, n))]
    out_shape = [jax.ShapeDtypeStruct((proj.shape[0], B_WIDTH), BF16)]
    if chained:
        in_specs.append(pl.BlockSpec((None, None, 2, B_BLOCK), lambda b, n: (b, layer, 0, n)))
        args.append(state)
    else:
        n_seq = tb // LRU_SEG
        out_specs.append(pl.BlockSpec((2, n_seq, B_BLOCK), lambda b, n: (0, b, n)))
        out_shape.append(jax.ShapeDtypeStruct((2, n_blk * n_seq, B_WIDTH), F32))
    aliases = _alias_out([(0, prev)], in_specs, args)
    res = pl.pallas_call(
        functools.partial(_lru_kernel, row_len=row_len, chained=chained, aliased=len(aliases)),
        grid=(n_blk, nb),
        in_specs=in_specs, out_specs=out_specs, out_shape=out_shape,
        input_output_aliases=aliases,
        scratch_shapes=[pltpu.VMEM((2, tb, B_BLOCK), F32) for _ in range(4)],
        compiler_params=_cp("arbitrary", "arbitrary"),
        name="rglru",
    )(*args)
    return res[0], (None if chained else res[1])


def _ssd_kernel(*refs, seq_len, row_len, chained, aliased):
    it = iter(refs)
    (xs_ref, bm_ref, cm_ref, z_ref, dt_ref, cwx_ref, cwb_ref, cwc_ref, cbx_ref, cbb_ref, cbc_ref,
     dtb_ref, a_ref, dsk_ref, ng_ref, tri_ref, ecat_ref, hmask_ref) = (next(it) for _ in range(18))
    s0_ref = next(it) if chained else None
    for _ in range(aliased):
        next(it)
    o_ref = next(it)
    sout_ref = None if chained else next(it)
    (xs_scr, bm_scr, cm_scr, dt_scr, y_scr, dcy_scr, u_scr, dec_scr, snap_scr,
     st_scr) = (next(it) for _ in range(10))

    lc = SSD_L
    hpg = C_HEADS // C_GROUPS
    gw = hpg * C_HEADDIM
    tb = xs_ref.shape[0]
    n_chunks = tb // lc
    n_ch = seq_len // lc
    xs_scr[...] = _silu(_dwconv(xs_ref[...], cwx_ref[...], cbx_ref[...], row_len))
    bm_scr[...] = _silu(_dwconv(bm_ref[...], cwb_ref[...], cbb_ref[...], row_len)).astype(BF16)
    cm_scr[...] = _silu(_dwconv(cm_ref[...], cwc_ref[...], cbc_ref[...], row_len)).astype(BF16)
    dt_scr[...] = jax.nn.softplus(dt_ref[...] + dtb_ref[...])
    a_row = -jnp.exp(a_ref[...])
    rr = lax.broadcasted_iota(jnp.int32, (lc, lc), 0)
    cc = lax.broadcasted_iota(jnp.int32, (lc, lc), 1)
    causal = (rr >= cc, cc >= rr)
    rep0 = hpg * lc

    def split2(x):
        hi = x.astype(BF16)
        return hi, (x - hi.astype(F32)).astype(BF16)

    def phase_a(n, carry):
        rows = pl.ds(pl.multiple_of(n * lc, lc), lc)
        dtc = dt_scr[rows, :]
        hi, mid = split2(dtc * a_row)
        c2 = _dot(tri_ref[...], jnp.concatenate([hi, mid], axis=1))
        cum2 = c2[:, :128] + c2[:, 128:]
        dhi, dmid = split2(dtc)
        xs = xs_scr[rows, :]
        bm = bm_scr[rows, :]
        cm = cm_scr[rows, :]
        scores = _dot_nt(cm, bm)
        y = dsk_ref[...] * xs
        for d in range(2):
            cum = cum2[d * lc:(d + 1) * lc]
            chi, cmid = split2(cum)
            r4 = _dot(jnp.concatenate([chi, cmid, dhi, dmid], axis=0), ecat_ref[d])
            rep = r4[0:lc] + r4[lc:2 * lc]
            rep64 = rep[:, rep0:]
            dtrep = r4[2 * lc:3 * lc, rep0:] + r4[3 * lc:, rep0:]
            cum_t = cum.T
            ps = []
            for hh in range(hpg):
                ln = d * hpg + hh
                seg = jnp.exp(jnp.where(causal[d], rep[:, lc * hh:lc * (hh + 1)] - cum_t[ln:ln + 1, :], NEG_BIG))
                ps.append((scores * seg).astype(BF16))
            xdt = xs * dtrep
            rhs = jnp.concatenate([(xdt * hmask_ref[hh]).astype(BF16) for hh in range(hpg)], axis=0)
            y = y + _dot(jnp.concatenate(ps, axis=1), rhs)
            end = rep64[lc - 1:lc] if d == 0 else rep64[0:1]
            u_scr[d, n] = _dot_tn(bm, (xdt * jnp.exp(end - rep64)).astype(BF16))
            dec_scr[d, n] = jnp.exp(end)
            dcy_scr[d, rows, :] = jnp.exp(rep64)
        y_scr[rows, :] = y
        return carry

    lax.fori_loop(0, n_chunks, phase_a, 0)

    for d in range(2):
        if chained:
            st_scr[d] = jnp.concatenate([s0_ref[d, hh] for hh in range(hpg)], axis=0).T
        else:
            st_scr[d] = jnp.zeros((C_STATE, gw), F32)

    def phase_b1(j, carry):
        nf = j
        nb = n_chunks - 1 - j
        sf = st_scr[0]
        sb = st_scr[1]
        if not chained:
            sf = jnp.where(nf % n_ch == 0, 0.0, sf)
            sb = jnp.where(nb % n_ch == n_ch - 1, 0.0, sb)
        snap_scr[0, nf] = sf.astype(BF16)
        snap_scr[1, nb] = sb.astype(BF16)
        sf = sf * dec_scr[0, nf] + u_scr[0, nf]
        sb = sb * dec_scr[1, nb] + u_scr[1, nb]
        st_scr[0] = sf
        st_scr[1] = sb
        if not chained:
            @pl.when(nf % n_ch == n_ch - 1)
            def _():
                sft = sf.T
                for hh in range(hpg):
                    sout_ref[nf // n_ch, 0, hh] = sft[hh * C_HEADDIM:(hh + 1) * C_HEADDIM]

            @pl.when(nb % n_ch == 0)
            def _():
                sbt = sb.T
                for hh in range(hpg):
                    sout_ref[nb // n_ch, 1, hh] = sbt[hh * C_HEADDIM:(hh + 1) * C_HEADDIM]
        return carry

    lax.fori_loop(0, n_chunks, phase_b1, 0)

    ng = ng_ref[...]

    def phase_b2(n, carry):
        rows = pl.ds(pl.multiple_of(n * lc, lc), lc)
        cm = cm_scr[rows, :]
        y = y_scr[rows, :] + _dot(cm, snap_scr[0, n]) * dcy_scr[0, rows, :]
        y = y + _dot(cm, snap_scr[1, n]) * dcy_scr[1, rows, :]
        y = y * _silu(z_ref[rows, :])
        y = y * lax.rsqrt(jnp.mean(y * y, axis=-1, keepdims=True) + EPS) * ng
        o_ref[rows, :] = y.astype(o_ref.dtype)
        return carry

    lax.fori_loop(0, n_chunks, phase_b2, 0, unroll=2)


def _ssd(proj, dtp, p, layer, blk0, n_blk, seq_len, row_len, state=None, prev=None, st_prev=None):
    tb = TOK_BLOCK
    chained = state is not None
    hpg = C_HEADS // C_GROUPS
    gw = hpg * C_HEADDIM
    col_x, col_b, col_c = COL_XBC, COL_XBC + C_INNER, COL_XBC + C_INNER + C_GROUPS * C_STATE
    lc = SSD_L
    tri = np.concatenate([np.tril(np.ones((lc, lc), np.float32)), np.triu(np.ones((lc, lc), np.float32))])
    ecat = np.zeros((2, 128, hpg * lc + gw), np.float32)
    hmask = np.zeros((hpg, 1, gw), np.float32)
    for hh in range(hpg):
        hmask[hh, 0, hh * C_HEADDIM:(hh + 1) * C_HEADDIM] = 1.0
        for d in range(2):
            ecat[d, d * hpg + hh, hh * lc:(hh + 1) * lc] = 1.0
            ecat[d, d * hpg + hh, hpg * lc + hh * C_HEADDIM:hpg * lc + (hh + 1) * C_HEADDIM] = 1.0
    in_specs = [pl.BlockSpec((tb, gw), lambda b, g: (b + blk0, col_x // gw + g)),
                pl.BlockSpec((tb, C_STATE), lambda b, g: (b + blk0, col_b // C_STATE + g)),
                pl.BlockSpec((tb, C_STATE), lambda b, g: (b + blk0, col_c // C_STATE + g)),
                pl.BlockSpec((tb, gw), lambda b, g: (b + blk0, COL_Z // gw + g)),
                pl.BlockSpec((tb, 128), lambda b, g: (b + blk0, g)),
                pl.BlockSpec((None, CONV_W, gw), lambda b, g: (layer, 0, g)),
                pl.BlockSpec((None, CONV_W, C_STATE), lambda b, g: (layer, 0, C_INNER // C_STATE + g)),
                pl.BlockSpec((None, CONV_W, C_STATE), lambda b, g: (layer, 0, C_INNER // C_STATE + C_GROUPS + g)),
                pl.BlockSpec((None, 1, gw), lambda b, g: (layer, 0, g)),
                pl.BlockSpec((None, 1, C_STATE), lambda b, g: (layer, 0, C_INNER // C_STATE + g)),
                pl.BlockSpec((None, 1, C_STATE), lambda b, g: (layer, 0, C_INNER // C_STATE + C_GROUPS + g)),
                pl.BlockSpec((None, None, 1, 128), lambda b, g: (layer, g, 0, 0)),
                pl.BlockSpec((None, None, 1, 128), lambda b, g: (layer, g, 0, 0)),
                pl.BlockSpec((None, 1, gw), lambda b, g: (layer, 0, g)),
                pl.BlockSpec((None, 1, gw), lambda b, g: (layer, 0, g)),
                pl.BlockSpec(tri.shape, lambda b, g: (0, 0)),
                pl.BlockSpec(ecat.shape, lambda b, g: (0, 0, 0)),
                pl.BlockSpec(hmask.shape, lambda b, g: (0, 0, 0))]
    cw = p["conv_c_w"]
    cbias = p["conv_c_b"].reshape(DEPTH, 1, -1)
    args = [proj, proj, proj, proj, dtp, cw, cw, cw, cbias, cbias, cbias,
            p["dt_bias_g"], p["a_log_g"], p["ssd_d_rep"], p["ssd_norm_g"].reshape(DEPTH, 1, C_INNER),
            jnp.asarray(tri, BF16), jnp.asarray(ecat, BF16), jnp.asarray(hmask)]
    out_specs = [pl.BlockSpec((tb, gw), lambda b, g: (b + blk0, g))]
    out_shape = [jax.ShapeDtypeStruct((proj.shape[0], C_INNER), BF16)]
    if chained:
        in_specs.append(pl.BlockSpec((None, None, 2, hpg, C_HEADDIM, C_STATE),
                                     lambda b, g: (b, layer, 0, g, 0, 0)))
        args.append(state)
    else:
        n_seq = tb // seq_len
        out_specs.append(pl.BlockSpec((n_seq, None, 2, hpg, C_HEADDIM, C_STATE),
                                      lambda b, g: (b, layer, 0, g, 0, 0)))
        out_shape.append(jax.ShapeDtypeStruct((n_blk * n_seq, DEPTH, 2, C_HEADS, C_HEADDIM, C_STATE), F32))
    aliases = _alias_out([(0, prev), (1, st_prev)], in_specs, args)
    res = pl.pallas_call(
        functools.partial(_ssd_kernel, seq_len=seq_len, row_len=row_len, chained=chained,
                          aliased=len(aliases)),
        grid=(n_blk, C_GROUPS),
        in_specs=in_specs, out_specs=out_specs, out_shape=out_shape,
        input_output_aliases=aliases,
        scratch_shapes=[pltpu.VMEM((tb, gw), F32), pltpu.VMEM((tb, C_STATE), BF16), pltpu.VMEM((tb, C_STATE), BF16),
                        pltpu.VMEM((tb, 128), F32), pltpu.VMEM((tb, gw), F32), pltpu.VMEM((2, tb, gw), F32),
                        pltpu.VMEM((2, tb // lc, C_STATE, gw), F32), pltpu.VMEM((2, tb // lc, 1, gw), F32),
                        pltpu.VMEM((2, tb // lc, C_STATE, gw), BF16), pltpu.VMEM((2, C_STATE, gw), F32)],
        compiler_params=_cp("arbitrary", "arbitrary"),
        name="ssd",
    )(*args)
    return res[0], (None if chained else res[1])


def _route_kernel(lg_ref, bias_ref, e_ref, w_ref):
    lg = lg_ref[...]
    ne = lg.shape[0]
    epg = ne // N_EXPERT_GROUPS
    mx = jnp.max(lg, axis=0, keepdims=True)
    ex = jnp.exp(lg - mx)
    probs = ex / jnp.sum(ex, axis=0, keepdims=True)
    sel = probs + bias_ref[...]
    rows = [sel[e:e + 1] for e in range(ne)]
    top2 = []
    for e in range(ne):
        g0 = (e // epg) * epg
        rank = jnp.zeros_like(rows[e])
        for o in range(g0, g0 + epg):
            if o == e:
                continue
            ahead = (rows[o] > rows[e]) | ((rows[o] == rows[e]) & (o < e))
            rank = rank + jnp.where(ahead, 1.0, 0.0)
        top2.append(rank < 1.5)
    score = []
    for g in range(N_EXPERT_GROUPS):
        sc = jnp.zeros_like(rows[0])
        for e in range(g * epg, (g + 1) * epg):
            sc = sc + jnp.where(top2[e], rows[e], 0.0)
        score.append(sc)
    best = []
    for g in range(N_EXPERT_GROUPS):
        ok = jnp.ones(rows[0].shape, jnp.bool_)
        for o in range(N_EXPERT_GROUPS):
            if o < g:
                ok = ok & (score[g] > score[o])
            elif o > g:
                ok = ok & (score[g] >= score[o])
        best.append(ok)
    first = jnp.full(rows[0].shape, float(ne), F32)
    second = jnp.full(rows[0].shape, -1.0, F32)
    p_first = jnp.zeros_like(rows[0])
    p_second = jnp.zeros_like(rows[0])
    for e in range(ne - 1, -1, -1):
        ch = top2[e] & best[e // epg]
        first = jnp.where(ch, float(e), first)
        p_first = jnp.where(ch, probs[e:e + 1], p_first)
    for e in range(ne):
        ch = top2[e] & best[e // epg]
        second = jnp.where(ch, float(e), second)
        p_second = jnp.where(ch, probs[e:e + 1], p_second)
    tot = p_first + p_second
    e_ref[0:1, :] = first.astype(jnp.int32)
    e_ref[1:2, :] = second.astype(jnp.int32)
    w_ref[0:1, :] = p_first / tot
    w_ref[1:2, :] = p_second / tot


def _route(logits_t, router_bias):
    ne, n = logits_t.shape
    return pl.pallas_call(
        _route_kernel,
        grid=(1,),
        in_specs=[pl.BlockSpec((ne, n), lambda i: (0, 0)), pl.BlockSpec((ne, 1), lambda i: (0, 0))],
        out_specs=[pl.BlockSpec((2, n), lambda i: (0, 0)), pl.BlockSpec((2, n), lambda i: (0, 0))],
        out_shape=[jax.ShapeDtypeStruct((2, n), jnp.int32), jax.ShapeDtypeStruct((2, n), F32)],
        compiler_params=_cp("arbitrary"),
        name="route",
    )(logits_t, router_bias.reshape(ne, 1))


def _dispatch_plan(e2, tm, n_tiles):
    n = e2.shape[1]
    e_flat = e2.reshape(-1)
    onehot = (e_flat[:, None] == jnp.arange(N_EXPERTS, dtype=jnp.int32)[None, :]).astype(jnp.int32)
    counts = jnp.sum(onehot, axis=0)
    rank = jnp.sum(jnp.cumsum(onehot, axis=0) * onehot, axis=1) - 1
    tiles_per = (counts + tm - 1) // tm
    tile_end = jnp.cumsum(tiles_per)
    starts = (tile_end - tiles_per) * tm
    dest = jnp.sum(onehot * starts[None, :], axis=1) + rank
    src = jnp.zeros((n_tiles * tm,), jnp.int32).at[dest].set(jnp.arange(2 * n, dtype=jnp.int32) % n)
    n_used = tile_end[-1]
    t_idx = jnp.minimum(jnp.arange(n_tiles, dtype=jnp.int32), n_used - 1)
    tile_expert = jnp.sum((tile_end[None, :] <= t_idx[:, None]).astype(jnp.int32), axis=1)
    tile_expert = jnp.minimum(tile_expert, N_EXPERTS - 1)
    te_hot = (tile_expert[:, None] == jnp.arange(N_EXPERTS, dtype=jnp.int32)[None, :]).astype(jnp.int32)
    first_tile = jnp.sum(te_hot * (tile_end - tiles_per)[None, :], axis=1)
    rows = jnp.sum(te_hot * counts[None, :], axis=1) - (jnp.arange(n_tiles, dtype=jnp.int32) - first_tile) * tm
    tile_groups = (jnp.clip(rows, 0, tm) + 7) // 8
    return dest, src, tile_expert, n_used.reshape(1).astype(jnp.int32), tile_groups.astype(jnp.int32)


def _moe_kernel(te_ref, nu_ref, src_ref, tg_ref, h_hbm, wg_ref, wu_ref, wd_ref, o_ref, xbuf, xb_scr, sem, *, tm):
    t, j = pl.program_id(0), pl.program_id(1)
    n_used = nu_ref[0]

    def row_copy(tile, slot, k):
        r = src_ref[tile * tm + k]
        return pltpu.make_async_copy(h_hbm.at[pl.ds(r, 1), :], xbuf.at[slot, pl.ds(k, 1), :], sem.at[slot])

    def issue(tile, slot):
        def body(g, c):
            for u in range(8):
                row_copy(tile, slot, g * 8 + u).start()
            return c
        lax.fori_loop(0, tg_ref[tile], body, 0)

    def drain(tile, slot):
        def body(g, c):
            for u in range(8):
                row_copy(tile, slot, g * 8 + u).wait()
            return c
        lax.fori_loop(0, tg_ref[tile], body, 0)

    @pl.when((j == 0) & (t == 0))
    def _():
        xbuf[...] = jnp.zeros_like(xbuf)
        issue(0, 0)

    @pl.when((j == 0) & (t + 1 < n_used))
    def _():
        issue(t + 1, (t + 1) % 2)

    @pl.when((j == 0) & (t < n_used))
    def _():
        drain(t, t % 2)
        xb_scr[...] = xbuf[t % 2].astype(BF16)

    @pl.when(j == 0)
    def _():
        o_ref[...] = jnp.zeros_like(o_ref)

    def experts(rows):
        x = xb_scr[0:rows, :]
        hg = _dot(x, wg_ref[...].astype(BF16))
        hu = _dot(x, wu_ref[...].astype(BF16))
        act = (_silu(hg) * hu).astype(BF16)
        o_ref[0:rows, :] += _dot(act, wd_ref[...].astype(BF16))

    few = tg_ref[t] * 8 <= tm // 2

    @pl.when((t < n_used) & jnp.logical_not(few))
    def _():
        experts(tm)

    @pl.when((t < n_used) & few)
    def _():
        experts(tm // 2)


def _moe(h, src, tile_expert, n_used, tile_groups, w_gate, w_up, w_down, layer, tm=MOE_TM, tf=MOE_TF):
    d = h.shape[1]
    r = src.shape[0]
    n_tiles = r // tm
    nj = D_EXPERT // tf

    def jj(t, j, nu):
        return jnp.where(t < nu[0], j, nj - 1)

    grid_spec = pltpu.PrefetchScalarGridSpec(
        num_scalar_prefetch=4,
        grid=(n_tiles, nj),
        in_specs=[pl.BlockSpec(memory_space=pl.ANY),
                  pl.BlockSpec((None, None, d, tf), lambda t, j, te, nu, sr, tg: (layer, te[t], 0, jj(t, j, nu))),
                  pl.BlockSpec((None, None, d, tf), lambda t, j, te, nu, sr, tg: (layer, te[t], 0, jj(t, j, nu))),
                  pl.BlockSpec((None, None, tf, d), lambda t, j, te, nu, sr, tg: (layer, te[t], jj(t, j, nu), 0))],
        out_specs=pl.BlockSpec((tm, d), lambda t, j, te, nu, sr, tg: (t, 0)),
        scratch_shapes=[pltpu.VMEM((2, tm, d), F32), pltpu.VMEM((tm, d), BF16), pltpu.SemaphoreType.DMA((2,))],
    )
    return pl.pallas_call(
        functools.partial(_moe_kernel, tm=tm),
        grid_spec=grid_spec,
        out_shape=jax.ShapeDtypeStruct((r, d), F32),
        compiler_params=_cp("arbitrary", "arbitrary"),
        name="moe_experts",
    )(tile_expert, n_used, src, tile_groups, h, w_gate, w_up, w_down)


def _combine_kernel(*refs, tm, n, fuse_norm):
    it = iter(refs)
    dest_ref, x_ref, y_hbm, w_ref, ga_ref = (next(it) for _ in range(5))
    g_ref, sc_ref, sh_ref = (next(it), next(it), next(it)) if fuse_norm else (None, None, None)
    o_ref = next(it)
    h_ref = next(it) if fuse_norm else None
    ybuf, sem = next(it), next(it)
    i = pl.program_id(0)
    nt = pl.num_programs(0)

    def row_copy(tile, slot, s, k):
        r = dest_ref[s * n + tile * tm + k]
        return pltpu.make_async_copy(y_hbm.at[pl.ds(r, 1), :], ybuf.at[slot, s, pl.ds(k, 1), :], sem.at[slot])

    def issue(tile, slot):
        def body(k, c):
            row_copy(tile, slot, 0, k).start()
            row_copy(tile, slot, 1, k).start()
            return c
        lax.fori_loop(0, tm, body, 0, unroll=8)

    def drain(tile, slot):
        def body(k, c):
            row_copy(tile, slot, 0, k).wait()
            row_copy(tile, slot, 1, k).wait()
            return c
        lax.fori_loop(0, tm, body, 0, unroll=8)

    @pl.when(i == 0)
    def _():
        issue(0, 0)

    @pl.when(i + 1 < nt)
    def _():
        issue(i + 1, (i + 1) % 2)

    slot = i % 2
    drain(i, slot)
    w = w_ref[...]
    y = w[:, 0:1] * ybuf[slot, 0] + w[:, 1:2] * ybuf[slot, 1]
    xn = x_ref[...] + ga_ref[...] * y
    o_ref[...] = xn
    if fuse_norm:
        h = xn * lax.rsqrt(jnp.mean(xn * xn, axis=-1, keepdims=True) + EPS) * g_ref[...]
        h_ref[...] = (h * (1.0 + sc_ref[...]) + sh_ref[...]).astype(h_ref.dtype)


def _combine(x, y, dest, w2, mod_l, ga_chunk, next_norm=None, tm=256):
    n, d = x.shape
    row = pl.BlockSpec((tm, d), lambda i, dst: (i, 0))
    in_specs = [row, pl.BlockSpec(memory_space=pl.ANY), pl.BlockSpec((tm, 2), lambda i, dst: (i, 0)),
                pl.BlockSpec((None, 1, d), lambda i, dst: (_cond_row(i, tm), 0, ga_chunk))]
    args = [dest, x, y, w2, mod_l]
    out_specs = [row]
    out_shape = [jax.ShapeDtypeStruct((n, d), F32)]
    if next_norm is not None:
        g, mod_n, (sh_c, sc_c) = next_norm
        in_specs += [pl.BlockSpec((1, d), lambda i, dst: (0, 0)),
                     pl.BlockSpec((None, 1, d), lambda i, dst: (_cond_row(i, tm), 0, sc_c)),
                     pl.BlockSpec((None, 1, d), lambda i, dst: (_cond_row(i, tm), 0, sh_c))]
        args += [g.reshape(1, d), mod_n, mod_n]
        out_specs.append(row)
        out_shape.append(jax.ShapeDtypeStruct((n, d), BF16))
    grid_spec = pltpu.PrefetchScalarGridSpec(
        num_scalar_prefetch=1,
        grid=(n // tm,),
        in_specs=in_specs,
        out_specs=out_specs,
        scratch_shapes=[pltpu.VMEM((2, 2, tm, d), F32), pltpu.SemaphoreType.DMA((2,))],
    )
    res = pl.pallas_call(
        functools.partial(_combine_kernel, tm=tm, n=n, fuse_norm=next_norm is not None),
        grid_spec=grid_spec,
        out_shape=out_shape,
        compiler_params=_cp("arbitrary"),
        name="combine",
    )(*args)
    return (res[0], res[1]) if next_norm is not None else (res[0], None)


def _layer(x, h, l, mod, p, lbs, w_router, router_bias, states, new_states):
    n = x.shape[0]
    mod_l = mod[l]
    state_hgrn, state_rglru, state_ssd = states
    if h is None:
        h = _normmod(x, p["norm1_g"][l], mod_l, chunk=(0, 1))
    proj = _mm_t(h, p["w_in_t"], l, 0, N_MAIN)
    dtp = _mm_t(h, p["w_dt_t"], l, 0, C_GROUPS * 128)
    gates = _mm_t(h, p["w_in_t"], l, COL_MERGE, 3 * D_MODEL, sigmoid=True, out_dtype=BF16)
    nb = n // TOK_BLOCK
    ns = nb - N_CTX_BLOCKS
    hg_prev, ssd_prev = new_states
    oa, hg_new = _hgrn(proj, lbs[l], p["hgrn_norm_g"][l], 0, N_CTX_BLOCKS, SEQ, layer=l, st_prev=hg_prev)
    oa, _ = _hgrn(proj, lbs[l], p["hgrn_norm_g"][l], N_CTX_BLOCKS, ns, TOK_BLOCK, state_hgrn, l, prev=oa)
    ob, lru_new = _lru(proj, p, l, 0, N_CTX_BLOCKS, SEQ)
    ob, _ = _lru(proj, p, l, N_CTX_BLOCKS, ns, GRID_W, state_rglru, prev=ob)
    oc, ssd_new = _ssd(proj, dtp, p, l, 0, N_CTX_BLOCKS, SEQ, SEQ, st_prev=ssd_prev)
    oc, _ = _ssd(proj, dtp, p, l, N_CTX_BLOCKS, ns, TOK_BLOCK, GRID_W, state_ssd, prev=oc)
    merged = _merge(oa, ob, oc, p["w_branch_a"], p["w_branch_b"], p["w_branch_c"], gates, l)
    x = _mm_residual(merged, p["w_out"], l, x, mod_l, 2)
    h2, logits = _normmod(x, p["norm2_g"][l], mod_l, chunk=(3, 4), w_router=w_router, out_dtype=F32)
    e2, w2 = _route(logits.T, router_bias)
    n_tiles = (2 * n) // MOE_TM + N_EXPERTS
    dest, src, tile_expert, n_used, tile_groups = _dispatch_plan(e2, MOE_TM, n_tiles)
    y = _moe(h2, src, tile_expert, n_used, tile_groups, p["w_e_gate"], p["w_e_up"], p["w_e_down"], l)
    next_norm = (p["norm1_g"][l + 1], mod[l + 1], (0, 1)) if l + 1 < DEPTH else None
    x, h_next = _combine(x, y, dest, w2.T, mod_l, 5, next_norm)
    return x, h_next, hg_new, lru_new.transpose(1, 0, 2), ssd_new


def _hgrn_lower_bounds(lb_raw):
    pr = jax.nn.softmax(lb_raw.astype(F32), axis=0)
    cum = jnp.cumsum(pr, axis=0)
    return cum - cum[0]


def kernel(x_prompt, x_sample, state_hgrn, state_rglru, state_ssd, c, c_ctx, w_mod, b_mod, norm1_g, norm2_g, w_in, hgrn_lb, hgrn_norm_g, conv_b_w, conv_b_b, lru_wa, lru_ba, lru_wx, lru_bx, lru_lambda, conv_c_w, conv_c_b, ssd_a_log, ssd_dt_bias, ssd_d, ssd_norm_g, w_branch_a, w_branch_b, w_branch_c, w_out, w_router, router_bias, w_e_gate, w_e_up, w_e_down, final_g):
    bsz, seq, d = x_prompt.shape
    dbsz, dseq, _ = x_sample.shape
    assert seq == SEQ and dseq == TOK_BLOCK and (bsz * seq) == N_CTX_BLOCKS * TOK_BLOCK and d == D_MODEL
    x = jnp.concatenate([x_prompt.reshape(-1, d), x_sample.reshape(-1, d)], axis=0)

    cond8 = jnp.zeros((8, d), F32).at[0].set(c_ctx).at[1:1 + dbsz].set(c)
    mod = _modulation(cond8, w_mod, b_mod).reshape(DEPTH, 8, 1, N_MOD * d)

    hpg = C_HEADS // C_GROUPS
    w_in_t = jnp.swapaxes(w_in, 1, 2)
    w_dt_raw = w_in_t[:, COL_DT:COL_DT + 2 * C_HEADS].reshape(DEPTH, 2, C_GROUPS, hpg, d)
    w_dt_t = jnp.zeros((DEPTH, C_GROUPS, 128, d), F32).at[:, :, :2 * hpg].set(
        w_dt_raw.transpose(0, 2, 1, 3, 4).reshape(DEPTH, C_GROUPS, 2 * hpg, d)).reshape(DEPTH, C_GROUPS * 128, d)

    def group_lanes(a):
        g = a.reshape(DEPTH, 2, C_GROUPS, hpg).transpose(0, 2, 1, 3).reshape(DEPTH, C_GROUPS, 1, 2 * hpg)
        return jnp.zeros((DEPTH, C_GROUPS, 1, 128), F32).at[..., :2 * hpg].set(g)

    p = dict(norm1_g=norm1_g, norm2_g=norm2_g, w_in_t=w_in_t, w_dt_t=w_dt_t, hgrn_norm_g=hgrn_norm_g,
             conv_b_w=conv_b_w, conv_b_b=conv_b_b, lru_wa=lru_wa, lru_ba=lru_ba, lru_wx=lru_wx,
             lru_bx=lru_bx, lru_lambda=lru_lambda, conv_c_w=conv_c_w, conv_c_b=conv_c_b,
             dt_bias_g=group_lanes(ssd_dt_bias), a_log_g=group_lanes(ssd_a_log),
             ssd_d_rep=jnp.repeat(ssd_d, C_HEADDIM, axis=-1).reshape(DEPTH, 1, C_INNER),
             ssd_norm_g=ssd_norm_g, w_branch_a=w_branch_a, w_branch_b=w_branch_b, w_branch_c=w_branch_c,
             w_out=w_out, w_e_gate=w_e_gate, w_e_up=w_e_up, w_e_down=w_e_down)
    lbs = _hgrn_lower_bounds(hgrn_lb)

    h = hg_new = ssd_new = None
    lru_list = []
    for l in range(DEPTH):
        x, h, hg_new, lru_new, ssd_new = _layer(x, h, l, mod, p, lbs, w_router, router_bias,
                                                (state_hgrn, state_rglru, state_ssd), (hg_new, ssd_new))
        lru_list.append(lru_new)
    n_ctx = bsz * seq
    y_prompt = _normmod(x, final_g, out_dtype=F32, row0=0, n_rows=n_ctx).reshape(bsz, seq, d)
    y_sample = _normmod(x, final_g, out_dtype=F32, row0=n_ctx, n_rows=dbsz * dseq).reshape(dbsz, dseq, d)
    return (y_prompt, y_sample, hg_new, jnp.stack(lru_list, axis=1), ssd_new)
```

```python
import functools
import math

import numpy as np
import jax
import jax.numpy as jnp
from jax import lax
from jax.experimental import pallas as pl
from jax.experimental.pallas import tpu as pltpu

F32 = jnp.float32
BF16 = jnp.bfloat16
HIGHEST = lax.Precision.HIGHEST

EPS = 1e-6
D_MODEL = 2048
DEPTH = 4
N_MOD = 6
GRID_W = 64
SEQ = 256
TOK_BLOCK = 2048
N_CTX_BLOCKS = 2

A_HEADS = 8
A_DK = 128
A_DV = 128
A_WIDTH = 1024
HGRN_C = 128

B_WIDTH = 1024
B_BLOCK = 128
LRU_C = 8.0
LRU_SEG = 256

C_INNER = 1024
C_HEADDIM = 64
C_HEADS = 16
C_GROUPS = 4
C_STATE = 128
SSD_L = 128
CONV_W = 4

N_EXPERTS = 16
N_EXPERT_GROUPS = 4
D_EXPERT = 1024
MOE_TM = 1024
MOE_TF = 256

COL_Q, COL_FF, COL_FB, COL_V, COL_GA = 0, 1024, 2048, 3072, 4096
COL_XB, COL_GB, COL_Z, COL_XBC, COL_DT, COL_MERGE = 5120, 6144, 7168, 8192, 10240, 10272
N_MAIN = 10240

VMEM_LIMIT_BYTES = 56 * 1024 * 1024
NEG_BIG = -1e30


def _cp(*sem):
    return pltpu.CompilerParams(dimension_semantics=sem, vmem_limit_bytes=VMEM_LIMIT_BYTES)


def _silu(x):
    return x * jax.nn.sigmoid(x)


def _dot(a, b):
    return jnp.dot(a, b, preferred_element_type=F32)


def _dot_nt(a, b):
    return lax.dot_general(a, b, (((1,), (1,)), ((), ())), preferred_element_type=F32)


def _dot_tn(a, b):
    return lax.dot_general(a, b, (((0,), (0,)), ((), ())), preferred_element_type=F32)


def _dot_hi(a, b):
    return jnp.dot(a, b, preferred_element_type=F32, precision=HIGHEST)


def _cond_row(i, tm):
    return jnp.maximum((i * tm) // TOK_BLOCK - (N_CTX_BLOCKS - 1), 0)


def _mod_kernel(c_ref, w_ref, b_ref, o_ref):
    s = _silu(c_ref[...]).astype(BF16)
    o_ref[...] = _dot(s, w_ref[...].astype(BF16)) + b_ref[...]


def _modulation(cond8, w_mod, b_mod):
    depth, d, n = w_mod.shape
    tn = 1024
    return pl.pallas_call(
        _mod_kernel,
        grid=(depth, n // tn),
        in_specs=[pl.BlockSpec((8, d), lambda l, j: (0, 0)),
                  pl.BlockSpec((None, d, tn), lambda l, j: (l, 0, j)),
                  pl.BlockSpec((None, 1, tn), lambda l, j: (l, 0, j))],
        out_specs=pl.BlockSpec((None, 8, tn), lambda l, j: (l, 0, j)),
        out_shape=jax.ShapeDtypeStruct((depth, 8, n), F32),
        compiler_params=_cp("arbitrary", "arbitrary"),
        name="modulation",
    )(cond8, w_mod, b_mod.reshape(depth, 1, n))


def _normmod_kernel(*refs, modulate, router):
    it = iter(refs)
    x_ref, g_ref = next(it), next(it)
    sc_ref = sh_ref = wr_ref = lg_ref = None
    if modulate:
        sc_ref, sh_ref = next(it), next(it)
    if router:
        wr_ref = next(it)
    h_ref = next(it)
    if router:
        lg_ref = next(it)
    x = x_ref[...]
    h = x * lax.rsqrt(jnp.mean(x * x, axis=-1, keepdims=True) + EPS) * g_ref[...]
    if modulate:
        h = h * (1.0 + sc_ref[...]) + sh_ref[...]
    h_ref[...] = h.astype(h_ref.dtype)
    if router:
        lg_ref[...] = _dot_hi(h, wr_ref[...])


def _normmod(x, g, mod_l=None, chunk=None, w_router=None, out_dtype=BF16, tm=512, row0=0, n_rows=None):
    d = x.shape[1]
    n = x.shape[0] if n_rows is None else n_rows
    r0 = row0 // tm
    modulate = mod_l is not None
    router = w_router is not None
    assert not (modulate and row0)
    in_specs = [pl.BlockSpec((tm, d), lambda i: (i + r0, 0)),
                pl.BlockSpec((1, d), lambda i: (0, 0))]
    args = [x, g.reshape(1, d)]
    if modulate:
        sh_c, sc_c = chunk
        in_specs += [pl.BlockSpec((None, 1, d), lambda i: (_cond_row(i, tm), 0, sc_c)),
                     pl.BlockSpec((None, 1, d), lambda i: (_cond_row(i, tm), 0, sh_c))]
        args += [mod_l, mod_l]
    out_specs = [pl.BlockSpec((tm, d), lambda i: (i, 0))]
    out_shape = [jax.ShapeDtypeStruct((n, d), out_dtype)]
    if router:
        ne = w_router.shape[1]
        in_specs.append(pl.BlockSpec((d, ne), lambda i: (0, 0)))
        args.append(w_router)
        out_specs.append(pl.BlockSpec((tm, ne), lambda i: (i, 0)))
        out_shape.append(jax.ShapeDtypeStruct((n, ne), F32))
    res = pl.pallas_call(
        functools.partial(_normmod_kernel, modulate=modulate, router=router),
        grid=(n // tm,),
        in_specs=in_specs, out_specs=out_specs, out_shape=out_shape,
        compiler_params=_cp("arbitrary"),
        name="normmod",
    )(*args)
    return res if router else res[0]


def _mm_kernel(a_ref, w_ref, o_ref, *, sigmoid):
    acc = _dot_nt(a_ref[...], w_ref[0].astype(BF16))
    if sigmoid:
        acc = jax.nn.sigmoid(acc)
    o_ref[...] = acc.astype(o_ref.dtype)


def _mm_t(a, wt, layer, row0, n_out, sigmoid=False, out_dtype=F32, tm=2048, tn=512):
    m, k = a.shape
    tn = min(tn, n_out)
    assert row0 % 8 == 0 and n_out % tn == 0 and m % tm == 0
    return pl.pallas_call(
        functools.partial(_mm_kernel, sigmoid=sigmoid),
        grid=(m // tm, n_out // tn),
        in_specs=[pl.BlockSpec((tm, k), lambda i, j: (i, 0)),
                  pl.BlockSpec((pl.Element(1), pl.Element(tn), pl.Element(k)),
                               lambda i, j: (layer, pl.multiple_of(row0 + j * tn, 8), 0))],
        out_specs=pl.BlockSpec((tm, tn), lambda i, j: (i, j)),
        out_shape=jax.ShapeDtypeStruct((m, n_out), out_dtype),
        compiler_params=_cp("arbitrary", "arbitrary"),
        name="matmul",
    )(a, wt)


def _mm_res_kernel(a_ref, w_ref, x_ref, ga_ref, o_ref):
    acc = _dot(a_ref[...], w_ref[...].astype(BF16))
    o_ref[...] = x_ref[...] + ga_ref[...] * acc


def _mm_residual(a, w, layer, x, mod_l, ga_chunk, tm=1024, tn=512):
    m, k = a.shape
    n = w.shape[-1]
    gs = n // tn
    return pl.pallas_call(
        _mm_res_kernel,
        grid=(m // tm, n // tn),
        in_specs=[pl.BlockSpec((tm, k), lambda i, j: (i, 0)),
                  pl.BlockSpec((None, k, tn), lambda i, j: (layer, 0, j)),
                  pl.BlockSpec((tm, tn), lambda i, j: (i, j)),
                  pl.BlockSpec((None, 1, tn), lambda i, j: (_cond_row(i, tm), 0, ga_chunk * gs + j))],
        out_specs=pl.BlockSpec((tm, tn), lambda i, j: (i, j)),
        out_shape=jax.ShapeDtypeStruct((m, n), F32),
        compiler_params=_cp("arbitrary", "arbitrary"),
        name="out_proj",
    )(a, w, x, mod_l)


def _merge_kernel(oa_ref, ob_ref, oc_ref, wa_ref, wb_ref, wc_ref, g1_ref, g2_ref, g3_ref, o_ref):
    ya = _dot(oa_ref[...], wa_ref[...].astype(BF16))
    yb = _dot(ob_ref[...], wb_ref[...].astype(BF16))
    yc = _dot(oc_ref[...], wc_ref[...].astype(BF16))
    m = (g1_ref[...].astype(F32) * ya + g2_ref[...].astype(F32) * yb + g3_ref[...].astype(F32) * yc)
    o_ref[...] = m.astype(o_ref.dtype)


def _merge(oa, ob, oc, wa, wb, wc, gates, layer, tm=1024, tn=512):
    m, k = oa.shape
    n = wa.shape[-1]
    gs = n // tn
    a_spec = pl.BlockSpec((tm, k), lambda i, j: (i, 0))
    w_spec = pl.BlockSpec((None, k, tn), lambda i, j: (layer, 0, j))
    return pl.pallas_call(
        _merge_kernel,
        grid=(m // tm, n // tn),
        in_specs=[a_spec, a_spec, a_spec, w_spec, w_spec, w_spec,
                  pl.BlockSpec((tm, tn), lambda i, j: (i, j)),
                  pl.BlockSpec((tm, tn), lambda i, j: (i, j + gs)),
                  pl.BlockSpec((tm, tn), lambda i, j: (i, j + 2 * gs))],
        out_specs=pl.BlockSpec((tm, tn), lambda i, j: (i, j)),
        out_shape=jax.ShapeDtypeStruct((m, n), BF16),
        compiler_params=_cp("arbitrary", "arbitrary"),
        name="merge",
    )(oa, ob, oc, wa, wb, wc, gates, gates, gates)


def _dwconv(x, w, b, row_len):
    t = x.shape[0]
    pos = lax.broadcasted_iota(jnp.int32, x.shape, 0) % row_len
    xm2 = jnp.where(pos >= 2, pltpu.roll(x, 2, axis=0), 0.0)
    xm1 = jnp.where(pos >= 1, pltpu.roll(x, 1, axis=0), 0.0)
    xp1 = jnp.where(pos <= row_len - 2, pltpu.roll(x, t - 1, axis=0), 0.0)
    y = b + xm2 * w[0:1]
    y = y + xm1 * w[1:2]
    y = y + x * w[2:3]
    y = y + xp1 * w[3:4]
    return y


def _hgrn_consts(c):
    nl = int(math.log2(c))
    seg = np.zeros((2, nl + 1, c, c), np.float32)
    msk = np.zeros((2, nl + 1, c, c), np.float32)
    up = np.zeros((2, nl, c, 128), np.float32)
    for d in range(2):
        tt = np.arange(c) if d == 0 else c - 1 - np.arange(c)
        tr, tc = tt[:, None], tt[None, :]
        seg[d, 0] = tc <= tr
        for l in range(nl):
            s = 1 << l
            blk, upper = tt // (2 * s), (tt % (2 * s)) >= s
            mid = (blk * 2 * s + s)[:, None]
            seg_u = (tc >= mid) & (tc <= tr)
            seg_l = (tc >= tr + 1) & (tc <= mid - 1)
            seg[d, l + 1] = np.where(upper[:, None], seg_u, seg_l)
            msk[d, l] = upper[:, None] & (~upper)[None, :] & (blk[:, None] == blk[None, :])
            up[d, l] = upper[:, None]
        msk[d, nl] = np.eye(c)
    return seg.reshape(2, (nl + 1) * c, c), msk, up


def _hgrn_kernel(*refs, seq_len, chained, aliased):
    it = iter(refs)
    q_ref, ff_ref, fb_ref, v_ref, ga_ref, lb_ref, ng_ref = (next(it) for _ in range(7))
    seg_ref, msk_ref, up_ref = next(it), next(it), next(it)
    s0_ref = next(it) if chained else None
    for _ in range(aliased):
        next(it)
    o_ref = next(it)
    sout_ref = None if chained else next(it)
    o_scr, qp_scr, u_scr, dec_scr, snap_scr = (next(it) for _ in range(5))

    c = HGRN_C
    nl = int(math.log2(c))
    tb = q_ref.shape[0]
    n_chunks = tb // c
    n_ch = seq_len // c
    lb = lb_ref[...]
    f_refs = (ff_ref, fb_ref)

    def phase_a(n, carry):
        rows = pl.ds(pl.multiple_of(n * c, c), c)
        q = _silu(q_ref[rows, :]) * (A_DK ** -0.5)
        qb = q.astype(BF16)
        v = v_ref[rows, :].astype(BF16)
        o = jnp.zeros((c, A_DV), F32)
        qps, kps, decs = [], [], []
        for d in range(2):
            f = lb + (1.0 - lb) * jax.nn.sigmoid(f_refs[d][rows, :])
            k = 1.0 - f
            lf = jnp.log(f)
            hi = lf.astype(BF16)
            mid = (lf - hi.astype(F32)).astype(BF16)
            e2 = _dot(seg_ref[d], jnp.concatenate([hi, mid], axis=1))
            e = e2[:, :A_DK] + e2[:, A_DK:]
            b = e[0:c]
            att = msk_ref[d, nl] * _dot_nt(qb, k.astype(BF16))
            for l in range(nl):
                x = (jnp.where(up_ref[d, l] > 0.5, q, k) * jnp.exp(e[(l + 1) * c:(l + 2) * c])).astype(BF16)
                att = att + msk_ref[d, l] * _dot_nt(x, x)
            o = o + _dot(att.astype(BF16), v)
            b_end = b[c - 1:c] if d == 0 else b[0:1]
            qps.append((q * jnp.exp(b)).astype(BF16))
            kps.append((k * jnp.exp(b_end - b)).astype(BF16))
            decs.append(jnp.exp(b_end))
        o_scr[rows, :] = o
        qp_scr[rows, :] = jnp.concatenate(qps, axis=1)
        u_scr[n] = _dot_tn(v, jnp.concatenate(kps, axis=1))
        dec_scr[n] = jnp.concatenate(decs, axis=1)
        return carry

    lax.fori_loop(0, n_chunks, phase_a, 0)

    def init_state(d):
        return s0_ref[d].T if chained else jnp.zeros((A_DV, A_DK), F32)

    def phase_b1(j, carry):
        sf, sb = carry
        nf = j
        nb = n_chunks - 1 - j
        if not chained:
            sf = jnp.where(nf % n_ch == 0, 0.0, sf)
            sb = jnp.where(nb % n_ch == n_ch - 1, 0.0, sb)
        snap_scr[nf, :, 0:A_DK] = sf.astype(BF16)
        snap_scr[nb, :, A_DK:2 * A_DK] = sb.astype(BF16)
        sf = sf * dec_scr[nf][:, 0:A_DK] + u_scr[nf][:, 0:A_DK]
        sb = sb * dec_scr[nb][:, A_DK:2 * A_DK] + u_scr[nb][:, A_DK:2 * A_DK]
        if not chained:
            @pl.when(nf % n_ch == n_ch - 1)
            def _():
                sout_ref[nf // n_ch, 0] = sf.T

            @pl.when(nb % n_ch == 0)
            def _():
                sout_ref[nb // n_ch, 1] = sb.T
        return sf, sb

    lax.fori_loop(0, n_chunks, phase_b1, (init_state(0), init_state(1)))

    ng = ng_ref[...]

    def phase_b2(n, carry):
        rows = pl.ds(pl.multiple_of(n * c, c), c)
        o = o_scr[rows, :] + _dot_nt(qp_scr[rows, :], snap_scr[n])
        o = o * lax.rsqrt(jnp.mean(o * o, axis=-1, keepdims=True) + EPS) * ng
        o_ref[rows, :] = (o * _silu(ga_ref[rows, :])).astype(o_ref.dtype)
        return carry

    lax.fori_loop(0, n_chunks, phase_b2, 0, unroll=4)


def _alias_out(prevs, in_specs, args):
    aliases = {}
    for out_idx, prev in prevs:
        if prev is not None:
            in_specs.append(pl.BlockSpec(memory_space=pl.ANY))
            args.append(prev)
            aliases[len(args) - 1] = out_idx
    return aliases


def _hgrn(proj, lb_l, ng_l, blk0, n_blk, seq_len, state=None, layer=0, prev=None, st_prev=None):
    tb = TOK_BLOCK
    chained = state is not None
    seg, msk, up = _hgrn_consts(HGRN_C)
    cb = lambda col: col // A_DK

    def col_spec(col):
        return pl.BlockSpec((tb, A_DK), lambda b, h: (b + blk0, cb(col) + h))

    def full(a):
        nd = a.ndim
        return pl.BlockSpec(a.shape, lambda b, h: (0,) * nd)

    in_specs = [col_spec(COL_Q), col_spec(COL_FF), col_spec(COL_FB), col_spec(COL_V), col_spec(COL_GA),
                pl.BlockSpec((1, A_DK), lambda b, h: (0, h)),
                pl.BlockSpec((1, A_DV), lambda b, h: (0, 0)),
                full(seg), full(msk), full(up)]
    args = [proj, proj, proj, proj, proj, lb_l.reshape(1, -1), ng_l.reshape(1, -1),
            jnp.asarray(seg, BF16), jnp.asarray(msk), jnp.asarray(up)]
    out_specs = [pl.BlockSpec((tb, A_DV), lambda b, h: (b + blk0, h))]
    out_shape = [jax.ShapeDtypeStruct((proj.shape[0], A_WIDTH), BF16)]
    if chained:
        in_specs.append(pl.BlockSpec((None, None, 2, None, A_DK, A_DV), lambda b, h: (b, layer, 0, h, 0, 0)))
        args.append(state)
    else:
        n_seq = tb // seq_len
        out_specs.append(pl.BlockSpec((n_seq, None, 2, None, A_DK, A_DV), lambda b, h: (b, layer, 0, h, 0, 0)))
        out_shape.append(jax.ShapeDtypeStruct((n_blk * n_seq, DEPTH, 2, A_HEADS, A_DK, A_DV), F32))
    aliases = _alias_out([(0, prev), (1, st_prev)], in_specs, args)
    n_chunks = tb // HGRN_C
    res = pl.pallas_call(
        functools.partial(_hgrn_kernel, seq_len=seq_len, chained=chained, aliased=len(aliases)),
        grid=(n_blk, A_HEADS),
        in_specs=in_specs, out_specs=out_specs, out_shape=out_shape,
        input_output_aliases=aliases,
        scratch_shapes=[pltpu.VMEM((tb, A_DV), F32), pltpu.VMEM((tb, 2 * A_DK), BF16),
                        pltpu.VMEM((n_chunks, A_DV, 2 * A_DK), F32), pltpu.VMEM((n_chunks, 1, 2 * A_DK), F32),
                        pltpu.VMEM((n_chunks, A_DV, 2 * A_DK), BF16)],
        compiler_params=_cp("arbitrary", "arbitrary"),
        name="hgrn2",
    )(*args)
    return res[0], (None if chained else res[1])


def _lru_kernel(*refs, row_len, chained, aliased):
    it = iter(refs)
    x_ref, gb_ref, cw_ref, cb_ref, wa_ref, ba_ref, wx_ref, bx_ref, lam_ref = (next(it) for _ in range(9))
    h0_ref = next(it) if chained else None
    for _ in range(aliased):
        next(it)
    o_ref = next(it)
    hout_ref = None if chained else next(it)
    a_scr, u_scr, h_scr, p_scr = (next(it) for _ in range(4))

    tb = x_ref.shape[0]
    seg = LRU_SEG
    n_seg = tb // seg
    xc = _dwconv(x_ref[...], cw_ref[...], cb_ref[...], row_len)
    xcb = xc.astype(BF16)
    for d in range(2):
        r = jax.nn.sigmoid(_dot(xcb, wa_ref[d].astype(BF16)) + ba_ref[d])
        g = jax.nn.sigmoid(_dot(xcb, wx_ref[d].astype(BF16)) + bx_ref[d])
        log_a = LRU_C * r * jax.nn.log_sigmoid(lam_ref[d])
        a = jnp.exp(log_a)
        a_scr[d] = a
        u_scr[d] = jnp.sqrt(1.0 - a * a) * (g * xc)

    def step(i, carry):
        hf, pf, hb, pb = carry
        tf = i
        tr = seg - 1 - i
        af = a_scr[0, pl.ds(tf, n_seg, stride=seg), :]
        hf = af * hf + u_scr[0, pl.ds(tf, n_seg, stride=seg), :]
        h_scr[0, pl.ds(tf, n_seg, stride=seg), :] = hf
        ab = a_scr[1, pl.ds(tr, n_seg, stride=seg), :]
        hb = ab * hb + u_scr[1, pl.ds(tr, n_seg, stride=seg), :]
        h_scr[1, pl.ds(tr, n_seg, stride=seg), :] = hb
        if chained:
            pf = af * pf
            pb = ab * pb
            p_scr[0, pl.ds(tf, n_seg, stride=seg), :] = pf
            p_scr[1, pl.ds(tr, n_seg, stride=seg), :] = pb
        return hf, pf, hb, pb

    zeros = jnp.zeros((n_seg, B_BLOCK), F32)
    ones = jnp.ones((n_seg, B_BLOCK), F32)
    hf, pf, hb, pb = lax.fori_loop(0, seg, step, (zeros, ones, zeros, ones), unroll=8)

    if chained:
        hin = h0_ref[0:1, :]
        for s in range(n_seg):
            rows = pl.ds(s * seg, seg)
            h_scr[0, rows, :] = h_scr[0, rows, :] + p_scr[0, rows, :] * hin
            hin = hf[s:s + 1] + pf[s:s + 1] * hin
        hin = h0_ref[1:2, :]
        for s in range(n_seg - 1, -1, -1):
            rows = pl.ds(s * seg, seg)
            h_scr[1, rows, :] = h_scr[1, rows, :] + p_scr[1, rows, :] * hin
            hin = hb[s:s + 1] + pb[s:s + 1] * hin
    else:
        hout_ref[0] = hf
        hout_ref[1] = hb
    o_ref[...] = ((h_scr[0] + h_scr[1]) * jax.nn.gelu(gb_ref[...])).astype(o_ref.dtype)


def _lru(proj, p, layer, blk0, n_blk, row_len, state=None, prev=None):
    tb = TOK_BLOCK
    chained = state is not None
    nb = B_WIDTH // B_BLOCK
    cbx, cbg = COL_XB // B_BLOCK, COL_GB // B_BLOCK
    in_specs = [pl.BlockSpec((tb, B_BLOCK), lambda b, n: (b + blk0, cbx + n)),
                pl.BlockSpec((tb, B_BLOCK), lambda b, n: (b + blk0, cbg + n)),
                pl.BlockSpec((None, CONV_W, B_BLOCK), lambda b, n: (layer, 0, n)),
                pl.BlockSpec((None, 1, B_BLOCK), lambda b, n: (layer, 0, n)),
                pl.BlockSpec((None, 2, None, B_BLOCK, B_BLOCK), lambda b, n: (layer, 0, n, 0, 0)),
                pl.BlockSpec((None, 2, 1, B_BLOCK), lambda b, n: (layer, 0, 0, n)),
                pl.BlockSpec((None, 2, None, B_BLOCK, B_BLOCK), lambda b, n: (layer, 0, n, 0, 0)),
                pl.BlockSpec((None, 2, 1, B_BLOCK), lambda b, n: (layer, 0, 0, n)),
                pl.BlockSpec((None, 2, 1, B_BLOCK), lambda b, n: (layer, 0, 0, n))]
    d4 = lambda a: a.reshape(DEPTH, 2, 1, B_WIDTH)
    args = [proj, proj, p["conv_b_w"], p["conv_b_b"].reshape(DEPTH, 1, B_WIDTH),
            p["lru_wa"], d4(p["lru_ba"]), p["lru_wx"], d4(p["lru_bx"]), d4(p["lru_lambda"])]
    out_specs = [pl.BlockSpec((tb, B_BLOCK), lambda b, n: (b + blk0, n))]
    out_shape = [jax.ShapeDtypeStruct((proj.shape[0], B_WIDTH), BF16)]
    if chained:
        in_specs.append(pl.BlockSpec((None, None, 2, B_BLOCK), lambda b, n: (b, layer, 0, n)))
        args.append(state)
    else:
        n_seq = tb // LRU_SEG
        out_specs.append(pl.BlockSpec((2, n_seq, B_BLOCK), lambda b, n: (0, b, n)))
        out_shape.append(jax.ShapeDtypeStruct((2, n_blk * n_seq, B_WIDTH), F32))
    aliases = _alias_out([(0, prev)], in_specs, args)
    res = pl.pallas_call(
        functools.partial(_lru_kernel, row_len=row_len, chained=chained, aliased=len(aliases)),
        grid=(n_blk, nb),
        in_specs=in_specs, out_specs=out_specs, out_shape=out_shape,
        input_output_aliases=aliases,
        scratch_shapes=[pltpu.VMEM((2, tb, B_BLOCK), F32) for _ in range(4)],
        compiler_params=_cp("arbitrary", "arbitrary"),
        name="rglru",
    )(*args)
    return res[0], (None if chained else res[1])


def _ssd_kernel(*refs, seq_len, row_len, chained, aliased):
    it = iter(refs)
    (xs_ref, bm_ref, cm_ref, z_ref, dt_ref, cwx_ref, cwb_ref, cwc_ref, cbx_ref, cbb_ref, cbc_ref,
     dtb_ref, a_ref, dsk_ref, ng_ref, tri_ref, ecat_ref, hmask_ref) = (next(it) for _ in range(18))
    s0_ref = next(it) if chained else None
    for _ in range(aliased):
        next(it)
    o_ref = next(it)
    sout_ref = None if chained else next(it)
    (xs_scr, bm_scr, cm_scr, dt_scr, y_scr, dcy_scr, u_scr, dec_scr, snap_scr,
     st_scr) = (next(it) for _ in range(10))

    lc = SSD_L
    hpg = C_HEADS // C_GROUPS
    gw = hpg * C_HEADDIM
    tb = xs_ref.shape[0]
    n_chunks = tb // lc
    n_ch = seq_len // lc
    xs_scr[...] = _silu(_dwconv(xs_ref[...], cwx_ref[...], cbx_ref[...], row_len))
    bm_scr[...] = _silu(_dwconv(bm_ref[...], cwb_ref[...], cbb_ref[...], row_len)).astype(BF16)
    cm_scr[...] = _silu(_dwconv(cm_ref[...], cwc_ref[...], cbc_ref[...], row_len)).astype(BF16)
    dt_scr[...] = jax.nn.softplus(dt_ref[...] + dtb_ref[...])
    a_row = -jnp.exp(a_ref[...])
    rr = lax.broadcasted_iota(jnp.int32, (lc, lc), 0)
    cc = lax.broadcasted_iota(jnp.int32, (lc, lc), 1)
    causal = (rr >= cc, cc >= rr)
    rep0 = hpg * lc

    def split2(x):
        hi = x.astype(BF16)
        return hi, (x - hi.astype(F32)).astype(BF16)

    def phase_a(n, carry):
        rows = pl.ds(pl.multiple_of(n * lc, lc), lc)
        dtc = dt_scr[rows, :]
        hi, mid = split2(dtc * a_row)
        c2 = _dot(tri_ref[...], jnp.concatenate([hi, mid], axis=1))
        cum2 = c2[:, :128] + c2[:, 128:]
        dhi, dmid = split2(dtc)
        xs = xs_scr[rows, :]
        bm = bm_scr[rows, :]
        cm = cm_scr[rows, :]
        scores = _dot_nt(cm, bm)
        y = dsk_ref[...] * xs
        for d in range(2):
            cum = cum2[d * lc:(d + 1) * lc]
            chi, cmid = split2(cum)
            r4 = _dot(jnp.concatenate([chi, cmid, dhi, dmid], axis=0), ecat_ref[d])
            rep = r4[0:lc] + r4[lc:2 * lc]
            rep64 = rep[:, rep0:]
            dtrep = r4[2 * lc:3 * lc, rep0:] + r4[3 * lc:, rep0:]
            cum_t = cum.T
            ps = []
            for hh in range(hpg):
                ln = d * hpg + hh
                seg = jnp.exp(jnp.where(causal[d], rep[:, lc * hh:lc * (hh + 1)] - cum_t[ln:ln + 1, :], NEG_BIG))
                ps.append((scores * seg).astype(BF16))
            xdt = xs * dtrep
            rhs = jnp.concatenate([(xdt * hmask_ref[hh]).astype(BF16) for hh in range(hpg)], axis=0)
            y = y + _dot(jnp.concatenate(ps, axis=1), rhs)
            end = rep64[lc - 1:lc] if d == 0 else rep64[0:1]
            u_scr[d, n] = _dot_tn(bm, (xdt * jnp.exp(end - rep64)).astype(BF16))
            dec_scr[d, n] = jnp.exp(end)
            dcy_scr[d, rows, :] = jnp.exp(rep64)
        y_scr[rows, :] = y
        return carry

    lax.fori_loop(0, n_chunks, phase_a, 0)

    for d in range(2):
        if chained:
            st_scr[d] = jnp.concatenate([s0_ref[d, hh] for hh in range(hpg)], axis=0).T
        else:
            st_scr[d] = jnp.zeros((C_STATE, gw), F32)

    def phase_b1(j, carry):
        nf = j
        nb = n_chunks - 1 - j
        sf = st_scr[0]
        sb = st_scr[1]
        if not chained:
            sf = jnp.where(nf % n_ch == 0, 0.0, sf)
            sb = jnp.where(nb % n_ch == n_ch - 1, 0.0, sb)
        snap_scr[0, nf] = sf.astype(BF16)
        snap_scr[1, nb] = sb.astype(BF16)
        sf = sf * dec_scr[0, nf] + u_scr[0, nf]
        sb = sb * dec_scr[1, nb] + u_scr[1, nb]
        st_scr[0] = sf
        st_scr[1] = sb
        if not chained:
            @pl.when(nf % n_ch == n_ch - 1)
            def _():
                sft = sf.T
                for hh in range(hpg):
                    sout_ref[nf // n_ch, 0, hh] = sft[hh * C_HEADDIM:(hh + 1) * C_HEADDIM]

            @pl.when(nb % n_ch == 0)
            def _():
                sbt = sb.T
                for hh in range(hpg):
                    sout_ref[nb // n_ch, 1, hh] = sbt[hh * C_HEADDIM:(hh + 1) * C_HEADDIM]
        return carry

    lax.fori_loop(0, n_chunks, phase_b1, 0)

    ng = ng_ref[...]

    def phase_b2(n, carry):
        rows = pl.ds(pl.multiple_of(n * lc, lc), lc)
        cm = cm_scr[rows, :]
        y = y_scr[rows, :] + _dot(cm, snap_scr[0, n]) * dcy_scr[0, rows, :]
        y = y + _dot(cm, snap_scr[1, n]) * dcy_scr[1, rows, :]
        y = y * _silu(z_ref[rows, :])
        y = y * lax.rsqrt(jnp.mean(y * y, axis=-1, keepdims=True) + EPS) * ng
        o_ref[rows, :] = y.astype(o_ref.dtype)
        return carry

    lax.fori_loop(0, n_chunks, phase_b2, 0, unroll=2)


def _ssd(proj, dtp, p, layer, blk0, n_blk, seq_len, row_len, state=None, prev=None, st_prev=None):
    tb = TOK_BLOCK
    chained = state is not None
    hpg = C_HEADS // C_GROUPS
    gw = hpg * C_HEADDIM
    col_x, col_b, col_c = COL_XBC, COL_XBC + C_INNER, COL_XBC + C_INNER + C_GROUPS * C_STATE
    lc = SSD_L
    tri = np.concatenate([np.tril(np.ones((lc, lc), np.float32)), np.triu(np.ones((lc, lc), np.float32))])
    ecat = np.zeros((2, 128, hpg * lc + gw), np.float32)
    hmask = np.zeros((hpg, 1, gw), np.float32)
    for hh in range(hpg):
        hmask[hh, 0, hh * C_HEADDIM:(hh + 1) * C_HEADDIM] = 1.0
        for d in range(2):
            ecat[d, d * hpg + hh, hh * lc:(hh + 1) * lc] = 1.0
            ecat[d, d * hpg + hh, hpg * lc + hh * C_HEADDIM:hpg * lc + (hh + 1) * C_HEADDIM] = 1.0
    in_specs = [pl.BlockSpec((tb, gw), lambda b, g: (b + blk0, col_x // gw + g)),
                pl.BlockSpec((tb, C_STATE), lambda b, g: (b + blk0, col_b // C_STATE + g)),
                pl.BlockSpec((tb, C_STATE), lambda b, g: (b + blk0, col_c // C_STATE + g)),
                pl.BlockSpec((tb, gw), lambda b, g: (b + blk0, COL_Z // gw + g)),
                pl.BlockSpec((tb, 128), lambda b, g: (b + blk0, g)),
                pl.BlockSpec((None, CONV_W, gw), lambda b, g: (layer, 0, g)),
                pl.BlockSpec((None, CONV_W, C_STATE), lambda b, g: (layer, 0, C_INNER // C_STATE + g)),
                pl.BlockSpec((None, CONV_W, C_STATE), lambda b, g: (layer, 0, C_INNER // C_STATE + C_GROUPS + g)),
                pl.BlockSpec((None, 1, gw), lambda b, g: (layer, 0, g)),
                pl.BlockSpec((None, 1, C_STATE), lambda b, g: (layer, 0, C_INNER // C_STATE + g)),
                pl.BlockSpec((None, 1, C_STATE), lambda b, g: (layer, 0, C_INNER // C_STATE + C_GROUPS + g)),
                pl.BlockSpec((None, None, 1, 128), lambda b, g: (layer, g, 0, 0)),
                pl.BlockSpec((None, None, 1, 128), lambda b, g: (layer, g, 0, 0)),
                pl.BlockSpec((None, 1, gw), lambda b, g: (layer, 0, g)),
                pl.BlockSpec((None, 1, gw), lambda b, g: (layer, 0, g)),
                pl.BlockSpec(tri.shape, lambda b, g: (0, 0)),
                pl.BlockSpec(ecat.shape, lambda b, g: (0, 0, 0)),
                pl.BlockSpec(hmask.shape, lambda b, g: (0, 0, 0))]
    cw = p["conv_c_w"]
    cbias = p["conv_c_b"].reshape(DEPTH, 1, -1)
    args = [proj, proj, proj, proj, dtp, cw, cw, cw, cbias, cbias, cbias,
            p["dt_bias_g"], p["a_log_g"], p["ssd_d_rep"], p["ssd_norm_g"].reshape(DEPTH, 1, C_INNER),
            jnp.asarray(tri, BF16), jnp.asarray(ecat, BF16), jnp.asarray(hmask)]
    out_specs = [pl.BlockSpec((tb, gw), lambda b, g: (b + blk0, g))]
    out_shape = [jax.ShapeDtypeStruct((proj.shape[0], C_INNER), BF16)]
    if chained:
        in_specs.append(pl.BlockSpec((None, None, 2, hpg, C_HEADDIM, C_STATE),
                                     lambda b, g: (b, layer, 0, g, 0, 0)))
        args.append(state)
    else:
        n_seq = tb // seq_len
        out_specs.append(pl.BlockSpec((n_seq, None, 2, hpg, C_HEADDIM, C_STATE),
                                      lambda b, g: (b, layer, 0, g, 0, 0)))
        out_shape.append(jax.ShapeDtypeStruct((n_blk * n_seq, DEPTH, 2, C_HEADS, C_HEADDIM, C_STATE), F32))
    aliases = _alias_out([(0, prev), (1, st_prev)], in_specs, args)
    res = pl.pallas_call(
        functools.partial(_ssd_kernel, seq_len=seq_len, row_len=row_len, chained=chained,
                          aliased=len(aliases)),
        grid=(n_blk, C_GROUPS),
        in_specs=in_specs, out_specs=out_specs, out_shape=out_shape,
        input_output_aliases=aliases,
        scratch_shapes=[pltpu.VMEM((tb, gw), F32), pltpu.VMEM((tb, C_STATE), BF16), pltpu.VMEM((tb, C_STATE), BF16),
                        pltpu.VMEM((tb, 128), F32), pltpu.VMEM((tb, gw), F32), pltpu.VMEM((2, tb, gw), F32),
                        pltpu.VMEM((2, tb // lc, C_STATE, gw), F32), pltpu.VMEM((2, tb // lc, 1, gw), F32),
                        pltpu.VMEM((2, tb // lc, C_STATE, gw), BF16), pltpu.VMEM((2, C_STATE, gw), F32)],
        compiler_params=_cp("arbitrary", "arbitrary"),
        name="ssd",
    )(*args)
    return res[0], (None if chained else res[1])


def _route_kernel(lg_ref, bias_ref, e_ref, w_ref):
    lg = lg_ref[...]
    ne = lg.shape[0]
    epg = ne // N_EXPERT_GROUPS
    mx = jnp.max(lg, axis=0, keepdims=True)
    ex = jnp.exp(lg - mx)
    probs = ex / jnp.sum(ex, axis=0, keepdims=True)
    sel = probs + bias_ref[...]
    rows = [sel[e:e + 1] for e in range(ne)]
    top2 = []
    for e in range(ne):
        g0 = (e // epg) * epg
        rank = jnp.zeros_like(rows[e])
        for o in range(g0, g0 + epg):
            if o == e:
                continue
            ahead = (rows[o] > rows[e]) | ((rows[o] == rows[e]) & (o < e))
            rank = rank + jnp.where(ahead, 1.0, 0.0)
        top2.append(rank < 1.5)
    score = []
    for g in range(N_EXPERT_GROUPS):
        sc = jnp.zeros_like(rows[0])
        for e in range(g * epg, (g + 1) * epg):
            sc = sc + jnp.where(top2[e], rows[e], 0.0)
        score.append(sc)
    best = []
    for g in range(N_EXPERT_GROUPS):
        ok = jnp.ones(rows[0].shape, jnp.bool_)
        for o in range(N_EXPERT_GROUPS):
            if o < g:
                ok = ok & (score[g] > score[o])
            elif o > g:
                ok = ok & (score[g] >= score[o])
        best.append(ok)
    first = jnp.full(rows[0].shape, float(ne), F32)
    second = jnp.full(rows[0].shape, -1.0, F32)
    p_first = jnp.zeros_like(rows[0])
    p_second = jnp.zeros_like(rows[0])
    for e in range(ne - 1, -1, -1):
        ch = top2[e] & best[e // epg]
        first = jnp.where(ch, float(e), first)
        p_first = jnp.where(ch, probs[e:e + 1], p_first)
    for e in range(ne):
        ch = top2[e] & best[e // epg]
        second = jnp.where(ch, float(e), second)
        p_second = jnp.where(ch, probs[e:e + 1], p_second)
    tot = p_first + p_second
    e_ref[0:1, :] = first.astype(jnp.int32)
    e_ref[1:2, :] = second.astype(jnp.int32)
    w_ref[0:1, :] = p_first / tot
    w_ref[1:2, :] = p_second / tot


def _route(logits_t, router_bias):
    ne, n = logits_t.shape
    return pl.pallas_call(
        _route_kernel,
        grid=(1,),
        in_specs=[pl.BlockSpec((ne, n), lambda i: (0, 0)), pl.BlockSpec((ne, 1), lambda i: (0, 0))],
        out_specs=[pl.BlockSpec((2, n), lambda i: (0, 0)), pl.BlockSpec((2, n), lambda i: (0, 0))],
        out_shape=[jax.ShapeDtypeStruct((2, n), jnp.int32), jax.ShapeDtypeStruct((2, n), F32)],
        compiler_params=_cp("arbitrary"),
        name="route",
    )(logits_t, router_bias.reshape(ne, 1))


def _dispatch_plan(e2, tm, n_tiles):
    n = e2.shape[1]
    e_flat = e2.reshape(-1)
    onehot = (e_flat[:, None] == jnp.arange(N_EXPERTS, dtype=jnp.int32)[None, :]).astype(jnp.int32)
    counts = jnp.sum(onehot, axis=0)
    rank = jnp.sum(jnp.cumsum(onehot, axis=0) * onehot, axis=1) - 1
    tiles_per = (counts + tm - 1) // tm
    tile_end = jnp.cumsum(tiles_per)
    starts = (tile_end - tiles_per) * tm
    dest = jnp.sum(onehot * starts[None, :], axis=1) + rank
    src = jnp.zeros((n_tiles * tm,), jnp.int32).at[dest].set(jnp.arange(2 * n, dtype=jnp.int32) % n)
    n_used = tile_end[-1]
    t_idx = jnp.minimum(jnp.arange(n_tiles, dtype=jnp.int32), n_used - 1)
    tile_expert = jnp.sum((tile_end[None, :] <= t_idx[:, None]).astype(jnp.int32), axis=1)
    tile_expert = jnp.minimum(tile_expert, N_EXPERTS - 1)
    te_hot = (tile_expert[:, None] == jnp.arange(N_EXPERTS, dtype=jnp.int32)[None, :]).astype(jnp.int32)
    first_tile = jnp.sum(te_hot * (tile_end - tiles_per)[None, :], axis=1)
    rows = jnp.sum(te_hot * counts[None, :], axis=1) - (jnp.arange(n_tiles, dtype=jnp.int32) - first_tile) * tm
    tile_groups = (jnp.clip(rows, 0, tm) + 7) // 8
    return dest, src, tile_expert, n_used.reshape(1).astype(jnp.int32), tile_groups.astype(jnp.int32)


def _moe_kernel(te_ref, nu_ref, src_ref, tg_ref, h_hbm, wg_ref, wu_ref, wd_ref, o_ref, xbuf, xb_scr, sem, *, tm):
    t, j = pl.program_id(0), pl.program_id(1)
    n_used = nu_ref[0]

    def row_copy(tile, slot, k):
        r = src_ref[tile * tm + k]
        return pltpu.make_async_copy(h_hbm.at[pl.ds(r, 1), :], xbuf.at[slot, pl.ds(k, 1), :], sem.at[slot])

    def issue(tile, slot):
        def body(g, c):
            for u in range(8):
                row_copy(tile, slot, g * 8 + u).start()
            return c
        lax.fori_loop(0, tg_ref[tile], body, 0)

    def drain(tile, slot):
        def body(g, c):
            for u in range(8):
                row_copy(tile, slot, g * 8 + u).wait()
            return c
        lax.fori_loop(0, tg_ref[tile], body, 0)

    @pl.when((j == 0) & (t == 0))
    def _():
        xbuf[...] = jnp.zeros_like(xbuf)
        issue(0, 0)

    @pl.when((j == 0) & (t + 1 < n_used))
    def _():
        issue(t + 1, (t + 1) % 2)

    @pl.when((j == 0) & (t < n_used))
    def _():
        drain(t, t % 2)
        xb_scr[...] = xbuf[t % 2].astype(BF16)

    @pl.when(j == 0)
    def _():
        o_ref[...] = jnp.zeros_like(o_ref)

    def experts(rows):
        x = xb_scr[0:rows, :]
        hg = _dot(x, wg_ref[...].astype(BF16))
        hu = _dot(x, wu_ref[...].astype(BF16))
        act = (_silu(hg) * hu).astype(BF16)
        o_ref[0:rows, :] += _dot(act, wd_ref[...].astype(BF16))

    few = tg_ref[t] * 8 <= tm // 2

    @pl.when((t < n_used) & jnp.logical_not(few))
    def _():
        experts(tm)

    @pl.when((t < n_used) & few)
    def _():
        experts(tm // 2)


def _moe(h, src, tile_expert, n_used, tile_groups, w_gate, w_up, w_down, layer, tm=MOE_TM, tf=MOE_TF):
    d = h.shape[1]
    r = src.shape[0]
    n_tiles = r // tm
    nj = D_EXPERT // tf

    def jj(t, j, nu):
        return jnp.where(t < nu[0], j, nj - 1)

    grid_spec = pltpu.PrefetchScalarGridSpec(
        num_scalar_prefetch=4,
        grid=(n_tiles, nj),
        in_specs=[pl.BlockSpec(memory_space=pl.ANY),
                  pl.BlockSpec((None, None, d, tf), lambda t, j, te, nu, sr, tg: (layer, te[t], 0, jj(t, j, nu))),
                  pl.BlockSpec((None, None, d, tf), lambda t, j, te, nu, sr, tg: (layer, te[t], 0, jj(t, j, nu))),
                  pl.BlockSpec((None, None, tf, d), lambda t, j, te, nu, sr, tg: (layer, te[t], jj(t, j, nu), 0))],
        out_specs=pl.BlockSpec((tm, d), lambda t, j, te, nu, sr, tg: (t, 0)),
        scratch_shapes=[pltpu.VMEM((2, tm, d), F32), pltpu.VMEM((tm, d), BF16), pltpu.SemaphoreType.DMA((2,))],
    )
    return pl.pallas_call(
        functools.partial(_moe_kernel, tm=tm),
        grid_spec=grid_spec,
        out_shape=jax.ShapeDtypeStruct((r, d), F32),
        compiler_params=_cp("arbitrary", "arbitrary"),
        name="moe_experts",
    )(tile_expert, n_used, src, tile_groups, h, w_gate, w_up, w_down)


def _combine_kernel(*refs, tm, n, fuse_norm):
    it = iter(refs)
    dest_ref, x_ref, y_hbm, w_ref, ga_ref = (next(it) for _ in range(5))
    g_ref, sc_ref, sh_ref = (next(it), next(it), next(it)) if fuse_norm else (None, None, None)
    o_ref = next(it)
    h_ref = next(it) if fuse_norm else None
    ybuf, sem = next(it), next(it)
    i = pl.program_id(0)
    nt = pl.num_programs(0)

    def row_copy(tile, slot, s, k):
        r = dest_ref[s * n + tile * tm + k]
        return pltpu.make_async_copy(y_hbm.at[pl.ds(r, 1), :], ybuf.at[slot, s, pl.ds(k, 1), :], sem.at[slot])

    def issue(tile, slot):
        def body(k, c):
            row_copy(tile, slot, 0, k).start(priority=0)
            row_copy(tile, slot, 1, k).start(priority=1)
            return c
        lax.fori_loop(0, tm, body, 0, unroll=8)

    def drain(tile, slot):
        def body(k, c):
            row_copy(tile, slot, 0, k).wait()
            row_copy(tile, slot, 1, k).wait()
            return c
        lax.fori_loop(0, tm, body, 0, unroll=8)

    @pl.when(i == 0)
    def _():
        issue(0, 0)

    @pl.when(i + 1 < nt)
    def _():
        issue(i + 1, (i + 1) % 2)

    slot = i % 2
    drain(i, slot)
    w = w_ref[...]
    y = w[:, 0:1] * ybuf[slot, 0] + w[:, 1:2] * ybuf[slot, 1]
    xn = x_ref[...] + ga_ref[...] * y
    o_ref[...] = xn
    if fuse_norm:
        h = xn * lax.rsqrt(jnp.mean(xn * xn, axis=-1, keepdims=True) + EPS) * g_ref[...]
        h_ref[...] = (h * (1.0 + sc_ref[...]) + sh_ref[...]).astype(h_ref.dtype)


def _combine(x, y, dest, w2, mod_l, ga_chunk, next_norm=None, tm=256):
    n, d = x.shape
    row = pl.BlockSpec((tm, d), lambda i, dst: (i, 0))
    in_specs = [row, pl.BlockSpec(memory_space=pl.ANY), pl.BlockSpec((tm, 2), lambda i, dst: (i, 0)),
                pl.BlockSpec((None, 1, d), lambda i, dst: (_cond_row(i, tm), 0, ga_chunk))]
    args = [dest, x, y, w2, mod_l]
    out_specs = [row]
    out_shape = [jax.ShapeDtypeStruct((n, d), F32)]
    if next_norm is not None:
        g, mod_n, (sh_c, sc_c) = next_norm
        in_specs += [pl.BlockSpec((1, d), lambda i, dst: (0, 0)),
                     pl.BlockSpec((None, 1, d), lambda i, dst: (_cond_row(i, tm), 0, sc_c)),
                     pl.BlockSpec((None, 1, d), lambda i, dst: (_cond_row(i, tm), 0, sh_c))]
        args += [g.reshape(1, d), mod_n, mod_n]
        out_specs.append(row)
        out_shape.append(jax.ShapeDtypeStruct((n, d), BF16))
    grid_spec = pltpu.PrefetchScalarGridSpec(
        num_scalar_prefetch=1,
        grid=(n // tm,),
        in_specs=in_specs,
        out_specs=out_specs,
        scratch_shapes=[pltpu.VMEM((2, 2, tm, d), F32), pltpu.SemaphoreType.DMA((2,))],
    )
    res = pl.pallas_call(
        functools.partial(_combine_kernel, tm=tm, n=n, fuse_norm=next_norm is not None),
        grid_spec=grid_spec,
        out_shape=out_shape,
        compiler_params=_cp("arbitrary"),
        name="combine",
    )(*args)
    return (res[0], res[1]) if next_norm is not None else (res[0], None)


def _layer(x, h, l, mod, p, lbs, w_router, router_bias, states, new_states):
    n = x.shape[0]
    mod_l = mod[l]
    state_hgrn, state_rglru, state_ssd = states
    if h is None:
        h = _normmod(x, p["norm1_g"][l], mod_l, chunk=(0, 1))
    proj = _mm_t(h, p["w_in_t"], l, 0, N_MAIN)
    dtp = _mm_t(h, p["w_dt_t"], l, 0, C_GROUPS * 128)
    gates = _mm_t(h, p["w_in_t"], l, COL_MERGE, 3 * D_MODEL, sigmoid=True, out_dtype=BF16)
    nb = n // TOK_BLOCK
    ns = nb - N_CTX_BLOCKS
    hg_prev, ssd_prev = new_states
    oa, hg_new = _hgrn(proj, lbs[l], p["hgrn_norm_g"][l], 0, N_CTX_BLOCKS, SEQ, layer=l, st_prev=hg_prev)
    oa, _ = _hgrn(proj, lbs[l], p["hgrn_norm_g"][l], N_CTX_BLOCKS, ns, TOK_BLOCK, state_hgrn, l, prev=oa)
    ob, lru_new = _lru(proj, p, l, 0, N_CTX_BLOCKS, SEQ)
    ob, _ = _lru(proj, p, l, N_CTX_BLOCKS, ns, GRID_W, state_rglru, prev=ob)
    oc, ssd_new = _ssd(proj, dtp, p, l, 0, N_CTX_BLOCKS, SEQ, SEQ, st_prev=ssd_prev)
    oc, _ = _ssd(proj, dtp, p, l, N_CTX_BLOCKS, ns, TOK_BLOCK, GRID_W, state_ssd, prev=oc)
    merged = _merge(oa, ob, oc, p["w_branch_a"], p["w_branch_b"], p["w_branch_c"], gates, l)
    x = _mm_residual(merged, p["w_out"], l, x, mod_l, 2)
    h2, logits = _normmod(x, p["norm2_g"][l], mod_l, chunk=(3, 4), w_router=w_router, out_dtype=F32)
    e2, w2 = _route(logits.T, router_bias)
    n_tiles = (2 * n) // MOE_TM + N_EXPERTS
    dest, src, tile_expert, n_used, tile_groups = _dispatch_plan(e2, MOE_TM, n_tiles)
    y = _moe(h2, src, tile_expert, n_used, tile_groups, p["w_e_gate"], p["w_e_up"], p["w_e_down"], l)
    next_norm = (p["norm1_g"][l + 1], mod[l + 1], (0, 1)) if l + 1 < DEPTH else None
    x, h_next = _combine(x, y, dest, w2.T, mod_l, 5, next_norm)
    return x, h_next, hg_new, lru_new.transpose(1, 0, 2), ssd_new


def _hgrn_lower_bounds(lb_raw):
    pr = jax.nn.softmax(lb_raw.astype(F32), axis=0)
    cum = jnp.cumsum(pr, axis=0)
    return cum - cum[0]


def kernel(x_prompt, x_sample, state_hgrn, state_rglru, state_ssd, c, c_ctx, w_mod, b_mod, norm1_g, norm2_g, w_in, hgrn_lb, hgrn_norm_g, conv_b_w, conv_b_b, lru_wa, lru_ba, lru_wx, lru_bx, lru_lambda, conv_c_w, conv_c_b, ssd_a_log, ssd_dt_bias, ssd_d, ssd_norm_g, w_branch_a, w_branch_b, w_branch_c, w_out, w_router, router_bias, w_e_gate, w_e_up, w_e_down, final_g):
    bsz, seq, d = x_prompt.shape
    dbsz, dseq, _ = x_sample.shape
    assert seq == SEQ and dseq == TOK_BLOCK and (bsz * seq) == N_CTX_BLOCKS * TOK_BLOCK and d == D_MODEL
    x = jnp.concatenate([x_prompt.reshape(-1, d), x_sample.reshape(-1, d)], axis=0)

    cond8 = jnp.zeros((8, d), F32).at[0].set(c_ctx).at[1:1 + dbsz].set(c)
    mod = _modulation(cond8, w_mod, b_mod).reshape(DEPTH, 8, 1, N_MOD * d)

    hpg = C_HEADS // C_GROUPS
    w_in_t = jnp.swapaxes(w_in, 1, 2)
    w_dt_raw = w_in_t[:, COL_DT:COL_DT + 2 * C_HEADS].reshape(DEPTH, 2, C_GROUPS, hpg, d)
    w_dt_t = jnp.zeros((DEPTH, C_GROUPS, 128, d), F32).at[:, :, :2 * hpg].set(
        w_dt_raw.transpose(0, 2, 1, 3, 4).reshape(DEPTH, C_GROUPS, 2 * hpg, d)).reshape(DEPTH, C_GROUPS * 128, d)

    def group_lanes(a):
        g = a.reshape(DEPTH, 2, C_GROUPS, hpg).transpose(0, 2, 1, 3).reshape(DEPTH, C_GROUPS, 1, 2 * hpg)
        return jnp.zeros((DEPTH, C_GROUPS, 1, 128), F32).at[..., :2 * hpg].set(g)

    p = dict(norm1_g=norm1_g, norm2_g=norm2_g, w_in_t=w_in_t, w_dt_t=w_dt_t, hgrn_norm_g=hgrn_norm_g,
             conv_b_w=conv_b_w, conv_b_b=conv_b_b, lru_wa=lru_wa, lru_ba=lru_ba, lru_wx=lru_wx,
             lru_bx=lru_bx, lru_lambda=lru_lambda, conv_c_w=conv_c_w, conv_c_b=conv_c_b,
             dt_bias_g=group_lanes(ssd_dt_bias), a_log_g=group_lanes(ssd_a_log),
             ssd_d_rep=jnp.repeat(ssd_d, C_HEADDIM, axis=-1).reshape(DEPTH, 1, C_INNER),
             ssd_norm_g=ssd_norm_g, w_branch_a=w_branch_a, w_branch_b=w_branch_b, w_branch_c=w_branch_c,
             w_out=w_out, w_e_gate=w_e_gate, w_e_up=w_e_up, w_e_down=w_e_down)
    lbs = _hgrn_lower_bounds(hgrn_lb)

    h = hg_new = ssd_new = None
    lru_list = []
    for l in range(DEPTH):
        x, h, hg_new, lru_new, ssd_new = _layer(x, h, l, mod, p, lbs, w_router, router_bias,
                                                (state_hgrn, state_rglru, state_ssd), (hg_new, ssd_new))
        lru_list.append(lru_new)
    n_ctx = bsz * seq
    y_prompt = _normmod(x, final_g, out_dtype=F32, row0=0, n_rows=n_ctx).reshape(bsz, seq, d)
    y_sample = _normmod(x, final_g, out_dtype=F32, row0=n_ctx, n_rows=dbsz * dseq).reshape(dbsz, dseq, d)
    return (y_prompt, y_sample, hg_new, jnp.stack(lru_list, axis=1), ssd_new)
```
